```python
import jax, jax.numpy as jnp
from jax import lax
import numpy as np

D_MODEL = 2048
BATCH = 2
SEQ = 16384
DEPTH = 2

CTX_LEN = 256
GRID_W = 64
EPS = 1e-6
N_MOD = 9
D_FF = 5632
HEAD_DIM = 128
ATTN_HEADS = 8
ATTN_KV_HEADS = 2
Q_BLOCK = 128
ROPE_THETA = 10000.0
GLA_HEADS = 4
GLA_DK = 64
GLA_DV = 128
GLA_GATE_RANK = 16
GLA_TAU = 16.0
CHUNK = 128
GMLP_GROUPS = 4
GMLP_GROUP_DIM = 128

ATTN_Q_W = ATTN_HEADS * HEAD_DIM
ATTN_KV_W = ATTN_KV_HEADS * HEAD_DIM
GLA_K_W = GLA_HEADS * GLA_DK
GLA_V_W = GLA_HEADS * GLA_DV
GMLP_W = GMLP_GROUPS * GMLP_GROUP_DIM
MIX_W = ATTN_Q_W + GLA_V_W + GMLP_W
IN_SIZES = (ATTN_Q_W, ATTN_KV_W, ATTN_KV_W, GLA_K_W, GLA_K_W, GLA_V_W, GLA_V_W, 2 * GLA_GATE_RANK, GMLP_W, GMLP_W)
IN_W = 4128

kernel_name = "hybrid_prefix_dit_attn_gla_gmlp"


def _rmsnorm(x, g):
    xf = x.astype(jnp.float32)
    y = xf * lax.rsqrt(jnp.mean(xf * xf, axis=-1, keepdims=True) + EPS)
    return (y * g.astype(jnp.float32)).astype(x.dtype)


def _mod_norm(x, mod, i, g):
    shift = mod[:, 3 * i][:, None, :]
    scale = mod[:, 3 * i + 1][:, None, :]
    return _rmsnorm(x, g) * (1.0 + scale) + shift


def _swiglu(h, w_gu, w_down):
    gate, up = jnp.split(h @ w_gu, 2, axis=-1)
    return (jax.nn.silu(gate) * up) @ w_down


def _ffn_sublayer(x, mod, i, g, w_gu, w_down):
    h = _mod_norm(x, mod, i, g)
    return x + mod[:, 3 * i + 2][:, None, :] * (0.5 * _swiglu(h, w_gu, w_down))


def _heads(z, n):
    b, l, w = z.shape
    return z.reshape(b, l, n, w // n).transpose(0, 2, 1, 3)


def _unheads(z):
    b, n, l, d = z.shape
    return z.transpose(0, 2, 1, 3).reshape(b, l, n * d)


def _rope_tables(length):
    rows = length // GRID_W
    row = jnp.repeat(jnp.arange(rows, dtype=jnp.float32), GRID_W)
    col = jnp.broadcast_to(jnp.arange(GRID_W, dtype=jnp.float32), (rows, GRID_W)).reshape(-1)
    nf = HEAD_DIM // 4
    inv = ROPE_THETA ** (-jnp.arange(nf, dtype=jnp.float32) / nf)
    ang = jnp.concatenate([row[:, None] * inv, col[:, None] * inv], axis=-1)
    return jnp.cos(ang), jnp.sin(ang)


def _apply_rope_2d(x, cos, sin):
    nf = HEAD_DIM // 4
    xf = x.astype(jnp.float32)
    x_row, x_col = jnp.split(xf, 2, axis=-1)

    def rot(z, cs, sn):
        z1, z2 = jnp.split(z, 2, axis=-1)
        return jnp.concatenate([z1 * cs - z2 * sn, z1 * sn + z2 * cs], axis=-1)

    out = jnp.concatenate([rot(x_row, cos[:, :nf], sin[:, :nf]), rot(x_col, cos[:, nf:], sin[:, nf:])], axis=-1)
    return out.astype(x.dtype)


def _attend(q, k, v):
    b, hq, lq, dh = q.shape
    hkv = k.shape[1]
    grp = hq // hkv
    nb = lq // Q_BLOCK
    qb = q.reshape(b, hkv, grp, nb, Q_BLOCK, dh).transpose(3, 0, 1, 2, 4, 5)
    scale = dh ** -0.5

    def block(qblk):
        s = jnp.einsum('bkgqd,bksd->bkgqs', qblk, k, preferred_element_type=jnp.float32) * scale
        p = jax.nn.softmax(s, axis=-1)
        return jnp.einsum('bkgqs,bksd->bkgqd', p.astype(v.dtype), v)

    o = lax.map(block, qb)
    return o.transpose(1, 2, 3, 0, 4, 5).reshape(b, hq, lq, dh)


def _gla_scan(q, k, v, g, s0):
    b, h, l, dk = q.shape
    dv = v.shape[-1]
    n = l // CHUNK

    def to_chunks(z):
        return z.astype(jnp.float32).reshape(b, h, n, CHUNK, z.shape[-1]).transpose(2, 0, 1, 3, 4)

    mask = jnp.tril(jnp.ones((CHUNK, CHUNK), dtype=bool))[:, :, None]

    def step(s, inp):
        qc, kc, vc, gc = inp
        bcum = jnp.cumsum(gc, axis=2)
        diff = bcum[:, :, :, None, :] - bcum[:, :, None, :, :]
        decay = jnp.exp(jnp.where(mask, diff, -jnp.inf))
        a = jnp.einsum('bhid,bhjd,bhijd->bhij', qc, kc, decay)
        o = jnp.einsum('bhid,bhde->bhie', qc * jnp.exp(bcum), s) + jnp.einsum('bhij,bhje->bhie', a, vc)
        b_last = bcum[:, :, -1:, :]
        s_new = jnp.exp(b_last[:, :, 0, :])[..., None] * s + jnp.einsum('bhjd,bhje->bhde', kc * jnp.exp(b_last - bcum), vc)
        return s_new, o

    s_fin, o = lax.scan(step, s0, (to_chunks(q), to_chunks(k), to_chunks(v), to_chunks(g)))
    o = o.transpose(1, 2, 0, 3, 4).reshape(b, h, l, dv)
    return o.astype(v.dtype), s_fin


def _gla_bidir(q, k, v, g_fwd, g_bwd, s0_fwd, s0_bwd):
    o_f, s_f = _gla_scan(q, k, v, g_fwd, s0_fwd)
    flip = lambda z: jnp.flip(z, axis=2)
    o_b, s_b = _gla_scan(flip(q), flip(k), flip(v), flip(g_bwd), s0_bwd)
    return o_f + flip(o_b), s_f, s_b


def _stream_features(h, w_in, qk_g, gate_w, gate_b, rope):
    offs = np.cumsum(IN_SIZES)[:-1].tolist()
    aq, ak, av, gq, gk, gv, gr, glr, mu, mv = jnp.split(h @ w_in, offs, axis=-1)
    q = _rmsnorm(_heads(aq, ATTN_HEADS), qk_g[0])
    k = _rmsnorm(_heads(ak, ATTN_KV_HEADS), qk_g[1])
    if rope is not None:
        q = _apply_rope_2d(q, rope[0], rope[1])
        k = _apply_rope_2d(k, rope[0], rope[1])
    lr_f, lr_b = jnp.split(glr, 2, axis=-1)

    def log_decay(lr, i):
        logits = (lr @ gate_w[i] + gate_b[i]).astype(jnp.float32)
        return _heads(jax.nn.log_sigmoid(logits) / GLA_TAU, GLA_HEADS)

    return {
        "q": q, "k": k, "v": _heads(av, ATTN_KV_HEADS),
        "gq": _heads(gq, GLA_HEADS) * (GLA_DK ** -0.5), "gk": _heads(gk, GLA_HEADS), "gv": _heads(gv, GLA_HEADS),
        "gr": gr, "gf": log_decay(lr_f, 0), "gb": log_decay(lr_b, 1),
        "mu": mu, "mv": mv,
    }


def _gla_out(o, r, g):
    ot = o.transpose(0, 2, 1, 3)
    on = _rmsnorm(ot, g.reshape(GLA_HEADS, GLA_DV))
    b, l = on.shape[:2]
    return on.reshape(b, l, GLA_V_W) * jax.nn.silu(r)


def _chunk_gmlp(u, v, w_s, b_s, g):
    b, l, _ = u.shape
    n = l // CHUNK
    vn = _rmsnorm(v, g).reshape(b, n, CHUNK, GMLP_GROUPS, GMLP_GROUP_DIM)
    z = jnp.einsum('gij,bnjgc->bnigc', w_s, vn) + b_s.T[:, :, None]
    return u * z.reshape(b, l, GMLP_W)


def _token_mixing(h_lat, h_ctx, w_in, w_out, qk_g, gate_w, gate_b, gla_g, w_s, b_s, gm_g, rope, want_ctx):
    fl = _stream_features(h_lat, w_in, qk_g, gate_w, gate_b, rope)
    fc = _stream_features(h_ctx, w_in, qk_g, gate_w, gate_b, None)
    att_l = _attend(fl["q"], jnp.concatenate([fc["k"], fl["k"]], axis=2), jnp.concatenate([fc["v"], fl["v"]], axis=2))
    bsz = h_ctx.shape[0]
    s0 = jnp.zeros((bsz, GLA_HEADS, GLA_DK, GLA_DV), jnp.float32)
    o_c, s_f, s_b = _gla_bidir(fc["gq"], fc["gk"], fc["gv"], fc["gf"], fc["gb"], s0, s0)
    o_l, _, _ = _gla_bidir(fl["gq"], fl["gk"], fl["gv"], fl["gf"], fl["gb"], s_f, s_b)
    y_l = jnp.concatenate([_unheads(att_l), _gla_out(o_l, fl["gr"], gla_g),
                           _chunk_gmlp(fl["mu"], fl["mv"], w_s, b_s, gm_g)], axis=-1) @ w_out
    y_c = None
    if want_ctx:
        att_c = _attend(fc["q"], fc["k"], fc["v"])
        y_c = jnp.concatenate([_unheads(att_c), _gla_out(o_c, fc["gr"], gla_g),
                               _chunk_gmlp(fc["mu"], fc["mv"], w_s, b_s, gm_g)], axis=-1) @ w_out
    return y_l, y_c


def setup_inputs(seed: int = 0) -> dict:
    key = jax.random.key(seed)
    ks = jax.random.split(key, 24)
    D = D_MODEL

    def nrm(k, shape, scale):
        return jax.random.normal(k, shape, jnp.float32) * scale

    return {
        "x": nrm(ks[0], (BATCH, SEQ, D), 1.0),
        "c": nrm(ks[1], (BATCH, D), 1.0),
        "ctx": nrm(ks[2], (BATCH, CTX_LEN, D), 1.0),
        "c_ctx": nrm(ks[3], (D,), 1.0),
        "mod_w": nrm(ks[4], (DEPTH, D, N_MOD * D), 0.5 * D ** -0.5),
        "mod_b": nrm(ks[5], (DEPTH, N_MOD * D), 0.02),
        "norm_g": 1.0 + nrm(ks[6], (DEPTH, 3, D), 0.02),
        "ffn1_w_gu": nrm(ks[7], (DEPTH, D, 2 * D_FF), D ** -0.5),
        "ffn1_w_down": nrm(ks[8], (DEPTH, D_FF, D), D_FF ** -0.5),
        "ffn2_w_gu": nrm(ks[9], (DEPTH, D, 2 * D_FF), D ** -0.5),
        "ffn2_w_down": nrm(ks[10], (DEPTH, D_FF, D), D_FF ** -0.5),
        "w_in": nrm(ks[11], (DEPTH, D, IN_W), D ** -0.5),
        "w_out": nrm(ks[12], (DEPTH, MIX_W, D), MIX_W ** -0.5),
        "qk_norm_g": 1.0 + nrm(ks[13], (DEPTH, 2, HEAD_DIM), 0.02),
        "gla_gate_w": nrm(ks[14], (DEPTH, 2, GLA_GATE_RANK, GLA_K_W), GLA_GATE_RANK ** -0.5),
        "gla_gate_b": 1.0 + nrm(ks[15], (DEPTH, 2, GLA_K_W), 0.1),
        "gla_norm_g": 1.0 + nrm(ks[16], (DEPTH, GLA_V_W), 0.02),
        "gmlp_w_s": nrm(ks[17], (DEPTH, GMLP_GROUPS, CHUNK, CHUNK), 0.5 * CHUNK ** -0.5),
        "gmlp_b_s": 1.0 + nrm(ks[18], (DEPTH, GMLP_GROUPS, CHUNK), 0.02),
        "gmlp_norm_g": 1.0 + nrm(ks[19], (DEPTH, GMLP_W), 0.02),
        "final_norm_g": 1.0 + nrm(ks[20], (D,), 0.02),
    }


def reference(x, c, ctx, c_ctx, mod_w, mod_b, norm_g, ffn1_w_gu, ffn1_w_down, ffn2_w_gu, ffn2_w_down,
              w_in, w_out, qk_norm_g, gla_gate_w, gla_gate_b, gla_norm_g, gmlp_w_s, gmlp_b_s, gmlp_norm_g,
              final_norm_g):
    length = x.shape[1]
    rope = _rope_tables(length)
    sc = jax.nn.silu(c)
    scc = jax.nn.silu(c_ctx)
    xl, xc = x, ctx
    for l in range(DEPTH):
        last = l == DEPTH - 1
        mod_l = (sc @ mod_w[l] + mod_b[l]).reshape(-1, N_MOD, D_MODEL)
        mod_c = (scc @ mod_w[l] + mod_b[l]).reshape(1, N_MOD, D_MODEL)
        xl = _ffn_sublayer(xl, mod_l, 0, norm_g[l, 0], ffn1_w_gu[l], ffn1_w_down[l])
        xc = _ffn_sublayer(xc, mod_c, 0, norm_g[l, 0], ffn1_w_gu[l], ffn1_w_down[l])
        hl = _mod_norm(xl, mod_l, 1, norm_g[l, 1])
        hc = _mod_norm(xc, mod_c, 1, norm_g[l, 1])
        yl, yc = _token_mixing(hl, hc, w_in[l], w_out[l], qk_norm_g[l], gla_gate_w[l], gla_gate_b[l],
                               gla_norm_g[l], gmlp_w_s[l], gmlp_b_s[l], gmlp_norm_g[l], rope, not last)
        xl = xl + mod_l[:, 5][:, None, :] * yl
        xl = _ffn_sublayer(xl, mod_l, 2, norm_g[l, 2], ffn2_w_gu[l], ffn2_w_down[l])
        if not last:
            xc = xc + mod_c[:, 5][:, None, :] * yc
            xc = _ffn_sublayer(xc, mod_c, 2, norm_g[l, 2], ffn2_w_gu[l], ffn2_w_down[l])
    return _rmsnorm(xl, final_norm_g)
```

```python
import functools

import numpy as np
import jax
import jax.numpy as jnp
from jax import lax
from jax.experimental import pallas as pl
from jax.experimental.pallas import tpu as pltpu

F32 = jnp.float32
BF16 = jnp.bfloat16

EPS = 1e-6
N_MOD = 9
HEAD_DIM = 128
ATTN_HEADS = 8
ATTN_KV_HEADS = 2
ATTN_GROUP = ATTN_HEADS // ATTN_KV_HEADS
ROPE_THETA = 10000.0
GRID_W = 64
GLA_HEADS = 4
GLA_DK = 64
GLA_DV = 128
GLA_GATE_RANK = 16
GLA_TAU = 16.0
CHUNK = 128
GMLP_GROUPS = 4
GMLP_GROUP_DIM = 128

ATTN_Q_W = ATTN_HEADS * HEAD_DIM
ATTN_KV_W = ATTN_KV_HEADS * HEAD_DIM
GLA_K_W = GLA_HEADS * GLA_DK
GLA_V_W = GLA_HEADS * GLA_DV
GMLP_W = GMLP_GROUPS * GMLP_GROUP_DIM
LANES = 128
GLR_PAD = LANES
N_LEVELS = 7

OFF_AQ = 0
OFF_AK = OFF_AQ + ATTN_Q_W
OFF_AV = OFF_AK + ATTN_KV_W
OFF_GQ = OFF_AV + ATTN_KV_W
OFF_GK = OFF_GQ + GLA_K_W
OFF_GV = OFF_GK + GLA_K_W
OFF_GR = OFF_GV + GLA_V_W
OFF_MU = OFF_GR + GLA_V_W
OFF_MV = OFF_MU + GMLP_W
OFF_LR = OFF_MV + GMLP_W
IN_W_R = OFF_LR + GLR_PAD

TM = 512
TF = 512
TQ = 256
TK = 512
MOD_TN = 2048
VMEM_LIMIT = 56 * 1024 * 1024


def _sigmoid(x):
    return 1.0 / (1.0 + jnp.exp(-x))


def _silu(x):
    return x * _sigmoid(x)


def _rms(x, g):
    ms = jnp.mean(x * x, axis=-1, keepdims=True)
    return x * lax.rsqrt(ms + EPS) * g


def _norm_mod(x, g, mod_ref, i):
    return _rms(x, g) * (1.0 + mod_ref[0, 3 * i + 1:3 * i + 2, :]) + mod_ref[0, 3 * i:3 * i + 1, :]


def _dot(a, b):
    return jnp.dot(a, b, preferred_element_type=F32)


def _dot_t(a, b):
    return lax.dot_general(a, b, (((1,), (1,)), ((), ())), preferred_element_type=F32)


def _tdot(a, b):
    return lax.dot_general(a, b, (((0,), (0,)), ((), ())), preferred_element_type=F32)


def _mod_kernel(c_ref, w_ref, b_ref, o_ref):
    sc = _silu(c_ref[...]).astype(BF16)
    o_ref[0] = _dot(sc, w_ref[0].astype(BF16)) + b_ref[0]


def _modulation(cc, mod_w, mod_b):
    depth, d, n = mod_w.shape
    return pl.pallas_call(
        _mod_kernel,
        grid=(depth, n // MOD_TN),
        in_specs=[
            pl.BlockSpec((8, d), lambda l, j: (0, 0)),
            pl.BlockSpec((1, d, MOD_TN), lambda l, j: (l, 0, j)),
            pl.BlockSpec((1, 1, MOD_TN), lambda l, j: (l, 0, j)),
        ],
        out_specs=pl.BlockSpec((1, 8, MOD_TN), lambda l, j: (l, 0, j)),
        out_shape=jax.ShapeDtypeStruct((depth, 8, n), F32),
        compiler_params=pltpu.CompilerParams(
            dimension_semantics=("parallel", "parallel"), vmem_limit_bytes=VMEM_LIMIT),
        name="modulation",
    )(cc, mod_w, mod_b.reshape(depth, 1, n))


def _ffn_kernel(x_ref, mod_ref, g_ref, wg_ref, wu_ref, wd_ref, fg_ref, o_ref, h_ref, acc_ref, *, sub, final):
    j = pl.program_id(1)

    @pl.when(j == 0)
    def _():
        h_ref[...] = _norm_mod(x_ref[...], g_ref[...], mod_ref, sub).astype(BF16)
        acc_ref[...] = jnp.zeros_like(acc_ref)

    h = h_ref[...]
    a = _silu(_dot(h, wg_ref[...])) * _dot(h, wu_ref[...])
    acc_ref[...] += _dot(a.astype(BF16), wd_ref[...])

    @pl.when(j == pl.num_programs(1) - 1)
    def _():
        y = x_ref[...] + mod_ref[0, 3 * sub + 2:3 * sub + 3, :] * (0.5 * acc_ref[...])
        if final:
            y = _rms(y, fg_ref[...])
        o_ref[...] = y


def _ffn(xs, mod, g, w_gu, w_down, final_g, *, sub, n_tiles, mod_row, final):
    t, d = xs.shape
    f = w_down.shape[0]
    nf = f // TF
    kern = functools.partial(_ffn_kernel, sub=sub, final=final)
    return pl.pallas_call(
        kern,
        grid=(n_tiles, nf),
        in_specs=[
            pl.BlockSpec((TM, d), lambda i, j: (i, 0)),
            pl.BlockSpec((1, N_MOD, d), lambda i, j: (mod_row(i), 0, 0)),
            pl.BlockSpec((1, d), lambda i, j: (0, 0)),
            pl.BlockSpec((d, TF), lambda i, j: (0, j)),
            pl.BlockSpec((d, TF), lambda i, j: (0, j + nf)),
            pl.BlockSpec((TF, d), lambda i, j: (j, 0)),
            pl.BlockSpec((1, d), lambda i, j: (0, 0)),
        ],
        out_specs=pl.BlockSpec((TM, d), lambda i, j: (i, 0)),
        out_shape=jax.ShapeDtypeStruct((n_tiles * TM, d), F32),
        scratch_shapes=[pltpu.VMEM((TM, d), BF16), pltpu.VMEM((TM, d), F32)],
        compiler_params=pltpu.CompilerParams(
            dimension_semantics=("parallel", "arbitrary"), vmem_limit_bytes=VMEM_LIMIT),
        name="ffn_final" if final else "ffn",
    )(xs, mod, g.reshape(1, d), w_gu, w_gu, w_down, final_g.reshape(1, d))


def _rope(x, cos, sin_signed, lane_low):
    partner = jnp.where(lane_low, pltpu.roll(x, LANES - 32, 1), pltpu.roll(x, 32, 1))
    return x * cos + partner * sin_signed


def _log_sigmoid(x):
    return jnp.minimum(x, 0.0) - jnp.log(1.0 + jnp.exp(-jnp.abs(x)))


def _inproj_kernel(x_ref, mod_ref, g_ref, w_ref, qkg_ref, cos_ref, sin_ref, gw_ref, gbias_ref,
                   ws_ref, bs_ref, gmg_ref,
                   q_ref, k_ref, v_ref, gq_ref, gk_ref, gv_ref, gr_ref, gf_ref, gb_ref, ym_ref):
    h = _norm_mod(x_ref[...], g_ref[...], mod_ref, 1).astype(BF16)
    cos = cos_ref[...]
    sin = sin_ref[...]
    lane = lax.broadcasted_iota(jnp.int32, cos.shape, 1)
    lane_low = (lane % 64) < 32

    def proj(off, width):
        return _dot(h, w_ref[:, off:off + width])

    scale = HEAD_DIM ** -0.5
    zq = proj(OFF_AQ, ATTN_Q_W)
    for hh in range(ATTN_HEADS):
        sl = slice(hh * HEAD_DIM, (hh + 1) * HEAD_DIM)
        qh = _rope(_rms(zq[:, sl], qkg_ref[0:1, :]), cos, sin, lane_low)
        q_ref[:, sl] = (qh * scale).astype(BF16)
    zk = proj(OFF_AK, ATTN_KV_W)
    for hh in range(ATTN_KV_HEADS):
        sl = slice(hh * HEAD_DIM, (hh + 1) * HEAD_DIM)
        k_ref[:, sl] = _rope(_rms(zk[:, sl], qkg_ref[1:2, :]), cos, sin, lane_low).astype(BF16)
    v_ref[...] = proj(OFF_AV, ATTN_KV_W).astype(BF16)

    gq_ref[...] = proj(OFF_GQ, GLA_K_W) * (GLA_DK ** -0.5)
    gk_ref[...] = proj(OFF_GK, GLA_K_W)
    gv_ref[...] = proj(OFF_GV, GLA_V_W).astype(BF16)
    gr_ref[...] = proj(OFF_GR, GLA_V_W)
    lr = proj(OFF_LR, GLR_PAD).astype(BF16)
    logits = _dot(lr, gw_ref[...]) + gbias_ref[...]
    ld = _log_sigmoid(logits) * (1.0 / GLA_TAU)
    gf_ref[...] = ld[:, :GLA_K_W]
    gb_ref[...] = ld[:, GLA_K_W:]

    mu = proj(OFF_MU, GMLP_W)
    vn = _rms(proj(OFF_MV, GMLP_W), gmg_ref[...]).astype(BF16)
    for c in range(x_ref.shape[0] // CHUNK):
        rows = slice(c * CHUNK, (c + 1) * CHUNK)
        for gi in range(GMLP_GROUPS):
            cols = slice(gi * GMLP_GROUP_DIM, (gi + 1) * GMLP_GROUP_DIM)
            z = _dot(ws_ref[gi], vn[rows, cols]) + bs_ref[gi]
            ym_ref[rows, cols] = (mu[rows, cols] * z).astype(BF16)


def _inproj(xs, mod, g, w_in_r, qk_g, cos_t, sin_t, gate_w_r, gate_b_r, ws, bs_b, gm_g, *, mod_row, rope_row):
    t, d = xs.shape
    n_tiles = t // TM
    row = lambda i: (i, 0)
    const2 = lambda i: (0, 0)
    const3 = lambda i: (0, 0, 0)
    widths = [(ATTN_Q_W, BF16), (ATTN_KV_W, BF16), (ATTN_KV_W, BF16), (GLA_K_W, F32), (GLA_K_W, F32),
              (GLA_V_W, BF16), (GLA_V_W, F32), (GLA_K_W, F32), (GLA_K_W, F32), (GMLP_W, BF16)]
    return pl.pallas_call(
        _inproj_kernel,
        grid=(n_tiles,),
        in_specs=[
            pl.BlockSpec((TM, d), row),
            pl.BlockSpec((1, N_MOD, d), lambda i: (mod_row(i), 0, 0)),
            pl.BlockSpec((1, d), const2),
            pl.BlockSpec((d, IN_W_R), const2, pipeline_mode=pl.Buffered(1)),
            pl.BlockSpec((2, HEAD_DIM), const2),
            pl.BlockSpec((TM, HEAD_DIM), lambda i: (rope_row(i), 0)),
            pl.BlockSpec((TM, HEAD_DIM), lambda i: (rope_row(i), 0)),
            pl.BlockSpec((GLR_PAD, 2 * GLA_K_W), const2),
            pl.BlockSpec((1, 2 * GLA_K_W), const2),
            pl.BlockSpec((GMLP_GROUPS, CHUNK, CHUNK), const3),
            pl.BlockSpec((GMLP_GROUPS, CHUNK, GMLP_GROUP_DIM), const3),
            pl.BlockSpec((1, GMLP_W), const2),
        ],
        out_specs=[pl.BlockSpec((TM, w), row) for w, _ in widths],
        out_shape=[jax.ShapeDtypeStruct((t, w), dt) for w, dt in widths],
        compiler_params=pltpu.CompilerParams(
            dimension_semantics=("parallel",), vmem_limit_bytes=VMEM_LIMIT),
        name="inproj",
    )(xs, mod, g.reshape(1, d), w_in_r, qk_g, cos_t, sin_t, gate_w_r, gate_b_r, ws, bs_b, gm_g.reshape(1, GMLP_W))


def _attn_kernel(*refs, n_lat_tiles):
    if n_lat_tiles:
        q_ref, kc_ref, vc_ref, kl_ref, vl_ref, o_ref, m_ref, l_ref, acc_ref = refs
    else:
        q_ref, kc_ref, vc_ref, o_ref, m_ref, l_ref, acc_ref = refs
    tq = q_ref.shape[0]
    q = jnp.concatenate([q_ref[:, g * HEAD_DIM:(g + 1) * HEAD_DIM] for g in range(ATTN_GROUP)], axis=0)

    def update(k, v, first):
        s = _dot_t(q, k)
        smax = jnp.max(s, axis=-1, keepdims=True)
        if first:
            m_new = smax
            p = jnp.exp(s - m_new)
            l_ref[...] = jnp.sum(p, axis=-1, keepdims=True)
            acc_ref[...] = _dot(p.astype(BF16), v)
        else:
            m_old = m_ref[...]
            m_new = jnp.maximum(m_old, smax)
            alpha = jnp.exp(m_old - m_new)
            p = jnp.exp(s - m_new)
            l_ref[...] = alpha * l_ref[...] + jnp.sum(p, axis=-1, keepdims=True)
            acc_ref[...] = alpha * acc_ref[...] + _dot(p.astype(BF16), v)
        m_ref[...] = m_new

    update(kc_ref[...], vc_ref[...], True)
    if n_lat_tiles:
        def body(i, carry):
            start = pl.multiple_of(i * TK, TK)
            update(kl_ref[pl.ds(start, TK), :], vl_ref[pl.ds(start, TK), :], False)
            return carry
        lax.fori_loop(0, n_lat_tiles, body, 0)
    out = acc_ref[...] / l_ref[...]
    for g in range(ATTN_GROUP):
        o_ref[:, g * HEAD_DIM:(g + 1) * HEAD_DIM] = out[g * tq:(g + 1) * tq].astype(BF16)


def _attention(q, k, v, *, batch, n_lat, n_ctx, latent):
    gw = ATTN_GROUP * HEAD_DIM
    ctx_blk0 = (batch * n_lat) // n_ctx
    kc_spec = pl.BlockSpec((n_ctx, HEAD_DIM), lambda b, kh, i: (ctx_blk0 + b, kh))
    if latent:
        tq = TQ
        nq = n_lat // tq
        q_spec = pl.BlockSpec((tq, gw), lambda b, kh, i: (b * nq + i, kh))
        kl_spec = pl.BlockSpec((n_lat, HEAD_DIM), lambda b, kh, i: (b, kh))
        in_specs = [q_spec, kc_spec, kc_spec, kl_spec, kl_spec]
        args = (q, k, v, k, v)
        out_rows = batch * n_lat
        n_lat_tiles = n_lat // TK
    else:
        tq = n_ctx
        nq = 1
        q_spec = pl.BlockSpec((tq, gw), lambda b, kh, i: (ctx_blk0 + b, kh))
        in_specs = [q_spec, kc_spec, kc_spec]
        args = (q, k, v)
        out_rows = batch * n_ctx
        n_lat_tiles = 0
    rows = ATTN_GROUP * tq
    return pl.pallas_call(
        functools.partial(_attn_kernel, n_lat_tiles=n_lat_tiles),
        grid=(batch, ATTN_KV_HEADS, nq),
        in_specs=in_specs,
        out_specs=pl.BlockSpec((tq, gw), lambda b, kh, i: (b * nq + i, kh)),
        out_shape=jax.ShapeDtypeStruct((out_rows, ATTN_Q_W), BF16),
        scratch_shapes=[pltpu.VMEM((rows, 1), F32), pltpu.VMEM((rows, 1), F32), pltpu.VMEM((rows, HEAD_DIM), F32)],
        compiler_params=pltpu.CompilerParams(
            dimension_semantics=("parallel", "parallel", "arbitrary"), vmem_limit_bytes=VMEM_LIMIT),
        name="attn_lat" if latent else "attn_ctx",
    )(*args)


def _gla_consts():
    idx = np.arange(CHUNK)
    tri = (idx[None, :] <= idx[:, None]).astype(np.float32)
    mats_f, mats_b = [tri], [tri.T]
    for lvl in range(1, N_LEVELS + 1):
        s = (2 * CHUNK) >> lvl
        base = (idx // s) * s
        mats_f.append(tri[base + s // 2 - 1])
        mats_b.append(tri.T[base + s // 2])
    cm = np.stack([np.concatenate(mats_f, 0), np.concatenate(mats_b, 0)])
    x = idx[:, None] ^ idx[None, :]
    hb = np.floor(np.log2(np.maximum(x, 1))).astype(np.int32)
    lv = np.where(x == 0, 0, N_LEVELS - hb)
    lv_f = np.where(idx[:, None] >= idx[None, :], lv, -1)
    lv_b = np.where(idx[:, None] <= idx[None, :], lv, -1)
    return cm, np.stack([lv_f, lv_b]).astype(np.int32)


def _gla_chunk(cm_ref, lv_ref, q_ref, k_ref, v_ref, g_ref, o_ref, st_ref, d):
    g = g_ref[...]
    g_hi = g.astype(BF16)
    g_lo = (g - g_hi.astype(F32)).astype(BF16)
    cm = cm_ref[d]
    cums = _dot(cm, g_hi) + _dot(cm, g_lo)
    cum = cums[0:CHUNK]
    q = q_ref[...]
    k = k_ref[...]
    lv = lv_ref[d]
    last = CHUNK - 1 if d == 0 else 0
    tail = cum[last:last + 1, :]
    lane = lax.broadcasted_iota(jnp.int32, (CHUNK, LANES), 1)
    low = lane < GLA_DK

    qs = [q.astype(BF16)]
    ks = [k.astype(BF16)]
    for lvl in range(1, N_LEVELS + 1):
        mid = cums[lvl * CHUNK:(lvl + 1) * CHUNK]
        qs.append((q * jnp.exp(jnp.minimum(cum - mid, 0.0))).astype(BF16))
        ks.append((k * jnp.exp(jnp.minimum(mid - cum, 0.0))).astype(BF16))
    q_in = (q * jnp.exp(cum)).astype(BF16)
    k_out = (k * jnp.exp(tail - cum)).astype(BF16)
    zero = jnp.zeros((CHUNK, LANES), BF16)

    for p in range(GLA_HEADS // 2):
        pl_sl = slice(p * LANES, (p + 1) * LANES)
        st = st_ref[d, p]
        st_b = st.astype(BF16)
        upd = []
        for hp in range(2):
            head = 2 * p + hp
            keep = low if hp == 0 else jnp.logical_not(low)
            a = jnp.zeros((CHUNK, CHUNK), F32)
            for lvl in range(N_LEVELS + 1):
                qm = jnp.where(keep, qs[lvl][:, pl_sl], zero)
                a = jnp.where(lv == lvl, _dot_t(qm, ks[lvl][:, pl_sl]), a)
            vh = v_ref[:, head * GLA_DV:(head + 1) * GLA_DV]
            inter = _dot_t(jnp.where(keep, q_in[:, pl_sl], zero), st_b)
            o_ref[:, head * GLA_DV:(head + 1) * GLA_DV] = inter + _dot(a.astype(BF16), vh)
            upd.append(_tdot(vh, k_out[:, pl_sl]))
        st_ref[d, p] = st * jnp.exp(tail[:, pl_sl]) + jnp.where(low, upd[0], upd[1])


def _gla_kernel(cm_ref, lv_ref, qf, kf, vf, gf, qb, kb, vb, gb, of_ref, ob_ref, st_ref):
    @pl.when(pl.program_id(1) == 0)
    def _():
        st_ref[...] = jnp.zeros_like(st_ref)

    _gla_chunk(cm_ref, lv_ref, qf, kf, vf, gf, of_ref, st_ref, 0)
    _gla_chunk(cm_ref, lv_ref, qb, kb, vb, gb, ob_ref, st_ref, 1)


def _gla(gq, gk, gv, gf, gb, *, batch, n_lat, n_ctx):
    t = gq.shape[0]
    cl, cc = n_lat // CHUNK, n_ctx // CHUNK
    ctx0 = batch * cl
    cm_np, lv_np = _gla_consts()
    cm = jnp.asarray(cm_np, BF16)
    lv = jnp.asarray(lv_np)

    def fwd(b, s):
        return (jnp.where(s < cc, ctx0 + b * cc + s, b * cl + s - cc), 0)

    def bwd(b, s):
        return (jnp.where(s < cc, ctx0 + b * cc + (cc - 1 - s), b * cl + (cl - 1 - (s - cc))), 0)

    def specs(m):
        return [pl.BlockSpec((CHUNK, GLA_K_W), m), pl.BlockSpec((CHUNK, GLA_K_W), m),
                pl.BlockSpec((CHUNK, GLA_V_W), m), pl.BlockSpec((CHUNK, GLA_K_W), m)]

    return pl.pallas_call(
        _gla_kernel,
        grid=(batch, cc + cl),
        in_specs=[pl.BlockSpec(cm.shape, lambda b, s: (0, 0, 0)), pl.BlockSpec(lv.shape, lambda b, s: (0, 0, 0))]
        + specs(fwd) + specs(bwd),
        out_specs=[pl.BlockSpec((CHUNK, GLA_V_W), fwd), pl.BlockSpec((CHUNK, GLA_V_W), bwd)],
        out_shape=[jax.ShapeDtypeStruct((t, GLA_V_W), F32)] * 2,
        scratch_shapes=[pltpu.VMEM((2, GLA_HEADS // 2, GLA_DV, LANES), F32)],
        compiler_params=pltpu.CompilerParams(
            dimension_semantics=("parallel", "arbitrary"), vmem_limit_bytes=VMEM_LIMIT),
        name="gla",
    )(cm, lv, gq, gk, gv, gf, gq, gk, gv, gb)


def _outproj_kernel(x_ref, mod_ref, att_ref, of_ref, ob_ref, gr_ref, ym_ref, gg_ref, w_ref, o_ref):
    o = of_ref[...] + ob_ref[...]
    r = gr_ref[...]
    y = _dot(att_ref[...], w_ref[0:ATTN_Q_W, :])
    for hh in range(GLA_HEADS):
        sl = slice(hh * GLA_DV, (hh + 1) * GLA_DV)
        gh = (_rms(o[:, sl], gg_ref[:, sl]) * _silu(r[:, sl])).astype(BF16)
        y += _dot(gh, w_ref[ATTN_Q_W + hh * GLA_DV:ATTN_Q_W + (hh + 1) * GLA_DV, :])
    y += _dot(ym_ref[...], w_ref[ATTN_Q_W + GLA_V_W:, :])
    o_ref[...] = x_ref[...] + mod_ref[0, 5:6, :] * y


def _outproj(xs, mod, att, o_f, o_b, gr, ym, gla_g, w_out, *, n_tiles, mod_row):
    t, d = xs.shape
    row = lambda i: (i, 0)
    const2 = lambda i: (0, 0)
    return pl.pallas_call(
        _outproj_kernel,
        grid=(n_tiles,),
        in_specs=[
            pl.BlockSpec((TM, d), row),
            pl.BlockSpec((1, N_MOD, d), lambda i: (mod_row(i), 0, 0)),
            pl.BlockSpec((TM, ATTN_Q_W), row),
            pl.BlockSpec((TM, GLA_V_W), row),
            pl.BlockSpec((TM, GLA_V_W), row),
            pl.BlockSpec((TM, GLA_V_W), row),
            pl.BlockSpec((TM, GMLP_W), row),
            pl.BlockSpec((1, GLA_V_W), const2),
            pl.BlockSpec(w_out.shape, const2, pipeline_mode=pl.Buffered(1)),
        ],
        out_specs=pl.BlockSpec((TM, d), row),
        out_shape=jax.ShapeDtypeStruct((n_tiles * TM, d), F32),
        compiler_params=pltpu.CompilerParams(
            dimension_semantics=("parallel",), vmem_limit_bytes=VMEM_LIMIT),
        name="outproj",
    )(xs, mod, att, o_f, o_b, gr, ym, gla_g.reshape(1, GLA_V_W), w_out)


def _rope_tables(n_lat):
    rows = n_lat // GRID_W
    row = jnp.repeat(jnp.arange(rows, dtype=F32), GRID_W)
    col = jnp.broadcast_to(jnp.arange(GRID_W, dtype=F32), (rows, GRID_W)).reshape(-1)
    nf = HEAD_DIM // 4
    inv = ROPE_THETA ** (-jnp.arange(nf, dtype=F32) / nf)
    ar, ac = row[:, None] * inv, col[:, None] * inv
    cos = jnp.concatenate([jnp.cos(ar), jnp.cos(ar), jnp.cos(ac), jnp.cos(ac)], axis=-1)
    sin = jnp.concatenate([-jnp.sin(ar), jnp.sin(ar), -jnp.sin(ac), jnp.sin(ac)], axis=-1)
    cos = jnp.concatenate([cos, jnp.ones((TM, HEAD_DIM), F32)], axis=0)
    sin = jnp.concatenate([sin, jnp.zeros((TM, HEAD_DIM), F32)], axis=0)
    return cos, sin


def kernel(x, c, ctx, c_ctx, mod_w, mod_b, norm_g, ffn1_w_gu, ffn1_w_down, ffn2_w_gu, ffn2_w_down, w_in, w_out,
           qk_norm_g, gla_gate_w, gla_gate_b, gla_norm_g, gmlp_w_s, gmlp_b_s, gmlp_norm_g, final_norm_g):
    batch, n_lat, d = x.shape
    n_ctx = ctx.shape[1]
    depth = mod_w.shape[0]
    assert n_lat % TM == 0 and (batch * n_ctx) % TM == 0 and n_lat % TK == 0 and n_lat % TQ == 0
    assert n_ctx % CHUNK == 0 and n_lat % n_ctx == 0 and batch + 1 <= 8
    lat_tiles = batch * n_lat // TM
    all_tiles = lat_tiles + batch * n_ctx // TM
    tiles_per_batch = n_lat // TM

    def mod_row(i):
        return jnp.minimum(i // tiles_per_batch, batch)

    def rope_row(i):
        return jnp.where(i < lat_tiles, i % tiles_per_batch, tiles_per_batch)

    xs = jnp.concatenate([x.reshape(batch * n_lat, d), ctx.reshape(batch * n_ctx, d)], axis=0)
    cc = jnp.zeros((8, d), F32).at[:batch].set(c).at[batch].set(c_ctx)
    mod_all = _modulation(cc, mod_w, mod_b).reshape(depth, 8, N_MOD, d)
    cos_t, sin_t = _rope_tables(n_lat)

    offs = np.cumsum([0, ATTN_Q_W, ATTN_KV_W, ATTN_KV_W, GLA_K_W, GLA_K_W, GLA_V_W, GLA_V_W, 2 * GLA_GATE_RANK,
                      GMLP_W, GMLP_W])
    lr0, lr1 = int(offs[7]), int(offs[8])
    w_in_r = jnp.concatenate(
        [w_in[:, :, :lr0], w_in[:, :, lr1:], w_in[:, :, lr0:lr1],
         jnp.zeros((depth, d, GLR_PAD - 2 * GLA_GATE_RANK), w_in.dtype)], axis=-1).astype(BF16)
    gate_w_r = jnp.zeros((depth, GLR_PAD, 2 * GLA_K_W), F32)
    gate_w_r = gate_w_r.at[:, :GLA_GATE_RANK, :GLA_K_W].set(gla_gate_w[:, 0])
    gate_w_r = gate_w_r.at[:, GLA_GATE_RANK:2 * GLA_GATE_RANK, GLA_K_W:].set(gla_gate_w[:, 1]).astype(BF16)
    gate_b_r = gla_gate_b.reshape(depth, 1, 2 * GLA_K_W)
    ws_b = gmlp_w_s.astype(BF16)
    bs_b = jnp.broadcast_to(gmlp_b_s[..., None], gmlp_b_s.shape + (GMLP_GROUP_DIM,))
    w_out_b = w_out.astype(BF16)
    f1gu, f1d = ffn1_w_gu.astype(BF16), ffn1_w_down.astype(BF16)
    f2gu, f2d = ffn2_w_gu.astype(BF16), ffn2_w_down.astype(BF16)

    for l in range(depth):
        last = l == depth - 1
        mod = mod_all[l]
        xs = _ffn(xs, mod, norm_g[l, 0], f1gu[l], f1d[l], final_norm_g, sub=0, n_tiles=all_tiles,
                  mod_row=mod_row, final=False)
        q, k, v, gq, gk, gv, gr, gf, gb, ym = _inproj(
            xs, mod, norm_g[l, 1], w_in_r[l], qk_norm_g[l], cos_t, sin_t, gate_w_r[l], gate_b_r[l],
            ws_b[l], bs_b[l], gmlp_norm_g[l], mod_row=mod_row, rope_row=rope_row)
        att = _attention(q, k, v, batch=batch, n_lat=n_lat, n_ctx=n_ctx, latent=True)
        o_f, o_b = _gla(gq, gk, gv, gf, gb, batch=batch, n_lat=n_lat, n_ctx=n_ctx)
        if not last:
            att_c = _attention(q, k, v, batch=batch, n_lat=n_lat, n_ctx=n_ctx, latent=False)
            att = jnp.concatenate([att, att_c], axis=0)
        n_tiles = lat_tiles if last else all_tiles
        xs = _outproj(xs, mod, att, o_f, o_b, gr, ym, gla_norm_g[l], w_out_b[l], n_tiles=n_tiles, mod_row=mod_row)
        xs = _ffn(xs, mod, norm_g[l, 2], f2gu[l], f2d[l], final_norm_g, sub=2, n_tiles=n_tiles,
                  mod_row=mod_row, final=last)
    return xs.reshape(batch, n_lat, d)
```

```python
import functools

import numpy as np
import jax
import jax.numpy as jnp
from jax import lax
from jax.experimental import pallas as pl
from jax.experimental.pallas import tpu as pltpu

F32 = jnp.float32
BF16 = jnp.bfloat16

EPS = 1e-6
N_MOD = 9
HEAD_DIM = 128
ATTN_HEADS = 8
ATTN_KV_HEADS = 2
ATTN_GROUP = ATTN_HEADS // ATTN_KV_HEADS
ROPE_THETA = 10000.0
GRID_W = 64
GLA_HEADS = 4
GLA_DK = 64
GLA_DV = 128
GLA_GATE_RANK = 16
GLA_TAU = 16.0
LOG2E = 1.4426950408889634
CHUNK = 128
GMLP_GROUPS = 4
GMLP_GROUP_DIM = 128

ATTN_Q_W = ATTN_HEADS * HEAD_DIM
ATTN_KV_W = ATTN_KV_HEADS * HEAD_DIM
GLA_K_W = GLA_HEADS * GLA_DK
GLA_V_W = GLA_HEADS * GLA_DV
GMLP_W = GMLP_GROUPS * GMLP_GROUP_DIM
LANES = 128
GLR_PAD = LANES
N_LEVELS = 7

OFF_AQ = 0
OFF_AK = OFF_AQ + ATTN_Q_W
OFF_AV = OFF_AK + ATTN_KV_W
OFF_GQ = OFF_AV + ATTN_KV_W
OFF_GK = OFF_GQ + GLA_K_W
OFF_GV = OFF_GK + GLA_K_W
OFF_GR = OFF_GV + GLA_V_W
OFF_MU = OFF_GR + GLA_V_W
OFF_MV = OFF_MU + GMLP_W
OFF_LR = OFF_MV + GMLP_W
IN_W_R = OFF_LR + GLR_PAD

TM = 512
TF = 512
TQ = 256
TK = 512
MOD_TN = 2048
VMEM_LIMIT = 56 * 1024 * 1024


def _sigmoid(x):
    return 1.0 / (1.0 + jnp.exp(-x))


def _silu(x):
    return x * _sigmoid(x)


def _rms(x, g):
    ms = jnp.mean(x * x, axis=-1, keepdims=True)
    return x * lax.rsqrt(ms + EPS) * g


def _norm_mod(x, g, mod_ref, i):
    return _rms(x, g) * (1.0 + mod_ref[0, 3 * i + 1:3 * i + 2, :]) + mod_ref[0, 3 * i:3 * i + 1, :]


def _dot(a, b):
    return jnp.dot(a, b, preferred_element_type=F32)


def _dot_t(a, b):
    return lax.dot_general(a, b, (((1,), (1,)), ((), ())), preferred_element_type=F32)


def _tdot(a, b):
    return lax.dot_general(a, b, (((0,), (0,)), ((), ())), preferred_element_type=F32)


def _mod_kernel(c_ref, w_ref, b_ref, o_ref):
    sc = _silu(c_ref[...]).astype(BF16)
    o_ref[0] = _dot(sc, w_ref[0].astype(BF16)) + b_ref[0]


def _modulation(cc, mod_w, mod_b):
    depth, d, n = mod_w.shape
    return pl.pallas_call(
        _mod_kernel,
        grid=(depth, n // MOD_TN),
        in_specs=[
            pl.BlockSpec((8, d), lambda l, j: (0, 0)),
            pl.BlockSpec((1, d, MOD_TN), lambda l, j: (l, 0, j)),
            pl.BlockSpec((1, 1, MOD_TN), lambda l, j: (l, 0, j)),
        ],
        out_specs=pl.BlockSpec((1, 8, MOD_TN), lambda l, j: (l, 0, j)),
        out_shape=jax.ShapeDtypeStruct((depth, 8, n), F32),
        compiler_params=pltpu.CompilerParams(
            dimension_semantics=("parallel", "parallel"), vmem_limit_bytes=VMEM_LIMIT),
        name="modulation",
    )(cc, mod_w, mod_b.reshape(depth, 1, n))


def _ffn_kernel(x_ref, mod_ref, g_ref, wg_ref, wu_ref, wd_ref, fg_ref, o_ref, h_ref, acc_ref, *, sub, final):
    j = pl.program_id(1)

    @pl.when(j == 0)
    def _():
        h_ref[...] = _norm_mod(x_ref[...], g_ref[...], mod_ref, sub).astype(BF16)
        acc_ref[...] = jnp.zeros_like(acc_ref)

    h = h_ref[...]
    a = _silu(_dot(h, wg_ref[...])) * _dot(h, wu_ref[...])
    acc_ref[...] += _dot(a.astype(BF16), wd_ref[...])

    @pl.when(j == pl.num_programs(1) - 1)
    def _():
        y = x_ref[...] + mod_ref[0, 3 * sub + 2:3 * sub + 3, :] * (0.5 * acc_ref[...])
        if final:
            y = _rms(y, fg_ref[...])
        o_ref[...] = y


def _ffn(xs, mod, g, w_gu, w_down, final_g, *, sub, n_tiles, mod_row, final):
    t, d = xs.shape
    f = w_down.shape[0]
    nf = f // TF
    kern = functools.partial(_ffn_kernel, sub=sub, final=final)
    return pl.pallas_call(
        kern,
        grid=(n_tiles, nf),
        in_specs=[
            pl.BlockSpec((TM, d), lambda i, j: (i, 0)),
            pl.BlockSpec((1, N_MOD, d), lambda i, j: (mod_row(i), 0, 0)),
            pl.BlockSpec((1, d), lambda i, j: (0, 0)),
            pl.BlockSpec((d, TF), lambda i, j: (0, j)),
            pl.BlockSpec((d, TF), lambda i, j: (0, j + nf)),
            pl.BlockSpec((TF, d), lambda i, j: (j, 0)),
            pl.BlockSpec((1, d), lambda i, j: (0, 0)),
        ],
        out_specs=pl.BlockSpec((TM, d), lambda i, j: (i, 0)),
        out_shape=jax.ShapeDtypeStruct((n_tiles * TM, d), F32),
        scratch_shapes=[pltpu.VMEM((TM, d), BF16), pltpu.VMEM((TM, d), F32)],
        compiler_params=pltpu.CompilerParams(
            dimension_semantics=("parallel", "arbitrary"), vmem_limit_bytes=VMEM_LIMIT),
        name="ffn_final" if final else "ffn",
    )(xs, mod, g.reshape(1, d), w_gu, w_gu, w_down, final_g.reshape(1, d))


def _rope(x, cos, sin_signed, lane_low):
    partner = jnp.where(lane_low, pltpu.roll(x, LANES - 32, 1), pltpu.roll(x, 32, 1))
    return x * cos + partner * sin_signed


def _log_sigmoid(x):
    return jnp.minimum(x, 0.0) - jnp.log(1.0 + jnp.exp(-jnp.abs(x)))


def _inproj_kernel(x_ref, mod_ref, g_ref, w_ref, qkg_ref, cos_ref, sin_ref, gw_ref, gbias_ref,
                   ws_ref, bs_ref, gmg_ref,
                   q_ref, k_ref, vt_ref, gq_ref, gk_ref, gv_ref, gr_ref, gf_ref, gb_ref, ym_ref):
    h = _norm_mod(x_ref[...], g_ref[...], mod_ref, 1).astype(BF16)
    cos = cos_ref[...]
    sin = sin_ref[...]
    lane = lax.broadcasted_iota(jnp.int32, cos.shape, 1)
    lane_low = (lane & 63) < 32

    def proj(off, width):
        return _dot(h, w_ref[:, off:off + width])

    scale = HEAD_DIM ** -0.5 * LOG2E
    zq = proj(OFF_AQ, ATTN_Q_W)
    for hh in range(ATTN_HEADS):
        sl = slice(hh * HEAD_DIM, (hh + 1) * HEAD_DIM)
        qh = _rope(_rms(zq[:, sl], qkg_ref[0:1, :]), cos, sin, lane_low)
        q_ref[:, sl] = (qh * scale).astype(BF16)
    zk = proj(OFF_AK, ATTN_KV_W)
    for hh in range(ATTN_KV_HEADS):
        sl = slice(hh * HEAD_DIM, (hh + 1) * HEAD_DIM)
        k_ref[:, sl] = _rope(_rms(zk[:, sl], qkg_ref[1:2, :]), cos, sin, lane_low).astype(BF16)
    vt_ref[0] = proj(OFF_AV, ATTN_KV_W).T.astype(BF16)

    gq_ref[...] = proj(OFF_GQ, GLA_K_W) * (GLA_DK ** -0.5)
    gk_ref[...] = proj(OFF_GK, GLA_K_W)
    gv_ref[...] = proj(OFF_GV, GLA_V_W).astype(BF16)
    gr_ref[...] = proj(OFF_GR, GLA_V_W)
    lr = proj(OFF_LR, GLR_PAD).astype(BF16)
    logits = _dot(lr, gw_ref[...]) + gbias_ref[...]
    ld = _log_sigmoid(logits) * (1.0 / GLA_TAU)
    gf_ref[...] = ld[:, :GLA_K_W]
    gb_ref[...] = ld[:, GLA_K_W:]

    mu = proj(OFF_MU, GMLP_W)
    vn = _rms(proj(OFF_MV, GMLP_W), gmg_ref[...]).astype(BF16)
    for c in range(x_ref.shape[0] // CHUNK):
        rows = slice(c * CHUNK, (c + 1) * CHUNK)
        for gi in range(GMLP_GROUPS):
            cols = slice(gi * GMLP_GROUP_DIM, (gi + 1) * GMLP_GROUP_DIM)
            z = _dot(ws_ref[gi], vn[rows, cols]) + bs_ref[gi]
            ym_ref[rows, cols] = (mu[rows, cols] * z).astype(BF16)


def _inproj(xs, mod, g, w_in_r, qk_g, cos_t, sin_t, gate_w_r, gate_b_r, ws, bs_b, gm_g, *, mod_row, rope_row):
    t, d = xs.shape
    n_tiles = t // TM
    row = lambda i: (i, 0)
    const2 = lambda i: (0, 0)
    const3 = lambda i: (0, 0, 0)
    widths = [(ATTN_Q_W, BF16), (ATTN_KV_W, BF16), None, (GLA_K_W, F32), (GLA_K_W, F32),
              (GLA_V_W, BF16), (GLA_V_W, F32), (GLA_K_W, F32), (GLA_K_W, F32), (GMLP_W, BF16)]
    out_specs = [pl.BlockSpec((1, ATTN_KV_W, TM), lambda i: (i, 0, 0)) if w is None else pl.BlockSpec((TM, w[0]), row)
                 for w in widths]
    out_shape = [jax.ShapeDtypeStruct((n_tiles, ATTN_KV_W, TM), BF16) if w is None
                 else jax.ShapeDtypeStruct((t, w[0]), w[1]) for w in widths]
    return pl.pallas_call(
        _inproj_kernel,
        grid=(n_tiles,),
        in_specs=[
            pl.BlockSpec((TM, d), row),
            pl.BlockSpec((1, N_MOD, d), lambda i: (mod_row(i), 0, 0)),
            pl.BlockSpec((1, d), const2),
            pl.BlockSpec((d, IN_W_R), const2, pipeline_mode=pl.Buffered(1)),
            pl.BlockSpec((2, HEAD_DIM), const2),
            pl.BlockSpec((TM, HEAD_DIM), lambda i: (rope_row(i), 0)),
            pl.BlockSpec((TM, HEAD_DIM), lambda i: (rope_row(i), 0)),
            pl.BlockSpec((GLR_PAD, 2 * GLA_K_W), const2),
            pl.BlockSpec((1, 2 * GLA_K_W), const2),
            pl.BlockSpec((GMLP_GROUPS, CHUNK, CHUNK), const3),
            pl.BlockSpec((GMLP_GROUPS, CHUNK, GMLP_GROUP_DIM), const3),
            pl.BlockSpec((1, GMLP_W), const2),
        ],
        out_specs=out_specs,
        out_shape=out_shape,
        compiler_params=pltpu.CompilerParams(
            dimension_semantics=("parallel",), vmem_limit_bytes=VMEM_LIMIT),
        name="inproj",
    )(xs, mod, g.reshape(1, d), w_in_r, qk_g, cos_t, sin_t, gate_w_r, gate_b_r, ws, bs_b, gm_g.reshape(1, GMLP_W))


def _attn_kernel(*refs, n_lat_tiles):
    if n_lat_tiles:
        q_ref, kc_ref, vtc_ref, kl_ref, vtl_ref, o_ref, sa_ref, sb_ref, m_ref, l_ref, acc_ref = refs
    else:
        q_ref, kc_ref, vtc_ref, o_ref, m_ref, l_ref, acc_ref = refs
    tq = q_ref.shape[0]
    q = jnp.concatenate([q_ref[:, g * HEAD_DIM:(g + 1) * HEAD_DIM] for g in range(ATTN_GROUP)], axis=0)

    def scores(k):
        return _dot_t(k, q)

    def first_tile(s, vt):
        m_new = jnp.max(s, axis=0, keepdims=True)
        p = jnp.exp2(s - m_new)
        m_ref[...] = m_new
        l_ref[...] = jnp.sum(p, axis=0, keepdims=True)
        acc_ref[...] = _dot(vt, p.astype(BF16))

    def next_tile(s_ref, t):
        m_old = m_ref[...]
        m_new = jnp.maximum(m_old, jnp.max(s_ref[...], axis=0, keepdims=True))
        alpha = jnp.exp2(m_old - m_new)
        p = jnp.exp2(s_ref[...] - m_new)
        m_ref[...] = m_new
        l_ref[...] = alpha * l_ref[...] + jnp.sum(p, axis=0, keepdims=True)
        acc_ref[...] = alpha * acc_ref[...] + _dot(vtl_ref[t], p.astype(BF16))

    def lat_scores(s_ref, t):
        start = pl.multiple_of(t * TK, TK)
        s_ref[...] = scores(kl_ref[pl.ds(start, TK), :])

    s_ctx = scores(kc_ref[...])
    if n_lat_tiles:
        lat_scores(sa_ref, 0)
    first_tile(s_ctx, vtc_ref[0])
    if n_lat_tiles:
        pairs = (n_lat_tiles - 1) // 2

        def body(u, carry):
            lat_scores(sb_ref, 2 * u + 1)
            next_tile(sa_ref, 2 * u)
            lat_scores(sa_ref, 2 * u + 2)
            next_tile(sb_ref, 2 * u + 1)
            return carry
        lax.fori_loop(0, pairs, body, 0)
        if n_lat_tiles - 2 * pairs == 2:
            lat_scores(sb_ref, n_lat_tiles - 1)
            next_tile(sa_ref, n_lat_tiles - 2)
            next_tile(sb_ref, n_lat_tiles - 1)
        else:
            next_tile(sa_ref, n_lat_tiles - 1)
    out_t = acc_ref[...] / l_ref[...]
    for g in range(ATTN_GROUP):
        o_ref[:, g * HEAD_DIM:(g + 1) * HEAD_DIM] = out_t[:, g * tq:(g + 1) * tq].T.astype(BF16)


def _attention(q, k, vt, *, batch, n_lat, n_ctx, latent):
    gw = ATTN_GROUP * HEAD_DIM
    ctx_blk0 = (batch * n_lat) // n_ctx
    lat_tiles = (batch * n_lat) // TK
    per_tile = TK // n_ctx
    kc_spec = pl.BlockSpec((n_ctx, HEAD_DIM), lambda b, kh, i: (ctx_blk0 + b, kh))
    vtc_spec = pl.BlockSpec((1, HEAD_DIM, n_ctx), lambda b, kh, i: (lat_tiles + b // per_tile, kh, b % per_tile))
    if latent:
        tq = TQ
        nq = n_lat // tq
        n_lat_tiles = n_lat // TK
        q_spec = pl.BlockSpec((tq, gw), lambda b, kh, i: (b * nq + i, kh))
        kl_spec = pl.BlockSpec((n_lat, HEAD_DIM), lambda b, kh, i: (b, kh))
        vtl_spec = pl.BlockSpec((n_lat_tiles, HEAD_DIM, TK), lambda b, kh, i: (b, kh, 0))
        in_specs = [q_spec, kc_spec, vtc_spec, kl_spec, vtl_spec]
        args = (q, k, vt, k, vt)
        out_rows = batch * n_lat
    else:
        tq = n_ctx
        nq = 1
        n_lat_tiles = 0
        q_spec = pl.BlockSpec((tq, gw), lambda b, kh, i: (ctx_blk0 + b, kh))
        in_specs = [q_spec, kc_spec, vtc_spec]
        args = (q, k, vt)
        out_rows = batch * n_ctx
    cols = ATTN_GROUP * tq
    scratch = [pltpu.VMEM((1, cols), F32), pltpu.VMEM((1, cols), F32), pltpu.VMEM((HEAD_DIM, cols), F32)]
    if latent:
        scratch = [pltpu.VMEM((TK, cols), F32), pltpu.VMEM((TK, cols), F32)] + scratch
    return pl.pallas_call(
        functools.partial(_attn_kernel, n_lat_tiles=n_lat_tiles),
        grid=(batch, ATTN_KV_HEADS, nq),
        in_specs=in_specs,
        out_specs=pl.BlockSpec((tq, gw), lambda b, kh, i: (b * nq + i, kh)),
        out_shape=jax.ShapeDtypeStruct((out_rows, ATTN_Q_W), BF16),
        scratch_shapes=scratch,
        compiler_params=pltpu.CompilerParams(
            dimension_semantics=("parallel", "parallel", "arbitrary"), vmem_limit_bytes=VMEM_LIMIT),
        name="attn_lat" if latent else "attn_ctx",
    )(*args)


def _gla_consts():
    idx = np.arange(CHUNK)
    tri = (idx[None, :] <= idx[:, None]).astype(np.float32)
    mats_f, mats_b = [tri], [tri.T]
    for lvl in range(1, N_LEVELS + 1):
        s = (2 * CHUNK) >> lvl
        base = (idx // s) * s
        mats_f.append(tri[base + s // 2 - 1])
        mats_b.append(tri.T[base + s // 2])
    cm = np.stack([np.concatenate(mats_f, 0), np.concatenate(mats_b, 0)])
    x = idx[:, None] ^ idx[None, :]
    hb = np.floor(np.log2(np.maximum(x, 1))).astype(np.int32)
    lv = np.where(x == 0, 0, N_LEVELS - hb)
    lv_f = np.where(idx[:, None] >= idx[None, :], lv, -1)
    lv_b = np.where(idx[:, None] <= idx[None, :], lv, -1)
    return cm, np.stack([lv_f, lv_b]).astype(np.int32)


def _gla_chunk(cm_ref, lv_ref, q_ref, k_ref, v_ref, g_ref, o_ref, st_ref, d):
    g = g_ref[...]
    g_hi = g.astype(BF16)
    g_lo = (g - g_hi.astype(F32)).astype(BF16)
    cm = cm_ref[d]
    cums = _dot(cm, g_hi) + _dot(cm, g_lo)
    cum = cums[0:CHUNK]
    q = q_ref[...]
    k = k_ref[...]
    lv = lv_ref[d]
    last = CHUNK - 1 if d == 0 else 0
    tail = cum[last:last + 1, :]
    lane = lax.broadcasted_iota(jnp.int32, (CHUNK, LANES), 1)
    low = lane < GLA_DK

    qs = [q.astype(BF16)]
    ks = [k.astype(BF16)]
    for lvl in range(1, N_LEVELS + 1):
        mid = cums[lvl * CHUNK:(lvl + 1) * CHUNK]
        qs.append((q * jnp.exp(jnp.minimum(cum - mid, 0.0))).astype(BF16))
        ks.append((k * jnp.exp(jnp.minimum(mid - cum, 0.0))).astype(BF16))
    q_in = (q * jnp.exp(cum)).astype(BF16)
    k_out = (k * jnp.exp(tail - cum)).astype(BF16)
    zero = jnp.zeros((CHUNK, LANES), BF16)

    for p in range(GLA_HEADS // 2):
        pl_sl = slice(p * LANES, (p + 1) * LANES)
        st = st_ref[d, p]
        st_b = st.astype(BF16)
        upd = []
        for hp in range(2):
            head = 2 * p + hp
            keep = low if hp == 0 else jnp.logical_not(low)
            a = jnp.zeros((CHUNK, CHUNK), F32)
            for lvl in range(N_LEVELS + 1):
                qm = jnp.where(keep, qs[lvl][:, pl_sl], zero)
                a = jnp.where(lv == lvl, _dot_t(qm, ks[lvl][:, pl_sl]), a)
            vh = v_ref[:, head * GLA_DV:(head + 1) * GLA_DV]
            inter = _dot_t(jnp.where(keep, q_in[:, pl_sl], zero), st_b)
            o_ref[:, head * GLA_DV:(head + 1) * GLA_DV] = inter + _dot(a.astype(BF16), vh)
            upd.append(_tdot(vh, k_out[:, pl_sl]))
        st_ref[d, p] = st * jnp.exp(tail[:, pl_sl]) + jnp.where(low, upd[0], upd[1])


def _gla_kernel(cm_ref, lv_ref, qf, kf, vf, gf, qb, kb, vb, gb, of_ref, ob_ref, st_ref):
    @pl.when(pl.program_id(1) == 0)
    def _():
        st_ref[...] = jnp.zeros_like(st_ref)

    _gla_chunk(cm_ref, lv_ref, qf, kf, vf, gf, of_ref, st_ref, 0)
    _gla_chunk(cm_ref, lv_ref, qb, kb, vb, gb, ob_ref, st_ref, 1)


def _gla(gq, gk, gv, gf, gb, *, batch, n_lat, n_ctx):
    t = gq.shape[0]
    cl, cc = n_lat // CHUNK, n_ctx // CHUNK
    ctx0 = batch * cl
    cm_np, lv_np = _gla_consts()
    cm = jnp.asarray(cm_np, BF16)
    lv = jnp.asarray(lv_np)

    def fwd(b, s):
        return (jnp.where(s < cc, ctx0 + b * cc + s, b * cl + s - cc), 0)

    def bwd(b, s):
        return (jnp.where(s < cc, ctx0 + b * cc + (cc - 1 - s), b * cl + (cl - 1 - (s - cc))), 0)

    def specs(m):
        return [pl.BlockSpec((CHUNK, GLA_K_W), m), pl.BlockSpec((CHUNK, GLA_K_W), m),
                pl.BlockSpec((CHUNK, GLA_V_W), m), pl.BlockSpec((CHUNK, GLA_K_W), m)]

    return pl.pallas_call(
        _gla_kernel,
        grid=(batch, cc + cl),
        in_specs=[pl.BlockSpec(cm.shape, lambda b, s: (0, 0, 0)), pl.BlockSpec(lv.shape, lambda b, s: (0, 0, 0))]
        + specs(fwd) + specs(bwd),
        out_specs=[pl.BlockSpec((CHUNK, GLA_V_W), fwd), pl.BlockSpec((CHUNK, GLA_V_W), bwd)],
        out_shape=[jax.ShapeDtypeStruct((t, GLA_V_W), F32)] * 2,
        scratch_shapes=[pltpu.VMEM((2, GLA_HEADS // 2, GLA_DV, LANES), F32)],
        compiler_params=pltpu.CompilerParams(
            dimension_semantics=("parallel", "arbitrary"), vmem_limit_bytes=VMEM_LIMIT),
        name="gla",
    )(cm, lv, gq, gk, gv, gf, gq, gk, gv, gb)


def _outproj_kernel(x_ref, mod_ref, att_ref, of_ref, ob_ref, gr_ref, ym_ref, gg_ref, w_ref, o_ref):
    o = of_ref[...] + ob_ref[...]
    r = gr_ref[...]
    y = _dot(att_ref[...], w_ref[0:ATTN_Q_W, :])
    for hh in range(GLA_HEADS):
        sl = slice(hh * GLA_DV, (hh + 1) * GLA_DV)
        gh = (_rms(o[:, sl], gg_ref[:, sl]) * _silu(r[:, sl])).astype(BF16)
        y += _dot(gh, w_ref[ATTN_Q_W + hh * GLA_DV:ATTN_Q_W + (hh + 1) * GLA_DV, :])
    y += _dot(ym_ref[...], w_ref[ATTN_Q_W + GLA_V_W:, :])
    o_ref[...] = x_ref[...] + mod_ref[0, 5:6, :] * y


def _outproj(xs, mod, att, o_f, o_b, gr, ym, gla_g, w_out, *, n_tiles, mod_row):
    t, d = xs.shape
    row = lambda i: (i, 0)
    const2 = lambda i: (0, 0)
    return pl.pallas_call(
        _outproj_kernel,
        grid=(n_tiles,),
        in_specs=[
            pl.BlockSpec((TM, d), row),
            pl.BlockSpec((1, N_MOD, d), lambda i: (mod_row(i), 0, 0)),
            pl.BlockSpec((TM, ATTN_Q_W), row),
            pl.BlockSpec((TM, GLA_V_W), row),
            pl.BlockSpec((TM, GLA_V_W), row),
            pl.BlockSpec((TM, GLA_V_W), row),
            pl.BlockSpec((TM, GMLP_W), row),
            pl.BlockSpec((1, GLA_V_W), const2),
            pl.BlockSpec(w_out.shape, const2, pipeline_mode=pl.Buffered(1)),
        ],
        out_specs=pl.BlockSpec((TM, d), row),
        out_shape=jax.ShapeDtypeStruct((n_tiles * TM, d), F32),
        compiler_params=pltpu.CompilerParams(
            dimension_semantics=("parallel",), vmem_limit_bytes=VMEM_LIMIT),
        name="outproj",
    )(xs, mod, att, o_f, o_b, gr, ym, gla_g.reshape(1, GLA_V_W), w_out)


def _rope_tables(n_lat):
    rows = n_lat // GRID_W
    row = jnp.repeat(jnp.arange(rows, dtype=F32), GRID_W)
    col = jnp.broadcast_to(jnp.arange(GRID_W, dtype=F32), (rows, GRID_W)).reshape(-1)
    nf = HEAD_DIM // 4
    inv = ROPE_THETA ** (-jnp.arange(nf, dtype=F32) / nf)
    ar, ac = row[:, None] * inv, col[:, None] * inv
    cos = jnp.concatenate([jnp.cos(ar), jnp.cos(ar), jnp.cos(ac), jnp.cos(ac)], axis=-1)
    sin = jnp.concatenate([-jnp.sin(ar), jnp.sin(ar), -jnp.sin(ac), jnp.sin(ac)], axis=-1)
    cos = jnp.concatenate([cos, jnp.ones((TM, HEAD_DIM), F32)], axis=0)
    sin = jnp.concatenate([sin, jnp.zeros((TM, HEAD_DIM), F32)], axis=0)
    return cos, sin


def kernel(x, c, ctx, c_ctx, mod_w, mod_b, norm_g, ffn1_w_gu, ffn1_w_down, ffn2_w_gu, ffn2_w_down, w_in, w_out,
           qk_norm_g, gla_gate_w, gla_gate_b, gla_norm_g, gmlp_w_s, gmlp_b_s, gmlp_norm_g, final_norm_g):
    batch, n_lat, d = x.shape
    n_ctx = ctx.shape[1]
    depth = mod_w.shape[0]
    assert n_lat % TM == 0 and (batch * n_ctx) % TM == 0 and n_lat % TK == 0 and n_lat % TQ == 0
    assert n_ctx % CHUNK == 0 and n_lat % n_ctx == 0 and batch + 1 <= 8 and TM == TK and TK % n_ctx == 0
    lat_tiles = batch * n_lat // TM
    all_tiles = lat_tiles + batch * n_ctx // TM
    tiles_per_batch = n_lat // TM

    def mod_row(i):
        return jnp.minimum(i // tiles_per_batch, batch)

    def rope_row(i):
        return jnp.where(i < lat_tiles, i % tiles_per_batch, tiles_per_batch)

    xs = jnp.concatenate([x.reshape(batch * n_lat, d), ctx.reshape(batch * n_ctx, d)], axis=0)
    cc = jnp.zeros((8, d), F32).at[:batch].set(c).at[batch].set(c_ctx)
    mod_all = _modulation(cc, mod_w, mod_b).reshape(depth, 8, N_MOD, d)
    cos_t, sin_t = _rope_tables(n_lat)

    offs = np.cumsum([0, ATTN_Q_W, ATTN_KV_W, ATTN_KV_W, GLA_K_W, GLA_K_W, GLA_V_W, GLA_V_W, 2 * GLA_GATE_RANK,
                      GMLP_W, GMLP_W])
    lr0, lr1 = int(offs[7]), int(offs[8])
    w_in_r = jnp.concatenate(
        [w_in[:, :, :lr0], w_in[:, :, lr1:], w_in[:, :, lr0:lr1],
         jnp.zeros((depth, d, GLR_PAD - 2 * GLA_GATE_RANK), w_in.dtype)], axis=-1).astype(BF16)
    gate_w_r = jnp.zeros((depth, GLR_PAD, 2 * GLA_K_W), F32)
    gate_w_r = gate_w_r.at[:, :GLA_GATE_RANK, :GLA_K_W].set(gla_gate_w[:, 0])
    gate_w_r = gate_w_r.at[:, GLA_GATE_RANK:2 * GLA_GATE_RANK, GLA_K_W:].set(gla_gate_w[:, 1]).astype(BF16)
    gate_b_r = gla_gate_b.reshape(depth, 1, 2 * GLA_K_W)
    ws_b = gmlp_w_s.astype(BF16)
    bs_b = jnp.broadcast_to(gmlp_b_s[..., None], gmlp_b_s.shape + (GMLP_GROUP_DIM,))
    w_out_b = w_out.astype(BF16)
    f1gu, f1d = ffn1_w_gu.astype(BF16), ffn1_w_down.astype(BF16)
    f2gu, f2d = ffn2_w_gu.astype(BF16), ffn2_w_down.astype(BF16)

    for l in range(depth):
        last = l == depth - 1
        mod = mod_all[l]
        xs = _ffn(xs, mod, norm_g[l, 0], f1gu[l], f1d[l], final_norm_g, sub=0, n_tiles=all_tiles,
                  mod_row=mod_row, final=False)
        q, k, vt, gq, gk, gv, gr, gf, gb, ym = _inproj(
            xs, mod, norm_g[l, 1], w_in_r[l], qk_norm_g[l], cos_t, sin_t, gate_w_r[l], gate_b_r[l],
            ws_b[l], bs_b[l], gmlp_norm_g[l], mod_row=mod_row, rope_row=rope_row)
        att = _attention(q, k, vt, batch=batch, n_lat=n_lat, n_ctx=n_ctx, latent=True)
        o_f, o_b = _gla(gq, gk, gv, gf, gb, batch=batch, n_lat=n_lat, n_ctx=n_ctx)
        if not last:
            att_c = _attention(q, k, vt, batch=batch, n_lat=n_lat, n_ctx=n_ctx, latent=False)
            att = jnp.concatenate([att, att_c], axis=0)
        n_tiles = lat_tiles if last else all_tiles
        xs = _outproj(xs, mod, att, o_f, o_b, gr, ym, gla_norm_g[l], w_out_b[l], n_tiles=n_tiles, mod_row=mod_row)
        xs = _ffn(xs, mod, norm_g[l, 2], f2gu[l], f2d[l], final_norm_g, sub=2, n_tiles=n_tiles,
                  mod_row=mod_row, final=last)
    return xs.reshape(batch, n_lat, d)
```

```python
import functools

import numpy as np
import jax
import jax.numpy as jnp
from jax import lax
from jax.experimental import pallas as pl
from jax.experimental.pallas import tpu as pltpu

F32 = jnp.float32
BF16 = jnp.bfloat16

EPS = 1e-6
N_MOD = 9
HEAD_DIM = 128
ATTN_HEADS = 8
ATTN_KV_HEADS = 2
ATTN_GROUP = ATTN_HEADS // ATTN_KV_HEADS
ROPE_THETA = 10000.0
GRID_W = 64
GLA_HEADS = 4
GLA_DK = 64
GLA_DV = 128
GLA_GATE_RANK = 16
GLA_TAU = 16.0
LOG2E = 1.4426950408889634
CHUNK = 128
GMLP_GROUPS = 4
GMLP_GROUP_DIM = 128

ATTN_Q_W = ATTN_HEADS * HEAD_DIM
ATTN_KV_W = ATTN_KV_HEADS * HEAD_DIM
GLA_K_W = GLA_HEADS * GLA_DK
GLA_V_W = GLA_HEADS * GLA_DV
GMLP_W = GMLP_GROUPS * GMLP_GROUP_DIM
LANES = 128
GLR_PAD = LANES
N_LEVELS = 7

OFF_AQ = 0
OFF_AK = OFF_AQ + ATTN_Q_W
OFF_AV = OFF_AK + ATTN_KV_W
OFF_GQ = OFF_AV + ATTN_KV_W
OFF_GK = OFF_GQ + GLA_K_W
OFF_GV = OFF_GK + GLA_K_W
OFF_GR = OFF_GV + GLA_V_W
OFF_MU = OFF_GR + GLA_V_W
OFF_MV = OFF_MU + GMLP_W
OFF_LR = OFF_MV + GMLP_W
IN_W_R = OFF_LR + GLR_PAD

TM = 512
TF = 512
TQ = 256
TK = 512
SUM_ROWS = 16
MOD_TN = 2048
VMEM_LIMIT = 56 * 1024 * 1024


def _sigmoid(x):
    return 1.0 / (1.0 + jnp.exp(-x))


def _silu(x):
    return x * _sigmoid(x)


def _rms(x, g):
    ms = jnp.mean(x * x, axis=-1, keepdims=True)
    return x * lax.rsqrt(ms + EPS) * g


def _norm_mod(x, g, mod_ref, i):
    return _rms(x, g) * (1.0 + mod_ref[0, 3 * i + 1:3 * i + 2, :]) + mod_ref[0, 3 * i:3 * i + 1, :]


def _dot(a, b):
    return jnp.dot(a, b, preferred_element_type=F32)


def _dot_t(a, b):
    return lax.dot_general(a, b, (((1,), (1,)), ((), ())), preferred_element_type=F32)


def _tdot(a, b):
    return lax.dot_general(a, b, (((0,), (0,)), ((), ())), preferred_element_type=F32)


def _mod_kernel(c_ref, w_ref, b_ref, o_ref):
    sc = _silu(c_ref[...]).astype(BF16)
    o_ref[0] = _dot(sc, w_ref[0].astype(BF16)) + b_ref[0]


def _modulation(cc, mod_w, mod_b):
    depth, d, n = mod_w.shape
    return pl.pallas_call(
        _mod_kernel,
        grid=(depth, n // MOD_TN),
        in_specs=[
            pl.BlockSpec((8, d), lambda l, j: (0, 0)),
            pl.BlockSpec((1, d, MOD_TN), lambda l, j: (l, 0, j)),
            pl.BlockSpec((1, 1, MOD_TN), lambda l, j: (l, 0, j)),
        ],
        out_specs=pl.BlockSpec((1, 8, MOD_TN), lambda l, j: (l, 0, j)),
        out_shape=jax.ShapeDtypeStruct((depth, 8, n), F32),
        compiler_params=pltpu.CompilerParams(
            dimension_semantics=("parallel", "parallel"), vmem_limit_bytes=VMEM_LIMIT),
        name="modulation",
    )(cc, mod_w, mod_b.reshape(depth, 1, n))


def _ffn_kernel(*refs, sub, final, split_at):
    if split_at is None:
        x_ref, mod_ref, g_ref, wg_ref, wu_ref, wd_ref, fg_ref, o_ref, h_ref, acc_ref = refs
        read_x = lambda: x_ref[...]
    else:
        x_ref, xc_ref, mod_ref, g_ref, wg_ref, wu_ref, wd_ref, fg_ref, o_ref, h_ref, acc_ref = refs
        is_ctx = pl.program_id(0) >= split_at
        read_x = lambda: jnp.where(is_ctx, xc_ref[...], x_ref[...])
    j = pl.program_id(1)

    @pl.when(j == 0)
    def _():
        h_ref[...] = _norm_mod(read_x(), g_ref[...], mod_ref, sub).astype(BF16)
        acc_ref[...] = jnp.zeros_like(acc_ref)

    h = h_ref[...]
    a = _silu(_dot(h, wg_ref[...])) * _dot(h, wu_ref[...])
    acc_ref[...] += _dot(a.astype(BF16), wd_ref[...])

    @pl.when(j == pl.num_programs(1) - 1)
    def _():
        y = read_x() + mod_ref[0, 3 * sub + 2:3 * sub + 3, :] * (0.5 * acc_ref[...])
        if final:
            y = _rms(y, fg_ref[...])
        o_ref[...] = y


def _ffn(xs, mod, g, w_gu, w_down, final_g, *, sub, n_tiles, mod_row, final, xc=None):
    t, d = xs.shape
    f = w_down.shape[0]
    nf = f // TF
    if xc is None:
        split_at = None
        x_specs = [pl.BlockSpec((TM, d), lambda i, j: (i, 0))]
        x_args = (xs,)
    else:
        split_at = t // TM
        x_specs = [pl.BlockSpec((TM, d), lambda i, j: (jnp.minimum(i, split_at - 1), 0)),
                   pl.BlockSpec((TM, d), lambda i, j: (jnp.maximum(i - split_at, 0), 0))]
        x_args = (xs, xc)
    kern = functools.partial(_ffn_kernel, sub=sub, final=final, split_at=split_at)
    return pl.pallas_call(
        kern,
        grid=(n_tiles, nf),
        in_specs=x_specs + [
            pl.BlockSpec((1, N_MOD, d), lambda i, j: (mod_row(i), 0, 0)),
            pl.BlockSpec((1, d), lambda i, j: (0, 0)),
            pl.BlockSpec((d, TF), lambda i, j: (0, j)),
            pl.BlockSpec((d, TF), lambda i, j: (0, j + nf)),
            pl.BlockSpec((TF, d), lambda i, j: (j, 0)),
            pl.BlockSpec((1, d), lambda i, j: (0, 0)),
        ],
        out_specs=pl.BlockSpec((TM, d), lambda i, j: (i, 0)),
        out_shape=jax.ShapeDtypeStruct((n_tiles * TM, d), F32),
        scratch_shapes=[pltpu.VMEM((TM, d), BF16), pltpu.VMEM((TM, d), F32)],
        compiler_params=pltpu.CompilerParams(
            dimension_semantics=("parallel", "arbitrary"), vmem_limit_bytes=VMEM_LIMIT),
        name="ffn_final" if final else "ffn",
    )(*x_args, mod, g.reshape(1, d), w_gu, w_gu, w_down, final_g.reshape(1, d))


def _rope(x, cos, sin_signed, lane_low):
    partner = jnp.where(lane_low, pltpu.roll(x, LANES - 32, 1), pltpu.roll(x, 32, 1))
    return x * cos + partner * sin_signed


def _log_sigmoid(x):
    return jnp.minimum(x, 0.0) - jnp.log(1.0 + jnp.exp(-jnp.abs(x)))


def _inproj_kernel(x_ref, mod_ref, g_ref, w_ref, qkg_ref, cos_ref, sin_ref, gw_ref, gbias_ref,
                   ws_ref, bs_ref, gmg_ref,
                   q_ref, k_ref, vt_ref, gq_ref, gk_ref, gv_ref, gr_ref, gf_ref, gb_ref, ym_ref):
    h = _norm_mod(x_ref[...], g_ref[...], mod_ref, 1).astype(BF16)
    cos = cos_ref[...]
    sin = sin_ref[...]
    lane = lax.broadcasted_iota(jnp.int32, cos.shape, 1)
    lane_low = (lane & 63) < 32

    def proj(off, width):
        return _dot(h, w_ref[:, off:off + width])

    scale = HEAD_DIM ** -0.5 * LOG2E
    zq = proj(OFF_AQ, ATTN_Q_W)
    for hh in range(ATTN_HEADS):
        sl = slice(hh * HEAD_DIM, (hh + 1) * HEAD_DIM)
        qh = _rope(_rms(zq[:, sl], qkg_ref[0:1, :]), cos, sin, lane_low)
        q_ref[:, sl] = (qh * scale).astype(BF16)
    zk = proj(OFF_AK, ATTN_KV_W)
    for hh in range(ATTN_KV_HEADS):
        sl = slice(hh * HEAD_DIM, (hh + 1) * HEAD_DIM)
        k_ref[:, sl] = _rope(_rms(zk[:, sl], qkg_ref[1:2, :]), cos, sin, lane_low).astype(BF16)
    vt_ref[0] = proj(OFF_AV, ATTN_KV_W).T.astype(BF16)

    gq_ref[...] = proj(OFF_GQ, GLA_K_W) * (GLA_DK ** -0.5)
    gk_ref[...] = proj(OFF_GK, GLA_K_W)
    gv_ref[...] = proj(OFF_GV, GLA_V_W).astype(BF16)
    gr_ref[...] = proj(OFF_GR, GLA_V_W)
    lr = proj(OFF_LR, GLR_PAD).astype(BF16)
    logits = _dot(lr, gw_ref[...]) + gbias_ref[...]
    ld = _log_sigmoid(logits) * (1.0 / GLA_TAU)
    gf_ref[...] = ld[:, :GLA_K_W]
    gb_ref[...] = ld[:, GLA_K_W:]

    mu = proj(OFF_MU, GMLP_W)
    vn = _rms(proj(OFF_MV, GMLP_W), gmg_ref[...]).astype(BF16)
    for c in range(x_ref.shape[0] // CHUNK):
        rows = slice(c * CHUNK, (c + 1) * CHUNK)
        for gi in range(GMLP_GROUPS):
            cols = slice(gi * GMLP_GROUP_DIM, (gi + 1) * GMLP_GROUP_DIM)
            z = _dot(ws_ref[gi], vn[rows, cols]) + bs_ref[gi]
            ym_ref[rows, cols] = (mu[rows, cols] * z).astype(BF16)


def _inproj(xs, mod, g, w_in_r, qk_g, cos_t, sin_t, gate_w_r, gate_b_r, ws, bs_b, gm_g, *, mod_row, rope_row):
    t, d = xs.shape
    n_tiles = t // TM
    row = lambda i: (i, 0)
    const2 = lambda i: (0, 0)
    const3 = lambda i: (0, 0, 0)
    widths = [(ATTN_Q_W, BF16), (ATTN_KV_W, BF16), None, (GLA_K_W, F32), (GLA_K_W, F32),
              (GLA_V_W, BF16), (GLA_V_W, F32), (GLA_K_W, F32), (GLA_K_W, F32), (GMLP_W, BF16)]
    out_specs = [pl.BlockSpec((1, ATTN_KV_W, TM), lambda i: (i, 0, 0)) if w is None else pl.BlockSpec((TM, w[0]), row)
                 for w in widths]
    out_shape = [jax.ShapeDtypeStruct((n_tiles, ATTN_KV_W, TM), BF16) if w is None
                 else jax.ShapeDtypeStruct((t, w[0]), w[1]) for w in widths]
    return pl.pallas_call(
        _inproj_kernel,
        grid=(n_tiles,),
        in_specs=[
            pl.BlockSpec((TM, d), row),
            pl.BlockSpec((1, N_MOD, d), lambda i: (mod_row(i), 0, 0)),
            pl.BlockSpec((1, d), const2),
            pl.BlockSpec((d, IN_W_R), const2, pipeline_mode=pl.Buffered(1)),
            pl.BlockSpec((2, HEAD_DIM), const2),
            pl.BlockSpec((TM, HEAD_DIM), lambda i: (rope_row(i), 0)),
            pl.BlockSpec((TM, HEAD_DIM), lambda i: (rope_row(i), 0)),
            pl.BlockSpec((GLR_PAD, 2 * GLA_K_W), const2),
            pl.BlockSpec((1, 2 * GLA_K_W), const2),
            pl.BlockSpec((GMLP_GROUPS, CHUNK, CHUNK), const3),
            pl.BlockSpec((GMLP_GROUPS, CHUNK, GMLP_GROUP_DIM), const3),
            pl.BlockSpec((1, GMLP_W), const2),
        ],
        out_specs=out_specs,
        out_shape=out_shape,
        compiler_params=pltpu.CompilerParams(
            dimension_semantics=("parallel",), vmem_limit_bytes=VMEM_LIMIT),
        name="inproj",
    )(xs, mod, g.reshape(1, d), w_in_r, qk_g, cos_t, sin_t, gate_w_r, gate_b_r, ws, bs_b, gm_g.reshape(1, GMLP_W))


def _attn_kernel(*refs, n_lat_tiles):
    if n_lat_tiles:
        q_ref, kc_ref, vtc_ref, kl_ref, vtl_ref, o_ref, sa_ref, sb_ref, ta_ref, tb_ref, m_ref, acc_ref = refs
        buf_a, buf_b = (sa_ref, ta_ref), (sb_ref, tb_ref)
    else:
        q_ref, kc_ref, vtc_ref, o_ref, m_ref, acc_ref = refs
    tq = q_ref.shape[0]
    q = jnp.concatenate([q_ref[:, g * HEAD_DIM:(g + 1) * HEAD_DIM] for g in range(ATTN_GROUP)], axis=0)

    def scores(k):
        return _dot_t(k, q)

    def with_ones(vt):
        return jnp.concatenate([vt, jnp.ones((SUM_ROWS, vt.shape[1]), BF16)], axis=0)

    def first_tile(s, vt):
        m_new = jnp.max(s, axis=0, keepdims=True)
        p = jnp.exp2(s - m_new)
        m_ref[...] = m_new
        acc_ref[...] = _dot(with_ones(vt), p.astype(BF16))

    def next_tile(buf, t):
        s_ref, tmax_ref = buf
        m_old = m_ref[...]
        m_new = jnp.maximum(m_old, tmax_ref[...])
        alpha = jnp.exp2(m_old - m_new)
        p = jnp.exp2(s_ref[...] - m_new)
        m_ref[...] = m_new
        acc_ref[...] = alpha * acc_ref[...] + _dot(with_ones(vtl_ref[t]), p.astype(BF16))

    def lat_scores(buf, t):
        s_ref, tmax_ref = buf
        start = pl.multiple_of(t * TK, TK)
        s = scores(kl_ref[pl.ds(start, TK), :])
        s_ref[...] = s
        tmax_ref[...] = jnp.max(s, axis=0, keepdims=True)

    s_ctx = scores(kc_ref[...])
    if n_lat_tiles:
        lat_scores(buf_a, 0)
    first_tile(s_ctx, vtc_ref[0])
    if n_lat_tiles:
        pairs = (n_lat_tiles - 1) // 2

        def body(u, carry):
            lat_scores(buf_b, 2 * u + 1)
            next_tile(buf_a, 2 * u)
            lat_scores(buf_a, 2 * u + 2)
            next_tile(buf_b, 2 * u + 1)
            return carry
        lax.fori_loop(0, pairs, body, 0)
        if n_lat_tiles - 2 * pairs == 2:
            lat_scores(buf_b, n_lat_tiles - 1)
            next_tile(buf_a, n_lat_tiles - 2)
            next_tile(buf_b, n_lat_tiles - 1)
        else:
            next_tile(buf_a, n_lat_tiles - 1)
    out_t = acc_ref[0:HEAD_DIM, :] / acc_ref[HEAD_DIM:HEAD_DIM + 1, :]
    for g in range(ATTN_GROUP):
        o_ref[:, g * HEAD_DIM:(g + 1) * HEAD_DIM] = out_t[:, g * tq:(g + 1) * tq].T.astype(BF16)


def _attention(q, k, vt, *, batch, n_lat, n_ctx, latent):
    gw = ATTN_GROUP * HEAD_DIM
    ctx_blk0 = (batch * n_lat) // n_ctx
    lat_tiles = (batch * n_lat) // TK
    per_tile = TK // n_ctx
    kc_spec = pl.BlockSpec((n_ctx, HEAD_DIM), lambda b, kh, i: (ctx_blk0 + b, kh))
    vtc_spec = pl.BlockSpec((1, HEAD_DIM, n_ctx), lambda b, kh, i: (lat_tiles + b // per_tile, kh, b % per_tile))
    if latent:
        tq = TQ
        nq = n_lat // tq
        n_lat_tiles = n_lat // TK
        q_spec = pl.BlockSpec((tq, gw), lambda b, kh, i: (b * nq + i, kh))
        kl_spec = pl.BlockSpec((n_lat, HEAD_DIM), lambda b, kh, i: (b, kh))
        vtl_spec = pl.BlockSpec((n_lat_tiles, HEAD_DIM, TK), lambda b, kh, i: (b, kh, 0))
        in_specs = [q_spec, kc_spec, vtc_spec, kl_spec, vtl_spec]
        args = (q, k, vt, k, vt)
        out_rows = batch * n_lat
    else:
        tq = n_ctx
        nq = 1
        n_lat_tiles = 0
        q_spec = pl.BlockSpec((tq, gw), lambda b, kh, i: (ctx_blk0 + b, kh))
        in_specs = [q_spec, kc_spec, vtc_spec]
        args = (q, k, vt)
        out_rows = batch * n_ctx
    cols = ATTN_GROUP * tq
    scratch = [pltpu.VMEM((1, cols), F32), pltpu.VMEM((HEAD_DIM + SUM_ROWS, cols), F32)]
    if latent:
        scratch = [pltpu.VMEM((TK, cols), F32), pltpu.VMEM((TK, cols), F32),
                   pltpu.VMEM((1, cols), F32), pltpu.VMEM((1, cols), F32)] + scratch
    return pl.pallas_call(
        functools.partial(_attn_kernel, n_lat_tiles=n_lat_tiles),
        grid=(batch, ATTN_KV_HEADS, nq),
        in_specs=in_specs,
        out_specs=pl.BlockSpec((tq, gw), lambda b, kh, i: (b * nq + i, kh)),
        out_shape=jax.ShapeDtypeStruct((out_rows, ATTN_Q_W), BF16),
        scratch_shapes=scratch,
        compiler_params=pltpu.CompilerParams(
            dimension_semantics=("parallel", "parallel", "arbitrary"), vmem_limit_bytes=VMEM_LIMIT),
        name="attn_lat" if latent else "attn_ctx",
    )(*args)


def _gla_consts():
    idx = np.arange(CHUNK)
    tri = (idx[None, :] <= idx[:, None]).astype(np.float32)
    mats_f, mats_b = [tri], [tri.T]
    for lvl in range(1, N_LEVELS + 1):
        s = (2 * CHUNK) >> lvl
        base = (idx // s) * s
        mats_f.append(tri[base + s // 2 - 1])
        mats_b.append(tri.T[base + s // 2])
    cm = np.stack([np.concatenate(mats_f, 0), np.concatenate(mats_b, 0)])
    cm = np.concatenate([cm, cm], axis=-1)
    x = idx[:, None] ^ idx[None, :]
    hb = np.floor(np.log2(np.maximum(x, 1))).astype(np.int32)
    lv = np.where(x == 0, 0, N_LEVELS - hb)
    lv_f = np.where(idx[:, None] >= idx[None, :], lv, -1)
    lv_b = np.where(idx[:, None] <= idx[None, :], lv, -1)
    return cm, np.stack([lv_f, lv_b]).astype(np.int32)


def _gla_chunk(cm_ref, lv_ref, q_ref, k_ref, v_ref, g_ref, o_ref, st_ref, d):
    g = g_ref[...]
    g_hi = g.astype(BF16)
    g_lo = (g - g_hi.astype(F32)).astype(BF16)
    cums = _dot(cm_ref[d], jnp.concatenate([g_hi, g_lo], axis=0))
    cum = cums[0:CHUNK]
    q = q_ref[...]
    k = k_ref[...]
    lv = lv_ref[d]
    last = CHUNK - 1 if d == 0 else 0
    tail = cum[last:last + 1, :]
    lane = lax.broadcasted_iota(jnp.int32, (CHUNK, LANES), 1)
    low = lane < GLA_DK

    qs = [q.astype(BF16)]
    ks = [k.astype(BF16)]
    for lvl in range(1, N_LEVELS + 1):
        mid = cums[lvl * CHUNK:(lvl + 1) * CHUNK]
        qs.append((q * jnp.exp(jnp.minimum(cum - mid, 0.0))).astype(BF16))
        ks.append((k * jnp.exp(jnp.minimum(mid - cum, 0.0))).astype(BF16))
    q_in = (q * jnp.exp(cum)).astype(BF16)
    k_out = (k * jnp.exp(tail - cum)).astype(BF16)
    zero = jnp.zeros((CHUNK, LANES), BF16)

    for p in range(GLA_HEADS // 2):
        pl_sl = slice(p * LANES, (p + 1) * LANES)
        st = st_ref[d, p]
        st_b = st.astype(BF16)
        upd = []
        for hp in range(2):
            head = 2 * p + hp
            keep = low if hp == 0 else jnp.logical_not(low)
            a = jnp.zeros((CHUNK, CHUNK), F32)
            for lvl in range(N_LEVELS + 1):
                qm = jnp.where(keep, qs[lvl][:, pl_sl], zero)
                a = jnp.where(lv == lvl, _dot_t(qm, ks[lvl][:, pl_sl]), a)
            vh = v_ref[:, head * GLA_DV:(head + 1) * GLA_DV]
            inter = _dot_t(jnp.where(keep, q_in[:, pl_sl], zero), st_b)
            o_ref[:, head * GLA_DV:(head + 1) * GLA_DV] = inter + _dot(a.astype(BF16), vh)
            upd.append(_tdot(vh, k_out[:, pl_sl]))
        st_ref[d, p] = st * jnp.exp(tail[:, pl_sl]) + jnp.where(low, upd[0], upd[1])


def _gla_kernel(cm_ref, lv_ref, qf, kf, vf, gf, qb, kb, vb, gb, of_ref, ob_ref, st_ref):
    @pl.when(pl.program_id(1) == 0)
    def _():
        st_ref[...] = jnp.zeros_like(st_ref)

    _gla_chunk(cm_ref, lv_ref, qf, kf, vf, gf, of_ref, st_ref, 0)
    _gla_chunk(cm_ref, lv_ref, qb, kb, vb, gb, ob_ref, st_ref, 1)


def _gla(gq, gk, gv, gf, gb, *, batch, n_lat, n_ctx):
    t = gq.shape[0]
    cl, cc = n_lat // CHUNK, n_ctx // CHUNK
    ctx0 = batch * cl
    cm_np, lv_np = _gla_consts()
    cm = jnp.asarray(cm_np, BF16)
    lv = jnp.asarray(lv_np)

    def fwd(b, s):
        return (jnp.where(s < cc, ctx0 + b * cc + s, b * cl + s - cc), 0)

    def bwd(b, s):
        return (jnp.where(s < cc, ctx0 + b * cc + (cc - 1 - s), b * cl + (cl - 1 - (s - cc))), 0)

    def specs(m):
        return [pl.BlockSpec((CHUNK, GLA_K_W), m), pl.BlockSpec((CHUNK, GLA_K_W), m),
                pl.BlockSpec((CHUNK, GLA_V_W), m), pl.BlockSpec((CHUNK, GLA_K_W), m)]

    return pl.pallas_call(
        _gla_kernel,
        grid=(batch, cc + cl),
        in_specs=[pl.BlockSpec(cm.shape, lambda b, s: (0, 0, 0)), pl.BlockSpec(lv.shape, lambda b, s: (0, 0, 0))]
        + specs(fwd) + specs(bwd),
        out_specs=[pl.BlockSpec((CHUNK, GLA_V_W), fwd), pl.BlockSpec((CHUNK, GLA_V_W), bwd)],
        out_shape=[jax.ShapeDtypeStruct((t, GLA_V_W), F32)] * 2,
        scratch_shapes=[pltpu.VMEM((2, GLA_HEADS // 2, GLA_DV, LANES), F32)],
        compiler_params=pltpu.CompilerParams(
            dimension_semantics=("parallel", "arbitrary"), vmem_limit_bytes=VMEM_LIMIT),
        name="gla",
    )(cm, lv, gq, gk, gv, gf, gq, gk, gv, gb)


def _outproj_kernel(*refs, split_at):
    if split_at is None:
        x_ref, mod_ref, att_ref, of_ref, ob_ref, gr_ref, ym_ref, gg_ref, w_ref, o_ref = refs
        att = att_ref[...]
    else:
        x_ref, mod_ref, att_ref, attc_ref, of_ref, ob_ref, gr_ref, ym_ref, gg_ref, w_ref, o_ref = refs
        att = jnp.where(pl.program_id(0) >= split_at, attc_ref[...], att_ref[...])
    o = of_ref[...] + ob_ref[...]
    r = gr_ref[...]
    y = _dot(att, w_ref[0:ATTN_Q_W, :])
    for hh in range(GLA_HEADS):
        sl = slice(hh * GLA_DV, (hh + 1) * GLA_DV)
        gh = (_rms(o[:, sl], gg_ref[:, sl]) * _silu(r[:, sl])).astype(BF16)
        y += _dot(gh, w_ref[ATTN_Q_W + hh * GLA_DV:ATTN_Q_W + (hh + 1) * GLA_DV, :])
    y += _dot(ym_ref[...], w_ref[ATTN_Q_W + GLA_V_W:, :])
    o_ref[...] = x_ref[...] + mod_ref[0, 5:6, :] * y


def _outproj(xs, mod, att, att_c, o_f, o_b, gr, ym, gla_g, w_out, *, n_tiles, mod_row):
    t, d = xs.shape
    row = lambda i: (i, 0)
    const2 = lambda i: (0, 0)
    if att_c is None:
        split_at = None
        att_specs = [pl.BlockSpec((TM, ATTN_Q_W), row)]
        att_args = (att,)
    else:
        split_at = att.shape[0] // TM
        att_specs = [pl.BlockSpec((TM, ATTN_Q_W), lambda i: (jnp.minimum(i, split_at - 1), 0)),
                     pl.BlockSpec((TM, ATTN_Q_W), lambda i: (jnp.maximum(i - split_at, 0), 0))]
        att_args = (att, att_c)
    return pl.pallas_call(
        functools.partial(_outproj_kernel, split_at=split_at),
        grid=(n_tiles,),
        in_specs=[
            pl.BlockSpec((TM, d), row),
            pl.BlockSpec((1, N_MOD, d), lambda i: (mod_row(i), 0, 0))] + att_specs + [
            pl.BlockSpec((TM, GLA_V_W), row),
            pl.BlockSpec((TM, GLA_V_W), row),
            pl.BlockSpec((TM, GLA_V_W), row),
            pl.BlockSpec((TM, GMLP_W), row),
            pl.BlockSpec((1, GLA_V_W), const2),
            pl.BlockSpec(w_out.shape, const2, pipeline_mode=pl.Buffered(1)),
        ],
        out_specs=pl.BlockSpec((TM, d), row),
        out_shape=jax.ShapeDtypeStruct((n_tiles * TM, d), F32),
        compiler_params=pltpu.CompilerParams(
            dimension_semantics=("parallel",), vmem_limit_bytes=VMEM_LIMIT),
        name="outproj",
    )(xs, mod, *att_args, o_f, o_b, gr, ym, gla_g.reshape(1, GLA_V_W), w_out)


def _rope_tables(n_lat):
    rows = n_lat // GRID_W
    row = jnp.repeat(jnp.arange(rows, dtype=F32), GRID_W)
    col = jnp.broadcast_to(jnp.arange(GRID_W, dtype=F32), (rows, GRID_W)).reshape(-1)
    nf = HEAD_DIM // 4
    inv = ROPE_THETA ** (-jnp.arange(nf, dtype=F32) / nf)
    ar, ac = row[:, None] * inv, col[:, None] * inv
    cos = jnp.concatenate([jnp.cos(ar), jnp.cos(ar), jnp.cos(ac), jnp.cos(ac)], axis=-1)
    sin = jnp.concatenate([-jnp.sin(ar), jnp.sin(ar), -jnp.sin(ac), jnp.sin(ac)], axis=-1)
    cos = jnp.concatenate([cos, jnp.ones((TM, HEAD_DIM), F32)], axis=0)
    sin = jnp.concatenate([sin, jnp.zeros((TM, HEAD_DIM), F32)], axis=0)
    return cos, sin


def kernel(x, c, ctx, c_ctx, mod_w, mod_b, norm_g, ffn1_w_gu, ffn1_w_down, ffn2_w_gu, ffn2_w_down, w_in, w_out,
           qk_norm_g, gla_gate_w, gla_gate_b, gla_norm_g, gmlp_w_s, gmlp_b_s, gmlp_norm_g, final_norm_g):
    batch, n_lat, d = x.shape
    n_ctx = ctx.shape[1]
    depth = mod_w.shape[0]
    assert n_lat % TM == 0 and (batch * n_ctx) % TM == 0 and n_lat % TK == 0 and n_lat % TQ == 0
    assert n_ctx % CHUNK == 0 and n_lat % n_ctx == 0 and batch + 1 <= 8 and TM == TK and TK % n_ctx == 0
    lat_tiles = batch * n_lat // TM
    all_tiles = lat_tiles + batch * n_ctx // TM
    tiles_per_batch = n_lat // TM

    def mod_row(i):
        return jnp.minimum(i // tiles_per_batch, batch)

    def rope_row(i):
        return jnp.where(i < lat_tiles, i % tiles_per_batch, tiles_per_batch)

    cc = jnp.zeros((8, d), F32).at[:batch].set(c).at[batch].set(c_ctx)
    mod_all = _modulation(cc, mod_w, mod_b).reshape(depth, 8, N_MOD, d)
    cos_t, sin_t = _rope_tables(n_lat)
    offs = np.cumsum([0, ATTN_Q_W, ATTN_KV_W, ATTN_KV_W, GLA_K_W, GLA_K_W, GLA_V_W, GLA_V_W, 2 * GLA_GATE_RANK,
                      GMLP_W, GMLP_W])
    lr0, lr1 = int(offs[7]), int(offs[8])
    xs = x.reshape(batch * n_lat, d)
    xc = ctx.reshape(batch * n_ctx, d)

    for l in range(depth):
        last = l == depth - 1
        mod = mod_all[l]
        w_in_l = w_in[l]
        w_in_r = jnp.concatenate(
            [w_in_l[:, :lr0], w_in_l[:, lr1:], w_in_l[:, lr0:lr1],
             jnp.zeros((d, GLR_PAD - 2 * GLA_GATE_RANK), w_in.dtype)], axis=-1).astype(BF16)
        gate_w_r = jnp.zeros((GLR_PAD, 2 * GLA_K_W), F32)
        gate_w_r = gate_w_r.at[:GLA_GATE_RANK, :GLA_K_W].set(gla_gate_w[l, 0])
        gate_w_r = gate_w_r.at[GLA_GATE_RANK:2 * GLA_GATE_RANK, GLA_K_W:].set(gla_gate_w[l, 1]).astype(BF16)
        gate_b_r = gla_gate_b[l].reshape(1, 2 * GLA_K_W)
        bs_b = jnp.broadcast_to(gmlp_b_s[l][..., None], gmlp_b_s.shape[1:] + (GMLP_GROUP_DIM,))

        xs = _ffn(xs, mod, norm_g[l, 0], ffn1_w_gu[l].astype(BF16), ffn1_w_down[l].astype(BF16), final_norm_g,
                  sub=0, n_tiles=all_tiles, mod_row=mod_row, final=False, xc=xc)
        xc = None
        q, k, vt, gq, gk, gv, gr, gf, gb, ym = _inproj(
            xs, mod, norm_g[l, 1], w_in_r, qk_norm_g[l], cos_t, sin_t, gate_w_r, gate_b_r,
            gmlp_w_s[l].astype(BF16), bs_b, gmlp_norm_g[l], mod_row=mod_row, rope_row=rope_row)
        att = _attention(q, k, vt, batch=batch, n_lat=n_lat, n_ctx=n_ctx, latent=True)
        o_f, o_b = _gla(gq, gk, gv, gf, gb, batch=batch, n_lat=n_lat, n_ctx=n_ctx)
        att_c = None if last else _attention(q, k, vt, batch=batch, n_lat=n_lat, n_ctx=n_ctx, latent=False)
        n_tiles = lat_tiles if last else all_tiles
        xs = _outproj(xs, mod, att, att_c, o_f, o_b, gr, ym, gla_norm_g[l], w_out[l].astype(BF16),
                      n_tiles=n_tiles, mod_row=mod_row)
        xs = _ffn(xs, mod, norm_g[l, 2], ffn2_w_gu[l].astype(BF16), ffn2_w_down[l].astype(BF16), final_norm_g,
                  sub=2, n_tiles=n_tiles, mod_row=mod_row, final=last)
    return xs.reshape(batch, n_lat, d)
```

```python
import functools

import numpy as np
import jax
import jax.numpy as jnp
from jax import lax
from jax.experimental import pallas as pl
from jax.experimental.pallas import tpu as pltpu

F32 = jnp.float32
BF16 = jnp.bfloat16

EPS = 1e-6
N_MOD = 9
HEAD_DIM = 128
ATTN_HEADS = 8
ATTN_KV_HEADS = 2
ATTN_GROUP = ATTN_HEADS // ATTN_KV_HEADS
ROPE_THETA = 10000.0
GRID_W = 64
GLA_HEADS = 4
GLA_DK = 64
GLA_DV = 128
GLA_GATE_RANK = 16
GLA_TAU = 16.0
LOG2E = 1.4426950408889634
CHUNK = 128
GMLP_GROUPS = 4
GMLP_GROUP_DIM = 128

ATTN_Q_W = ATTN_HEADS * HEAD_DIM
ATTN_KV_W = ATTN_KV_HEADS * HEAD_DIM
GLA_K_W = GLA_HEADS * GLA_DK
GLA_V_W = GLA_HEADS * GLA_DV
GMLP_W = GMLP_GROUPS * GMLP_GROUP_DIM
LANES = 128
GLR_PAD = LANES
N_LEVELS = 7

OFF_AQ = 0
OFF_AK = OFF_AQ + ATTN_Q_W
OFF_AV = OFF_AK + ATTN_KV_W
OFF_GQ = OFF_AV + ATTN_KV_W
OFF_GK = OFF_GQ + GLA_K_W
OFF_GV = OFF_GK + GLA_K_W
OFF_GR = OFF_GV + GLA_V_W
OFF_MU = OFF_GR + GLA_V_W
OFF_MV = OFF_MU + GMLP_W
OFF_LR = OFF_MV + GMLP_W
IN_W_R = OFF_LR + GLR_PAD

TM = 512
TF = 512
TQ = 256
TK = 512
GLA_STEP_CHUNKS = 2
SUM_ROWS = 16
MAX_SAFE_JUMP = 100.0
MOD_TN = 2048
VMEM_LIMIT = 56 * 1024 * 1024


def _sigmoid(x):
    return 1.0 / (1.0 + jnp.exp(-x))


def _silu(x):
    return x * _sigmoid(x)


def _rms(x, g):
    ms = jnp.mean(x * x, axis=-1, keepdims=True)
    return x * lax.rsqrt(ms + EPS) * g


def _norm_mod(x, g, mod_ref, i):
    return _rms(x, g) * (1.0 + mod_ref[0, 3 * i + 1:3 * i + 2, :]) + mod_ref[0, 3 * i:3 * i + 1, :]


def _dot(a, b):
    return jnp.dot(a, b, preferred_element_type=F32)


def _dot_t(a, b):
    return lax.dot_general(a, b, (((1,), (1,)), ((), ())), preferred_element_type=F32)


def _tdot(a, b):
    return lax.dot_general(a, b, (((0,), (0,)), ((), ())), preferred_element_type=F32)


def _mod_kernel(c_ref, w_ref, b_ref, o_ref):
    sc = _silu(c_ref[...]).astype(BF16)
    o_ref[0] = _dot(sc, w_ref[0].astype(BF16)) + b_ref[0]


def _modulation(cc, mod_w, mod_b):
    depth, d, n = mod_w.shape
    return pl.pallas_call(
        _mod_kernel,
        grid=(depth, n // MOD_TN),
        in_specs=[
            pl.BlockSpec((8, d), lambda l, j: (0, 0)),
            pl.BlockSpec((1, d, MOD_TN), lambda l, j: (l, 0, j)),
            pl.BlockSpec((1, 1, MOD_TN), lambda l, j: (l, 0, j)),
        ],
        out_specs=pl.BlockSpec((1, 8, MOD_TN), lambda l, j: (l, 0, j)),
        out_shape=jax.ShapeDtypeStruct((depth, 8, n), F32),
        compiler_params=pltpu.CompilerParams(
            dimension_semantics=("parallel", "parallel"), vmem_limit_bytes=VMEM_LIMIT),
        name="modulation",
    )(cc, mod_w, mod_b.reshape(depth, 1, n))


def _ffn_kernel(*refs, sub, final, split_at):
    if split_at is None:
        x_ref, mod_ref, g_ref, wg_ref, wu_ref, wd_ref, fg_ref, o_ref, h_ref, acc_ref = refs
        read_x = lambda: x_ref[...]
    else:
        x_ref, xc_ref, mod_ref, g_ref, wg_ref, wu_ref, wd_ref, fg_ref, o_ref, h_ref, acc_ref = refs
        is_ctx = pl.program_id(0) >= split_at
        read_x = lambda: jnp.where(is_ctx, xc_ref[...], x_ref[...])
    j = pl.program_id(1)

    @pl.when(j == 0)
    def _():
        h_ref[...] = _norm_mod(read_x(), g_ref[...], mod_ref, sub).astype(BF16)
        acc_ref[...] = jnp.zeros_like(acc_ref)

    h = h_ref[...]
    a = _silu(_dot(h, wg_ref[...])) * _dot(h, wu_ref[...])
    acc_ref[...] += _dot(a.astype(BF16), wd_ref[...])

    @pl.when(j == pl.num_programs(1) - 1)
    def _():
        y = read_x() + mod_ref[0, 3 * sub + 2:3 * sub + 3, :] * (0.5 * acc_ref[...])
        if final:
            y = _rms(y, fg_ref[...])
        o_ref[...] = y


def _ffn(xs, mod, g, w_gu, w_down, final_g, *, layer, sub, n_tiles, mod_row, final, xc=None):
    t, d = xs.shape
    f = w_down.shape[1]
    nf = f // TF
    if xc is None:
        split_at = None
        x_specs = [pl.BlockSpec((TM, d), lambda i, j: (i, 0))]
        x_args = (xs,)
    else:
        split_at = t // TM
        x_specs = [pl.BlockSpec((TM, d), lambda i, j: (jnp.minimum(i, split_at - 1), 0)),
                   pl.BlockSpec((TM, d), lambda i, j: (jnp.maximum(i - split_at, 0), 0))]
        x_args = (xs, xc)
    kern = functools.partial(_ffn_kernel, sub=sub, final=final, split_at=split_at)
    return pl.pallas_call(
        kern,
        grid=(n_tiles, nf),
        in_specs=x_specs + [
            pl.BlockSpec((1, N_MOD, d), lambda i, j: (mod_row(i), 0, 0)),
            pl.BlockSpec((1, d), lambda i, j: (0, 0)),
            pl.BlockSpec((None, d, TF), lambda i, j: (layer, 0, j)),
            pl.BlockSpec((None, d, TF), lambda i, j: (layer, 0, j + nf)),
            pl.BlockSpec((None, TF, d), lambda i, j: (layer, j, 0)),
            pl.BlockSpec((1, d), lambda i, j: (0, 0)),
        ],
        out_specs=pl.BlockSpec((TM, d), lambda i, j: (i, 0)),
        out_shape=jax.ShapeDtypeStruct((n_tiles * TM, d), F32),
        scratch_shapes=[pltpu.VMEM((TM, d), BF16), pltpu.VMEM((TM, d), F32)],
        compiler_params=pltpu.CompilerParams(
            dimension_semantics=("parallel", "arbitrary"), vmem_limit_bytes=VMEM_LIMIT),
        name="ffn_final" if final else "ffn",
    )(*x_args, mod, g.reshape(1, d), w_gu, w_gu, w_down, final_g.reshape(1, d))


def _rope(x, cos, sin_signed, lane_low):
    partner = jnp.where(lane_low, pltpu.roll(x, LANES - 32, 1), pltpu.roll(x, 32, 1))
    return x * cos + partner * sin_signed


def _log_sigmoid(x):
    return jnp.minimum(x, 0.0) - jnp.log(1.0 + jnp.exp(-jnp.abs(x)))


def _inproj_kernel(x_ref, mod_ref, g_ref, w_ref, qkg_ref, cos_ref, sin_ref, gw_ref, gbias_ref,
                   ws_ref, bs_ref, gmg_ref,
                   q_ref, k_ref, vt_ref, gq_ref, gk_ref, gv_ref, gr_ref, gf_ref, gb_ref, ym_ref):
    h = _norm_mod(x_ref[...], g_ref[...], mod_ref, 1).astype(BF16)
    cos = cos_ref[...]
    sin = sin_ref[...]
    lane = lax.broadcasted_iota(jnp.int32, cos.shape, 1)
    lane_low = (lane & 63) < 32

    def proj(off, width):
        return _dot(h, w_ref[:, off:off + width])

    scale = HEAD_DIM ** -0.5 * LOG2E
    zq = proj(OFF_AQ, ATTN_Q_W)
    for hh in range(ATTN_HEADS):
        sl = slice(hh * HEAD_DIM, (hh + 1) * HEAD_DIM)
        qh = _rope(_rms(zq[:, sl], qkg_ref[0:1, :]), cos, sin, lane_low)
        q_ref[:, sl] = (qh * scale).astype(BF16)
    zk = proj(OFF_AK, ATTN_KV_W)
    for hh in range(ATTN_KV_HEADS):
        sl = slice(hh * HEAD_DIM, (hh + 1) * HEAD_DIM)
        k_ref[:, sl] = _rope(_rms(zk[:, sl], qkg_ref[1:2, :]), cos, sin, lane_low).astype(BF16)
    vt_ref[0] = proj(OFF_AV, ATTN_KV_W).T.astype(BF16)

    gq_ref[...] = proj(OFF_GQ, GLA_K_W) * (GLA_DK ** -0.5)
    gk_ref[...] = proj(OFF_GK, GLA_K_W)
    gv_ref[...] = proj(OFF_GV, GLA_V_W).astype(BF16)
    gr_ref[...] = proj(OFF_GR, GLA_V_W)
    lr = proj(OFF_LR, GLR_PAD).astype(BF16)
    logits = _dot(lr, gw_ref[...]) + gbias_ref[...]
    ld = _log_sigmoid(logits) * (1.0 / GLA_TAU)
    gf_ref[...] = ld[:, :GLA_K_W]
    gb_ref[...] = ld[:, GLA_K_W:]

    mu = proj(OFF_MU, GMLP_W)
    vn = _rms(proj(OFF_MV, GMLP_W), gmg_ref[...]).astype(BF16)
    for c in range(x_ref.shape[0] // CHUNK):
        rows = slice(c * CHUNK, (c + 1) * CHUNK)
        for gi in range(GMLP_GROUPS):
            cols = slice(gi * GMLP_GROUP_DIM, (gi + 1) * GMLP_GROUP_DIM)
            z = _dot(ws_ref[gi], vn[rows, cols]) + bs_ref[gi]
            ym_ref[rows, cols] = (mu[rows, cols] * z).astype(BF16)


def _inproj(xs, mod, g, w_in_r, qk_g, cos_t, sin_t, gate_w_r, gate_b_r, ws, bs_b, gm_g, *, layer, mod_row, rope_row):
    t, d = xs.shape
    n_tiles = t // TM
    row = lambda i: (i, 0)
    const2 = lambda i: (0, 0)
    const3 = lambda i: (0, 0, 0)
    widths = [(ATTN_Q_W, BF16), (ATTN_KV_W, BF16), None, (GLA_K_W, F32), (GLA_K_W, F32),
              (GLA_V_W, BF16), (GLA_V_W, F32), (GLA_K_W, F32), (GLA_K_W, F32), (GMLP_W, BF16)]
    out_specs = [pl.BlockSpec((1, ATTN_KV_W, TM), lambda i: (i, 0, 0)) if w is None else pl.BlockSpec((TM, w[0]), row)
                 for w in widths]
    out_shape = [jax.ShapeDtypeStruct((n_tiles, ATTN_KV_W, TM), BF16) if w is None
                 else jax.ShapeDtypeStruct((t, w[0]), w[1]) for w in widths]
    return pl.pallas_call(
        _inproj_kernel,
        grid=(n_tiles,),
        in_specs=[
            pl.BlockSpec((TM, d), row),
            pl.BlockSpec((1, N_MOD, d), lambda i: (mod_row(i), 0, 0)),
            pl.BlockSpec((1, d), const2),
            pl.BlockSpec((None, d, IN_W_R), lambda i: (layer, 0, 0), pipeline_mode=pl.Buffered(1)),
            pl.BlockSpec((2, HEAD_DIM), const2),
            pl.BlockSpec((TM, HEAD_DIM), lambda i: (rope_row(i), 0)),
            pl.BlockSpec((TM, HEAD_DIM), lambda i: (rope_row(i), 0)),
            pl.BlockSpec((GLR_PAD, 2 * GLA_K_W), const2),
            pl.BlockSpec((1, 2 * GLA_K_W), const2),
            pl.BlockSpec((GMLP_GROUPS, CHUNK, CHUNK), const3),
            pl.BlockSpec((GMLP_GROUPS, CHUNK, GMLP_GROUP_DIM), const3),
            pl.BlockSpec((1, GMLP_W), const2),
        ],
        out_specs=out_specs,
        out_shape=out_shape,
        compiler_params=pltpu.CompilerParams(
            dimension_semantics=("parallel",), vmem_limit_bytes=VMEM_LIMIT),
        name="inproj",
    )(xs, mod, g.reshape(1, d), w_in_r, qk_g, cos_t, sin_t, gate_w_r, gate_b_r, ws, bs_b, gm_g.reshape(1, GMLP_W))


def _attn_kernel(*refs, n_lat_tiles):
    if n_lat_tiles:
        q_ref, kc_ref, vtc_ref, kl_ref, vtl_ref, o_ref, s_ref, off_ref, jump_ref, m_ref, acc_ref = refs
    else:
        q_ref, kc_ref, vtc_ref, o_ref, m_ref, acc_ref = refs
    tq = q_ref.shape[0]
    q = jnp.concatenate([q_ref[:, g * HEAD_DIM:(g + 1) * HEAD_DIM] for g in range(ATTN_GROUP)], axis=0)

    def scores(k):
        return _dot_t(k, q)

    def with_ones(vt):
        return jnp.concatenate([vt, jnp.ones((SUM_ROWS, vt.shape[1]), BF16)], axis=0)

    def lat_keys(t):
        start = pl.multiple_of(t * TK, TK)
        return kl_ref[pl.ds(start, TK), :]

    def ctx_tile():
        s = scores(kc_ref[...])
        m_new = jnp.max(s, axis=0, keepdims=True)
        m_ref[...] = m_new
        acc_ref[...] = _dot(with_ones(vtc_ref[0]), jnp.exp2(s - m_new).astype(BF16))

    def lagged_tile(t, carry):
        c = m_ref[...]
        s = scores(lat_keys(t))
        tmax = jnp.max(s, axis=0, keepdims=True)
        p = jnp.exp2(s - c)
        alpha = jnp.exp2(off_ref[...] - c)
        acc_ref[...] = alpha * acc_ref[...] + _dot(with_ones(vtl_ref[t]), p.astype(BF16))
        off_ref[...] = c
        jump_ref[...] = jnp.maximum(jump_ref[...], tmax - c)
        m_ref[...] = jnp.maximum(c, tmax)
        return carry

    def exact_tile(t, carry):
        s_ref[...] = scores(lat_keys(t))
        m_old = m_ref[...]
        m_new = jnp.maximum(m_old, jnp.max(s_ref[...], axis=0, keepdims=True))
        alpha = jnp.exp2(m_old - m_new)
        p = jnp.exp2(s_ref[...] - m_new)
        acc_ref[...] = alpha * acc_ref[...] + _dot(with_ones(vtl_ref[t]), p.astype(BF16))
        m_ref[...] = m_new
        return carry

    ctx_tile()
    if n_lat_tiles:
        off_ref[...] = m_ref[...]
        jump_ref[...] = jnp.zeros_like(jump_ref)
        lax.fori_loop(0, n_lat_tiles, lagged_tile, 0, unroll=8 if n_lat_tiles % 8 == 0 else 1)

        @pl.when(jnp.max(jump_ref[...]) > MAX_SAFE_JUMP)
        def _():
            ctx_tile()
            lax.fori_loop(0, n_lat_tiles, exact_tile, 0)
    out_t = acc_ref[0:HEAD_DIM, :] / acc_ref[HEAD_DIM:HEAD_DIM + 1, :]
    for g in range(ATTN_GROUP):
        o_ref[:, g * HEAD_DIM:(g + 1) * HEAD_DIM] = out_t[:, g * tq:(g + 1) * tq].T.astype(BF16)


def _attention(q, k, vt, *, batch, n_lat, n_ctx, latent):
    gw = ATTN_GROUP * HEAD_DIM
    ctx_blk0 = (batch * n_lat) // n_ctx
    lat_tiles = (batch * n_lat) // TK
    per_tile = TK // n_ctx
    kc_spec = pl.BlockSpec((n_ctx, HEAD_DIM), lambda b, kh, i: (ctx_blk0 + b, kh))
    vtc_spec = pl.BlockSpec((1, HEAD_DIM, n_ctx), lambda b, kh, i: (lat_tiles + b // per_tile, kh, b % per_tile))
    if latent:
        tq = TQ
        nq = n_lat // tq
        n_lat_tiles = n_lat // TK
        q_spec = pl.BlockSpec((tq, gw), lambda b, kh, i: (b * nq + i, kh))
        kl_spec = pl.BlockSpec((n_lat, HEAD_DIM), lambda b, kh, i: (b, kh))
        vtl_spec = pl.BlockSpec((n_lat_tiles, HEAD_DIM, TK), lambda b, kh, i: (b, kh, 0))
        in_specs = [q_spec, kc_spec, vtc_spec, kl_spec, vtl_spec]
        args = (q, k, vt, k, vt)
        out_rows = batch * n_lat
    else:
        tq = n_ctx
        nq = 1
        n_lat_tiles = 0
        q_spec = pl.BlockSpec((tq, gw), lambda b, kh, i: (ctx_blk0 + b, kh))
        in_specs = [q_spec, kc_spec, vtc_spec]
        args = (q, k, vt)
        out_rows = batch * n_ctx
    cols = ATTN_GROUP * tq
    scratch = [pltpu.VMEM((1, cols), F32), pltpu.VMEM((HEAD_DIM + SUM_ROWS, cols), F32)]
    if latent:
        scratch = [pltpu.VMEM((TK, cols), F32), pltpu.VMEM((1, cols), F32), pltpu.VMEM((1, cols), F32)] + scratch
    return pl.pallas_call(
        functools.partial(_attn_kernel, n_lat_tiles=n_lat_tiles),
        grid=(batch, ATTN_KV_HEADS, nq),
        in_specs=in_specs,
        out_specs=pl.BlockSpec((tq, gw), lambda b, kh, i: (b * nq + i, kh)),
        out_shape=jax.ShapeDtypeStruct((out_rows, ATTN_Q_W), BF16),
        scratch_shapes=scratch,
        compiler_params=pltpu.CompilerParams(
            dimension_semantics=("parallel", "parallel", "arbitrary"), vmem_limit_bytes=VMEM_LIMIT),
        name="attn_lat" if latent else "attn_ctx",
    )(*args)


def _gla_consts():
    idx = np.arange(CHUNK)
    tri = (idx[None, :] <= idx[:, None]).astype(np.float32)
    mats_f, mats_b = [tri], [tri.T]
    for lvl in range(1, N_LEVELS + 1):
        s = (2 * CHUNK) >> lvl
        base = (idx // s) * s
        mats_f.append(tri[base + s // 2 - 1])
        mats_b.append(tri.T[base + s // 2])
    cm = np.stack([np.concatenate(mats_f, 0), np.concatenate(mats_b, 0)])
    cm = np.concatenate([cm, cm], axis=-1)
    x = idx[:, None] ^ idx[None, :]
    hb = np.floor(np.log2(np.maximum(x, 1))).astype(np.int32)
    lv = np.where(x == 0, 0, N_LEVELS - hb)
    lv_f = np.where(idx[:, None] >= idx[None, :], lv, -1)
    lv_b = np.where(idx[:, None] <= idx[None, :], lv, -1)
    return cm, np.stack([lv_f, lv_b]).astype(np.int32)


def _gla_chunk(cm_ref, lv_ref, q_ref, k_ref, v_ref, g_ref, o_ref, st_ref, d, rows):
    g = g_ref[rows, :]
    g_hi = g.astype(BF16)
    g_lo = (g - g_hi.astype(F32)).astype(BF16)
    cums = _dot(cm_ref[d], jnp.concatenate([g_hi, g_lo], axis=0))
    cum = cums[0:CHUNK]
    q = q_ref[rows, :]
    k = k_ref[rows, :]
    lv = lv_ref[d]
    last = CHUNK - 1 if d == 0 else 0
    tail = cum[last:last + 1, :]
    lane = lax.broadcasted_iota(jnp.int32, (CHUNK, LANES), 1)
    low = lane < GLA_DK

    qs = [q.astype(BF16)]
    ks = [k.astype(BF16)]
    for lvl in range(1, N_LEVELS + 1):
        mid = cums[lvl * CHUNK:(lvl + 1) * CHUNK]
        qs.append((q * jnp.exp(jnp.minimum(cum - mid, 0.0))).astype(BF16))
        ks.append((k * jnp.exp(jnp.minimum(mid - cum, 0.0))).astype(BF16))
    q_in = (q * jnp.exp(cum)).astype(BF16)
    k_out = (k * jnp.exp(tail - cum)).astype(BF16)
    zero = jnp.zeros((CHUNK, LANES), BF16)

    for p in range(GLA_HEADS // 2):
        pl_sl = slice(p * LANES, (p + 1) * LANES)
        st = st_ref[d, p]
        st_b = st.astype(BF16)
        upd = []
        for hp in range(2):
            head = 2 * p + hp
            keep = low if hp == 0 else jnp.logical_not(low)
            a = jnp.zeros((CHUNK, CHUNK), F32)
            for lvl in range(N_LEVELS + 1):
                qm = jnp.where(keep, qs[lvl][:, pl_sl], zero)
                a = jnp.where(lv == lvl, _dot_t(qm, ks[lvl][:, pl_sl]), a)
            vh = v_ref[rows, head * GLA_DV:(head + 1) * GLA_DV]
            inter = _dot_t(jnp.where(keep, q_in[:, pl_sl], zero), st_b)
            o_ref[rows, head * GLA_DV:(head + 1) * GLA_DV] = inter + _dot(a.astype(BF16), vh)
            upd.append(_tdot(vh, k_out[:, pl_sl]))
        st_ref[d, p] = st * jnp.exp(tail[:, pl_sl]) + jnp.where(low, upd[0], upd[1])


def _gla_kernel(cm_ref, lv_ref, qf, kf, vf, gf, qb, kb, vb, gb, of_ref, ob_ref, st_ref):
    @pl.when(pl.program_id(1) == 0)
    def _():
        st_ref[...] = jnp.zeros_like(st_ref)

    n = qf.shape[0] // CHUNK
    for c in range(n):
        _gla_chunk(cm_ref, lv_ref, qf, kf, vf, gf, of_ref, st_ref, 0, slice(c * CHUNK, (c + 1) * CHUNK))
        cb = n - 1 - c
        _gla_chunk(cm_ref, lv_ref, qb, kb, vb, gb, ob_ref, st_ref, 1, slice(cb * CHUNK, (cb + 1) * CHUNK))


def _gla(gq, gk, gv, gf, gb, *, batch, n_lat, n_ctx):
    t = gq.shape[0]
    rows = GLA_STEP_CHUNKS * CHUNK
    cl, cc = n_lat // rows, n_ctx // rows
    ctx0 = batch * cl
    cm_np, lv_np = _gla_consts()
    cm = jnp.asarray(cm_np, BF16)
    lv = jnp.asarray(lv_np)

    def fwd(b, s):
        return (jnp.where(s < cc, ctx0 + b * cc + s, b * cl + s - cc), 0)

    def bwd(b, s):
        return (jnp.where(s < cc, ctx0 + b * cc + (cc - 1 - s), b * cl + (cl - 1 - (s - cc))), 0)

    def specs(m):
        return [pl.BlockSpec((rows, GLA_K_W), m), pl.BlockSpec((rows, GLA_K_W), m),
                pl.BlockSpec((rows, GLA_V_W), m), pl.BlockSpec((rows, GLA_K_W), m)]

    return pl.pallas_call(
        _gla_kernel,
        grid=(batch, cc + cl),
        in_specs=[pl.BlockSpec(cm.shape, lambda b, s: (0, 0, 0)), pl.BlockSpec(lv.shape, lambda b, s: (0, 0, 0))]
        + specs(fwd) + specs(bwd),
        out_specs=[pl.BlockSpec((rows, GLA_V_W), fwd), pl.BlockSpec((rows, GLA_V_W), bwd)],
        out_shape=[jax.ShapeDtypeStruct((t, GLA_V_W), F32)] * 2,
        scratch_shapes=[pltpu.VMEM((2, GLA_HEADS // 2, GLA_DV, LANES), F32)],
        compiler_params=pltpu.CompilerParams(
            dimension_semantics=("parallel", "arbitrary"), vmem_limit_bytes=VMEM_LIMIT),
        name="gla",
    )(cm, lv, gq, gk, gv, gf, gq, gk, gv, gb)


def _outproj_kernel(*refs, split_at):
    if split_at is None:
        x_ref, mod_ref, att_ref, of_ref, ob_ref, gr_ref, ym_ref, gg_ref, w_ref, o_ref = refs
        att = att_ref[...]
    else:
        x_ref, mod_ref, att_ref, attc_ref, of_ref, ob_ref, gr_ref, ym_ref, gg_ref, w_ref, o_ref = refs
        att = jnp.where(pl.program_id(0) >= split_at, attc_ref[...], att_ref[...])
    o = of_ref[...] + ob_ref[...]
    r = gr_ref[...]
    y = _dot(att, w_ref[0:ATTN_Q_W, :])
    for hh in range(GLA_HEADS):
        sl = slice(hh * GLA_DV, (hh + 1) * GLA_DV)
        gh = (_rms(o[:, sl], gg_ref[:, sl]) * _silu(r[:, sl])).astype(BF16)
        y += _dot(gh, w_ref[ATTN_Q_W + hh * GLA_DV:ATTN_Q_W + (hh + 1) * GLA_DV, :])
    y += _dot(ym_ref[...], w_ref[ATTN_Q_W + GLA_V_W:, :])
    o_ref[...] = x_ref[...] + mod_ref[0, 5:6, :] * y


def _outproj(xs, mod, att, att_c, o_f, o_b, gr, ym, gla_g, w_out, *, layer, n_tiles, mod_row):
    t, d = xs.shape
    row = lambda i: (i, 0)
    const2 = lambda i: (0, 0)
    if att_c is None:
        split_at = None
        att_specs = [pl.BlockSpec((TM, ATTN_Q_W), row)]
        att_args = (att,)
    else:
        split_at = att.shape[0] // TM
        att_specs = [pl.BlockSpec((TM, ATTN_Q_W), lambda i: (jnp.minimum(i, split_at - 1), 0)),
                     pl.BlockSpec((TM, ATTN_Q_W), lambda i: (jnp.maximum(i - split_at, 0), 0))]
        att_args = (att, att_c)
    return pl.pallas_call(
        functools.partial(_outproj_kernel, split_at=split_at),
        grid=(n_tiles,),
        in_specs=[
            pl.BlockSpec((TM, d), row),
            pl.BlockSpec((1, N_MOD, d), lambda i: (mod_row(i), 0, 0))] + att_specs + [
            pl.BlockSpec((TM, GLA_V_W), row),
            pl.BlockSpec((TM, GLA_V_W), row),
            pl.BlockSpec((TM, GLA_V_W), row),
            pl.BlockSpec((TM, GMLP_W), row),
            pl.BlockSpec((1, GLA_V_W), const2),
            pl.BlockSpec((None,) + w_out.shape[1:], lambda i: (layer, 0, 0), pipeline_mode=pl.Buffered(1)),
        ],
        out_specs=pl.BlockSpec((TM, d), row),
        out_shape=jax.ShapeDtypeStruct((n_tiles * TM, d), F32),
        compiler_params=pltpu.CompilerParams(
            dimension_semantics=("parallel",), vmem_limit_bytes=VMEM_LIMIT),
        name="outproj",
    )(xs, mod, *att_args, o_f, o_b, gr, ym, gla_g.reshape(1, GLA_V_W), w_out)


def _rope_tables(n_lat):
    rows = n_lat // GRID_W
    row = jnp.repeat(jnp.arange(rows, dtype=F32), GRID_W)
    col = jnp.broadcast_to(jnp.arange(GRID_W, dtype=F32), (rows, GRID_W)).reshape(-1)
    nf = HEAD_DIM // 4
    inv = ROPE_THETA ** (-jnp.arange(nf, dtype=F32) / nf)
    ar, ac = row[:, None] * inv, col[:, None] * inv
    cos = jnp.concatenate([jnp.cos(ar), jnp.cos(ar), jnp.cos(ac), jnp.cos(ac)], axis=-1)
    sin = jnp.concatenate([-jnp.sin(ar), jnp.sin(ar), -jnp.sin(ac), jnp.sin(ac)], axis=-1)
    cos = jnp.concatenate([cos, jnp.ones((TM, HEAD_DIM), F32)], axis=0)
    sin = jnp.concatenate([sin, jnp.zeros((TM, HEAD_DIM), F32)], axis=0)
    return cos, sin


def kernel(x, c, ctx, c_ctx, mod_w, mod_b, norm_g, ffn1_w_gu, ffn1_w_down, ffn2_w_gu, ffn2_w_down, w_in, w_out,
           qk_norm_g, gla_gate_w, gla_gate_b, gla_norm_g, gmlp_w_s, gmlp_b_s, gmlp_norm_g, final_norm_g):
    batch, n_lat, d = x.shape
    n_ctx = ctx.shape[1]
    depth = mod_w.shape[0]
    assert n_lat % TM == 0 and (batch * n_ctx) % TM == 0 and n_lat % TK == 0 and n_lat % TQ == 0
    assert n_ctx % (GLA_STEP_CHUNKS * CHUNK) == 0 and n_lat % n_ctx == 0 and batch + 1 <= 8 and TM == TK and TK % n_ctx == 0
    lat_tiles = batch * n_lat // TM
    all_tiles = lat_tiles + batch * n_ctx // TM
    tiles_per_batch = n_lat // TM

    def mod_row(i):
        return jnp.minimum(i // tiles_per_batch, batch)

    def rope_row(i):
        return jnp.where(i < lat_tiles, i % tiles_per_batch, tiles_per_batch)

    cc = jnp.zeros((8, d), F32).at[:batch].set(c).at[batch].set(c_ctx)
    mod_all = _modulation(cc, mod_w, mod_b).reshape(depth, 8, N_MOD, d)
    cos_t, sin_t = _rope_tables(n_lat)
    offs = np.cumsum([0, ATTN_Q_W, ATTN_KV_W, ATTN_KV_W, GLA_K_W, GLA_K_W, GLA_V_W, GLA_V_W, 2 * GLA_GATE_RANK,
                      GMLP_W, GMLP_W])
    lr0, lr1 = int(offs[7]), int(offs[8])
    w_in_r = jnp.concatenate(
        [w_in[:, :, :lr0], w_in[:, :, lr1:], w_in[:, :, lr0:lr1],
         jnp.zeros((depth, d, GLR_PAD - 2 * GLA_GATE_RANK), w_in.dtype)], axis=-1).astype(BF16)
    w_out_b = w_out.astype(BF16)
    f1gu, f1d = ffn1_w_gu.astype(BF16), ffn1_w_down.astype(BF16)
    f2gu, f2d = ffn2_w_gu.astype(BF16), ffn2_w_down.astype(BF16)
    xs = x.reshape(batch * n_lat, d)
    xc = ctx.reshape(batch * n_ctx, d)

    for l in range(depth):
        last = l == depth - 1
        mod = mod_all[l]
        gate_w_r = jnp.zeros((GLR_PAD, 2 * GLA_K_W), F32)
        gate_w_r = gate_w_r.at[:GLA_GATE_RANK, :GLA_K_W].set(gla_gate_w[l, 0])
        gate_w_r = gate_w_r.at[GLA_GATE_RANK:2 * GLA_GATE_RANK, GLA_K_W:].set(gla_gate_w[l, 1]).astype(BF16)
        gate_b_r = gla_gate_b[l].reshape(1, 2 * GLA_K_W)
        bs_b = jnp.broadcast_to(gmlp_b_s[l][..., None], gmlp_b_s.shape[1:] + (GMLP_GROUP_DIM,))

        xs = _ffn(xs, mod, norm_g[l, 0], f1gu, f1d, final_norm_g, layer=l,
                  sub=0, n_tiles=all_tiles, mod_row=mod_row, final=False, xc=xc)
        xc = None
        q, k, vt, gq, gk, gv, gr, gf, gb, ym = _inproj(
            xs, mod, norm_g[l, 1], w_in_r, qk_norm_g[l], cos_t, sin_t, gate_w_r, gate_b_r,
            gmlp_w_s[l].astype(BF16), bs_b, gmlp_norm_g[l], layer=l, mod_row=mod_row, rope_row=rope_row)
        att = _attention(q, k, vt, batch=batch, n_lat=n_lat, n_ctx=n_ctx, latent=True)
        o_f, o_b = _gla(gq, gk, gv, gf, gb, batch=batch, n_lat=n_lat, n_ctx=n_ctx)
        att_c = None if last else _attention(q, k, vt, batch=batch, n_lat=n_lat, n_ctx=n_ctx, latent=False)
        n_tiles = lat_tiles if last else all_tiles
        xs = _outproj(xs, mod, att, att_c, o_f, o_b, gr, ym, gla_norm_g[l], w_out_b,
                      layer=l, n_tiles=n_tiles, mod_row=mod_row)
        xs = _ffn(xs, mod, norm_g[l, 2], f2gu, f2d, final_norm_g, layer=l,
                  sub=2, n_tiles=n_tiles, mod_row=mod_row, final=last)
    return xs.reshape(batch, n_lat, d)
```

```python
import functools

import numpy as np
import jax
import jax.numpy as jnp
from jax import lax
from jax.experimental import pallas as pl
from jax.experimental.pallas import tpu as pltpu

F32 = jnp.float32
BF16 = jnp.bfloat16

EPS = 1e-6
N_MOD = 9
HEAD_DIM = 128
ATTN_HEADS = 8
ATTN_KV_HEADS = 2
ATTN_GROUP = ATTN_HEADS // ATTN_KV_HEADS
ROPE_THETA = 10000.0
GRID_W = 64
GLA_HEADS = 4
GLA_DK = 64
GLA_DV = 128
GLA_GATE_RANK = 16
GLA_TAU = 16.0
LOG2E = 1.4426950408889634
CHUNK = 128
GMLP_GROUPS = 4
GMLP_GROUP_DIM = 128

ATTN_Q_W = ATTN_HEADS * HEAD_DIM
ATTN_KV_W = ATTN_KV_HEADS * HEAD_DIM
GLA_K_W = GLA_HEADS * GLA_DK
GLA_V_W = GLA_HEADS * GLA_DV
GMLP_W = GMLP_GROUPS * GMLP_GROUP_DIM
LANES = 128
GLR_PAD = LANES
N_LEVELS = 7

OFF_AQ = 0
OFF_AK = OFF_AQ + ATTN_Q_W
OFF_AV = OFF_AK + ATTN_KV_W
OFF_GQ = OFF_AV + ATTN_KV_W
OFF_GK = OFF_GQ + GLA_K_W
OFF_GV = OFF_GK + GLA_K_W
OFF_GR = OFF_GV + GLA_V_W
OFF_MU = OFF_GR + GLA_V_W
OFF_MV = OFF_MU + GMLP_W
OFF_LR = OFF_MV + GMLP_W
IN_W_R = OFF_LR + GLR_PAD

TM = 512
TF = 512
TQ = 256
TK = 512
GLA_STEP_CHUNKS = 2
SUM_ROWS = 16
ATTN_UNROLL = 16
MAX_SAFE_JUMP = 100.0
MOD_TN = 2048
VMEM_LIMIT = 56 * 1024 * 1024


def _sigmoid(x):
    return 1.0 / (1.0 + jnp.exp(-x))


def _silu(x):
    return x * _sigmoid(x)


def _rms(x, g):
    ms = jnp.mean(x * x, axis=-1, keepdims=True)
    return x * lax.rsqrt(ms + EPS) * g


def _norm_mod(x, g, mod_ref, i):
    ms = jnp.mean(x * x, axis=-1, keepdims=True)
    gain = g * (1.0 + mod_ref[0, 3 * i + 1:3 * i + 2, :])
    return x * lax.rsqrt(ms + EPS) * gain + mod_ref[0, 3 * i:3 * i + 1, :]


def _dot(a, b):
    return jnp.dot(a, b, preferred_element_type=F32)


def _dot_t(a, b):
    return lax.dot_general(a, b, (((1,), (1,)), ((), ())), preferred_element_type=F32)


def _tdot(a, b):
    return lax.dot_general(a, b, (((0,), (0,)), ((), ())), preferred_element_type=F32)


def _mod_kernel(c_ref, w_ref, b_ref, o_ref):
    sc = _silu(c_ref[...]).astype(BF16)
    o_ref[0] = _dot(sc, w_ref[0].astype(BF16)) + b_ref[0]


def _modulation(cc, mod_w, mod_b):
    depth, d, n = mod_w.shape
    return pl.pallas_call(
        _mod_kernel,
        grid=(depth, n // MOD_TN),
        in_specs=[
            pl.BlockSpec((8, d), lambda l, j: (0, 0)),
            pl.BlockSpec((1, d, MOD_TN), lambda l, j: (l, 0, j)),
            pl.BlockSpec((1, 1, MOD_TN), lambda l, j: (l, 0, j)),
        ],
        out_specs=pl.BlockSpec((1, 8, MOD_TN), lambda l, j: (l, 0, j)),
        out_shape=jax.ShapeDtypeStruct((depth, 8, n), F32),
        compiler_params=pltpu.CompilerParams(
            dimension_semantics=("parallel", "parallel"), vmem_limit_bytes=VMEM_LIMIT),
        name="modulation",
    )(cc, mod_w, mod_b.reshape(depth, 1, n))


def _ffn_kernel(*refs, sub, final, split_at):
    if split_at is None:
        x_ref, mod_ref, g_ref, wg_ref, wu_ref, wd_ref, fg_ref, o_ref, h_ref, acc_ref = refs
        read_x = lambda: x_ref[...]
    else:
        x_ref, xc_ref, mod_ref, g_ref, wg_ref, wu_ref, wd_ref, fg_ref, o_ref, h_ref, acc_ref = refs
        is_ctx = pl.program_id(0) >= split_at
        read_x = lambda: jnp.where(is_ctx, xc_ref[...], x_ref[...])
    j = pl.program_id(1)

    @pl.when(j == 0)
    def _():
        h_ref[...] = _norm_mod(read_x(), g_ref[...], mod_ref, sub).astype(BF16)

    h = h_ref[...]
    a = _silu(_dot(h, wg_ref[...])) * _dot(h, wu_ref[...])
    d = _dot(a.astype(BF16), wd_ref[...])
    acc_ref[...] = jnp.where(j == 0, d, acc_ref[...] + d)

    @pl.when(j == pl.num_programs(1) - 1)
    def _():
        y = read_x() + mod_ref[0, 3 * sub + 2:3 * sub + 3, :] * (0.5 * acc_ref[...])
        if final:
            y = _rms(y, fg_ref[...])
        o_ref[...] = y


def _ffn(xs, mod, g, w_gu, w_down, final_g, *, layer, sub, n_tiles, mod_row, final, xc=None):
    t, d = xs.shape
    f = w_down.shape[1]
    nf = f // TF
    if xc is None:
        split_at = None
        x_specs = [pl.BlockSpec((TM, d), lambda i, j: (i, 0))]
        x_args = (xs,)
    else:
        split_at = t // TM
        x_specs = [pl.BlockSpec((TM, d), lambda i, j: (jnp.minimum(i, split_at - 1), 0)),
                   pl.BlockSpec((TM, d), lambda i, j: (jnp.maximum(i - split_at, 0), 0))]
        x_args = (xs, xc)
    kern = functools.partial(_ffn_kernel, sub=sub, final=final, split_at=split_at)
    return pl.pallas_call(
        kern,
        grid=(n_tiles, nf),
        in_specs=x_specs + [
            pl.BlockSpec((1, N_MOD, d), lambda i, j: (mod_row(i), 0, 0)),
            pl.BlockSpec((1, d), lambda i, j: (0, 0)),
            pl.BlockSpec((None, d, TF), lambda i, j: (layer, 0, j)),
            pl.BlockSpec((None, d, TF), lambda i, j: (layer, 0, j + nf)),
            pl.BlockSpec((None, TF, d), lambda i, j: (layer, j, 0)),
            pl.BlockSpec((1, d), lambda i, j: (0, 0)),
        ],
        out_specs=pl.BlockSpec((TM, d), lambda i, j: (i, 0)),
        out_shape=jax.ShapeDtypeStruct((n_tiles * TM, d), F32),
        scratch_shapes=[pltpu.VMEM((TM, d), BF16), pltpu.VMEM((TM, d), F32)],
        compiler_params=pltpu.CompilerParams(
            dimension_semantics=("parallel", "arbitrary"), vmem_limit_bytes=VMEM_LIMIT),
        name="ffn_final" if final else "ffn",
    )(*x_args, mod, g.reshape(1, d), w_gu, w_gu, w_down, final_g.reshape(1, d))


def _rope(x, cos, sin_signed, lane_low):
    partner = jnp.where(lane_low, pltpu.roll(x, LANES - 32, 1), pltpu.roll(x, 32, 1))
    return x * cos + partner * sin_signed


def _log_sigmoid(x):
    return jnp.minimum(x, 0.0) - jnp.log(1.0 + jnp.exp(-jnp.abs(x)))


def _inproj_kernel(x_ref, mod_ref, g_ref, w_ref, qkg_ref, cos_ref, sin_ref, gw_ref, gbias_ref,
                   ws_ref, bs_ref, gmg_ref,
                   q_ref, k_ref, vt_ref, gq_ref, gk_ref, gv_ref, gr_ref, gf_ref, gb_ref, ym_ref):
    half = x_ref.shape[0] // 2
    for r in range(2):
        _inproj_rows(slice(r * half, (r + 1) * half), x_ref, mod_ref, g_ref, w_ref, qkg_ref, cos_ref, sin_ref,
                     gw_ref, gbias_ref, ws_ref, bs_ref, gmg_ref,
                     q_ref, k_ref, vt_ref, gq_ref, gk_ref, gv_ref, gr_ref, gf_ref, gb_ref, ym_ref)


def _inproj_rows(rs, x_ref, mod_ref, g_ref, w_ref, qkg_ref, cos_ref, sin_ref, gw_ref, gbias_ref,
                 ws_ref, bs_ref, gmg_ref,
                 q_ref, k_ref, vt_ref, gq_ref, gk_ref, gv_ref, gr_ref, gf_ref, gb_ref, ym_ref):
    h = _norm_mod(x_ref[rs, :], g_ref[...], mod_ref, 1).astype(BF16)
    cos = cos_ref[rs, :]
    sin = sin_ref[rs, :]
    lane = lax.broadcasted_iota(jnp.int32, cos.shape, 1)
    lane_low = (lane & 63) < 32

    def proj(off, width):
        return _dot(h, w_ref[:, off:off + width])

    scale = HEAD_DIM ** -0.5 * LOG2E
    zq = proj(OFF_AQ, ATTN_Q_W)
    for hh in range(ATTN_HEADS):
        sl = slice(hh * HEAD_DIM, (hh + 1) * HEAD_DIM)
        qh = _rope(_rms(zq[:, sl], qkg_ref[0:1, :]), cos, sin, lane_low)
        q_ref[rs, sl] = (qh * scale).astype(BF16)
    zk = proj(OFF_AK, ATTN_KV_W)
    for hh in range(ATTN_KV_HEADS):
        sl = slice(hh * HEAD_DIM, (hh + 1) * HEAD_DIM)
        k_ref[rs, sl] = _rope(_rms(zk[:, sl], qkg_ref[1:2, :]), cos, sin, lane_low).astype(BF16)
    vt_ref[0, :, rs] = proj(OFF_AV, ATTN_KV_W).T.astype(BF16)

    gq_ref[rs, :] = proj(OFF_GQ, GLA_K_W) * (GLA_DK ** -0.5)
    gk_ref[rs, :] = proj(OFF_GK, GLA_K_W)
    gv_ref[rs, :] = proj(OFF_GV, GLA_V_W).astype(BF16)
    gr_ref[rs, :] = proj(OFF_GR, GLA_V_W)
    lr = proj(OFF_LR, GLR_PAD).astype(BF16)
    logits = _dot(lr, gw_ref[...]) + gbias_ref[...]
    ld = _log_sigmoid(logits) * (1.0 / GLA_TAU)
    gf_ref[rs, :] = ld[:, :GLA_K_W]
    gb_ref[rs, :] = ld[:, GLA_K_W:]

    mu = proj(OFF_MU, GMLP_W)
    vn = _rms(proj(OFF_MV, GMLP_W), gmg_ref[...]).astype(BF16)
    for c in range((rs.stop - rs.start) // CHUNK):
        rows = slice(c * CHUNK, (c + 1) * CHUNK)
        out_rows = slice(rs.start + c * CHUNK, rs.start + (c + 1) * CHUNK)
        for gi in range(GMLP_GROUPS):
            cols = slice(gi * GMLP_GROUP_DIM, (gi + 1) * GMLP_GROUP_DIM)
            z = _dot(ws_ref[gi], vn[rows, cols]) + bs_ref[gi]
            ym_ref[out_rows, cols] = (mu[rows, cols] * z).astype(BF16)


def _inproj(xs, mod, g, w_in_r, qk_g, cos_t, sin_t, gate_w_r, gate_b_r, ws, bs_b, gm_g, *, layer, mod_row, rope_row):
    t, d = xs.shape
    n_tiles = t // TM
    row = lambda i: (i, 0)
    const2 = lambda i: (0, 0)
    const3 = lambda i: (0, 0, 0)
    widths = [(ATTN_Q_W, BF16), (ATTN_KV_W, BF16), None, (GLA_K_W, F32), (GLA_K_W, F32),
              (GLA_V_W, BF16), (GLA_V_W, F32), (GLA_K_W, F32), (GLA_K_W, F32), (GMLP_W, BF16)]
    out_specs = [pl.BlockSpec((1, ATTN_KV_W, TM), lambda i: (i, 0, 0)) if w is None else pl.BlockSpec((TM, w[0]), row)
                 for w in widths]
    out_shape = [jax.ShapeDtypeStruct((n_tiles, ATTN_KV_W, TM), BF16) if w is None
                 else jax.ShapeDtypeStruct((t, w[0]), w[1]) for w in widths]
    return pl.pallas_call(
        _inproj_kernel,
        grid=(n_tiles,),
        in_specs=[
            pl.BlockSpec((TM, d), row),
            pl.BlockSpec((1, N_MOD, d), lambda i: (mod_row(i), 0, 0)),
            pl.BlockSpec((1, d), const2),
            pl.BlockSpec((None, d, IN_W_R), lambda i: (layer, 0, 0), pipeline_mode=pl.Buffered(1)),
            pl.BlockSpec((2, HEAD_DIM), const2),
            pl.BlockSpec((TM, HEAD_DIM), lambda i: (rope_row(i), 0)),
            pl.BlockSpec((TM, HEAD_DIM), lambda i: (rope_row(i), 0)),
            pl.BlockSpec((GLR_PAD, 2 * GLA_K_W), const2),
            pl.BlockSpec((1, 2 * GLA_K_W), const2),
            pl.BlockSpec((GMLP_GROUPS, CHUNK, CHUNK), const3),
            pl.BlockSpec((GMLP_GROUPS, CHUNK, GMLP_GROUP_DIM), const3),
            pl.BlockSpec((1, GMLP_W), const2),
        ],
        out_specs=out_specs,
        out_shape=out_shape,
        compiler_params=pltpu.CompilerParams(
            dimension_semantics=("parallel",), vmem_limit_bytes=VMEM_LIMIT),
        name="inproj",
    )(xs, mod, g.reshape(1, d), w_in_r, qk_g, cos_t, sin_t, gate_w_r, gate_b_r, ws, bs_b, gm_g.reshape(1, GMLP_W))


def _attn_kernel(*refs, n_lat_tiles):
    if n_lat_tiles:
        q_ref, kc_ref, vtc_ref, kl_ref, vtl_ref, o_ref, s_ref, off_ref, jump_ref, m_ref, acc_ref = refs
    else:
        q_ref, kc_ref, vtc_ref, o_ref, m_ref, acc_ref = refs
    tq = q_ref.shape[0]
    q = jnp.concatenate([q_ref[:, g * HEAD_DIM:(g + 1) * HEAD_DIM] for g in range(ATTN_GROUP)], axis=0)

    def scores(k):
        return _dot_t(k, q)

    def with_ones(vt):
        return jnp.concatenate([vt, jnp.ones((SUM_ROWS, vt.shape[1]), BF16)], axis=0)

    def lat_keys(t):
        start = pl.multiple_of(t * TK, TK)
        return kl_ref[pl.ds(start, TK), :]

    def ctx_tile():
        s = scores(kc_ref[...])
        m_new = jnp.max(s, axis=0, keepdims=True)
        m_ref[...] = m_new
        acc_ref[...] = _dot(with_ones(vtc_ref[0]), jnp.exp2(s - m_new).astype(BF16))

    def lagged_tile(t, carry):
        c = m_ref[...]
        s = scores(lat_keys(t))
        tmax = jnp.max(s, axis=0, keepdims=True)
        p = jnp.exp2(s - c)
        alpha = jnp.exp2(off_ref[...] - c)
        acc_ref[...] = alpha * acc_ref[...] + _dot(with_ones(vtl_ref[t]), p.astype(BF16))
        off_ref[...] = c
        jump_ref[...] = jnp.maximum(jump_ref[...], tmax - c)
        m_ref[...] = jnp.maximum(c, tmax)
        return carry

    def exact_tile(t, carry):
        s_ref[...] = scores(lat_keys(t))
        m_old = m_ref[...]
        m_new = jnp.maximum(m_old, jnp.max(s_ref[...], axis=0, keepdims=True))
        alpha = jnp.exp2(m_old - m_new)
        p = jnp.exp2(s_ref[...] - m_new)
        acc_ref[...] = alpha * acc_ref[...] + _dot(with_ones(vtl_ref[t]), p.astype(BF16))
        m_ref[...] = m_new
        return carry

    ctx_tile()
    if n_lat_tiles:
        off_ref[...] = m_ref[...]
        jump_ref[...] = jnp.zeros_like(jump_ref)
        lax.fori_loop(0, n_lat_tiles, lagged_tile, 0, unroll=ATTN_UNROLL if n_lat_tiles % ATTN_UNROLL == 0 else 1)

        @pl.when(jnp.max(jump_ref[...]) > MAX_SAFE_JUMP)
        def _():
            ctx_tile()
            lax.fori_loop(0, n_lat_tiles, exact_tile, 0)
    out_t = acc_ref[0:HEAD_DIM, :] / acc_ref[HEAD_DIM:HEAD_DIM + 1, :]
    for g in range(ATTN_GROUP):
        o_ref[:, g * HEAD_DIM:(g + 1) * HEAD_DIM] = out_t[:, g * tq:(g + 1) * tq].T.astype(BF16)


def _attention(q, k, vt, *, batch, n_lat, n_ctx, latent):
    gw = ATTN_GROUP * HEAD_DIM
    ctx_blk0 = (batch * n_lat) // n_ctx
    lat_tiles = (batch * n_lat) // TK
    per_tile = TK // n_ctx
    kc_spec = pl.BlockSpec((n_ctx, HEAD_DIM), lambda b, kh, i: (ctx_blk0 + b, kh))
    vtc_spec = pl.BlockSpec((1, HEAD_DIM, n_ctx), lambda b, kh, i: (lat_tiles + b // per_tile, kh, b % per_tile))
    if latent:
        tq = TQ
        nq = n_lat // tq
        n_lat_tiles = n_lat // TK
        q_spec = pl.BlockSpec((tq, gw), lambda b, kh, i: (b * nq + i, kh))
        kl_spec = pl.BlockSpec((n_lat, HEAD_DIM), lambda b, kh, i: (b, kh))
        vtl_spec = pl.BlockSpec((n_lat_tiles, HEAD_DIM, TK), lambda b, kh, i: (b, kh, 0))
        in_specs = [q_spec, kc_spec, vtc_spec, kl_spec, vtl_spec]
        args = (q, k, vt, k, vt)
        out_rows = batch * n_lat
    else:
        tq = n_ctx
        nq = 1
        n_lat_tiles = 0
        q_spec = pl.BlockSpec((tq, gw), lambda b, kh, i: (ctx_blk0 + b, kh))
        in_specs = [q_spec, kc_spec, vtc_spec]
        args = (q, k, vt)
        out_rows = batch * n_ctx
    cols = ATTN_GROUP * tq
    scratch = [pltpu.VMEM((1, cols), F32), pltpu.VMEM((HEAD_DIM + SUM_ROWS, cols), F32)]
    if latent:
        scratch = [pltpu.VMEM((TK, cols), F32), pltpu.VMEM((1, cols), F32), pltpu.VMEM((1, cols), F32)] + scratch
    return pl.pallas_call(
        functools.partial(_attn_kernel, n_lat_tiles=n_lat_tiles),
        grid=(batch, ATTN_KV_HEADS, nq),
        in_specs=in_specs,
        out_specs=pl.BlockSpec((tq, gw), lambda b, kh, i: (b * nq + i, kh)),
        out_shape=jax.ShapeDtypeStruct((out_rows, ATTN_Q_W), BF16),
        scratch_shapes=scratch,
        compiler_params=pltpu.CompilerParams(
            dimension_semantics=("parallel", "parallel", "arbitrary"), vmem_limit_bytes=VMEM_LIMIT),
        name="attn_lat" if latent else "attn_ctx",
    )(*args)


def _gla_consts():
    idx = np.arange(CHUNK)
    tri = (idx[None, :] <= idx[:, None]).astype(np.float32)
    mats_f, mats_b = [tri], [tri.T]
    for lvl in range(1, N_LEVELS + 1):
        s = (2 * CHUNK) >> lvl
        base = (idx // s) * s
        mats_f.append(tri[base + s // 2 - 1])
        mats_b.append(tri.T[base + s // 2])
    cm = np.stack([np.concatenate(mats_f, 0), np.concatenate(mats_b, 0)])
    cm = np.concatenate([cm, cm], axis=-1)
    x = idx[:, None] ^ idx[None, :]
    hb = np.floor(np.log2(np.maximum(x, 1))).astype(np.int32)
    lv = np.where(x == 0, 0, N_LEVELS - hb)
    lv_f = np.where(idx[:, None] >= idx[None, :], lv, -1)
    lv_b = np.where(idx[:, None] <= idx[None, :], lv, -1)
    return cm, np.stack([lv_f, lv_b]).astype(np.int32)


def _gla_chunk(cm_ref, lv_ref, q_ref, k_ref, v_ref, g_ref, o_ref, st_ref, d, rows):
    g = g_ref[rows, :]
    g_hi = g.astype(BF16)
    g_lo = (g - g_hi.astype(F32)).astype(BF16)
    cums = _dot(cm_ref[d], jnp.concatenate([g_hi, g_lo], axis=0))
    cum = cums[0:CHUNK]
    q = q_ref[rows, :]
    k = k_ref[rows, :]
    lv = lv_ref[d]
    last = CHUNK - 1 if d == 0 else 0
    tail = cum[last:last + 1, :]
    lane = lax.broadcasted_iota(jnp.int32, (CHUNK, LANES), 1)
    low = lane < GLA_DK

    qs = [q.astype(BF16)]
    ks = [k.astype(BF16)]
    for lvl in range(1, N_LEVELS + 1):
        mid = cums[lvl * CHUNK:(lvl + 1) * CHUNK]
        qs.append((q * jnp.exp(jnp.minimum(cum - mid, 0.0))).astype(BF16))
        ks.append((k * jnp.exp(jnp.minimum(mid - cum, 0.0))).astype(BF16))
    q_in = (q * jnp.exp(cum)).astype(BF16)
    k_out = (k * jnp.exp(tail - cum)).astype(BF16)
    zero = jnp.zeros((CHUNK, LANES), BF16)
    lv2 = jnp.concatenate([lv, lv], axis=1)

    def per_head_rows(x):
        return jnp.concatenate([jnp.where(low, x, zero), jnp.where(low, zero, x)], axis=0)

    for p in range(GLA_HEADS // 2):
        pl_sl = slice(p * LANES, (p + 1) * LANES)
        pv_sl = slice(2 * p * GLA_DV, 2 * (p + 1) * GLA_DV)
        st = st_ref[d, p]
        a = jnp.zeros((CHUNK, 2 * CHUNK), F32)
        for lvl in range(N_LEVELS + 1):
            a = jnp.where(lv2 == lvl, _dot_t(qs[lvl][:, pl_sl], per_head_rows(ks[lvl][:, pl_sl])), a)
        v2 = v_ref[rows, pv_sl]
        zv = jnp.zeros((CHUNK, GLA_DV), BF16)
        v_bd = jnp.concatenate([jnp.concatenate([v2[:, :GLA_DV], zv], axis=1),
                                jnp.concatenate([zv, v2[:, GLA_DV:]], axis=1)], axis=0)
        inter = _dot_t(per_head_rows(q_in[:, pl_sl]), st.astype(BF16))
        o_ref[rows, pv_sl] = (_dot(a.astype(BF16), v_bd)
                              + jnp.concatenate([inter[:CHUNK], inter[CHUNK:]], axis=1))
        upd = _tdot(v2, k_out[:, pl_sl])
        st_ref[d, p] = st * jnp.exp(tail[:, pl_sl]) + jnp.where(low, upd[:GLA_DV], upd[GLA_DV:])


def _gla_kernel(cm_ref, lv_ref, qf, kf, vf, gf, qb, kb, vb, gb, of_ref, ob_ref, st_ref):
    @pl.when(pl.program_id(1) == 0)
    def _():
        st_ref[...] = jnp.zeros_like(st_ref)

    n = qf.shape[0] // CHUNK
    for c in range(n):
        _gla_chunk(cm_ref, lv_ref, qf, kf, vf, gf, of_ref, st_ref, 0, slice(c * CHUNK, (c + 1) * CHUNK))
        cb = n - 1 - c
        _gla_chunk(cm_ref, lv_ref, qb, kb, vb, gb, ob_ref, st_ref, 1, slice(cb * CHUNK, (cb + 1) * CHUNK))


def _gla(gq, gk, gv, gf, gb, *, batch, n_lat, n_ctx):
    t = gq.shape[0]
    rows = GLA_STEP_CHUNKS * CHUNK
    cl, cc = n_lat // rows, n_ctx // rows
    ctx0 = batch * cl
    cm_np, lv_np = _gla_consts()
    cm = jnp.asarray(cm_np, BF16)
    lv = jnp.asarray(lv_np)

    def fwd(b, s):
        return (jnp.where(s < cc, ctx0 + b * cc + s, b * cl + s - cc), 0)

    def bwd(b, s):
        return (jnp.where(s < cc, ctx0 + b * cc + (cc - 1 - s), b * cl + (cl - 1 - (s - cc))), 0)

    def specs(m):
        return [pl.BlockSpec((rows, GLA_K_W), m), pl.BlockSpec((rows, GLA_K_W), m),
                pl.BlockSpec((rows, GLA_V_W), m), pl.BlockSpec((rows, GLA_K_W), m)]

    return pl.pallas_call(
        _gla_kernel,
        grid=(batch, cc + cl),
        in_specs=[pl.BlockSpec(cm.shape, lambda b, s: (0, 0, 0)), pl.BlockSpec(lv.shape, lambda b, s: (0, 0, 0))]
        + specs(fwd) + specs(bwd),
        out_specs=[pl.BlockSpec((rows, GLA_V_W), fwd), pl.BlockSpec((rows, GLA_V_W), bwd)],
        out_shape=[jax.ShapeDtypeStruct((t, GLA_V_W), F32)] * 2,
        scratch_shapes=[pltpu.VMEM((2, GLA_HEADS // 2, GLA_DV, LANES), F32)],
        compiler_params=pltpu.CompilerParams(
            dimension_semantics=("parallel", "arbitrary"), vmem_limit_bytes=VMEM_LIMIT),
        name="gla",
    )(cm, lv, gq, gk, gv, gf, gq, gk, gv, gb)


def _outproj_kernel(*refs, split_at):
    if split_at is None:
        x_ref, mod_ref, att_ref, of_ref, ob_ref, gr_ref, ym_ref, gg_ref, w_ref, o_ref = refs
        att = att_ref[...]
    else:
        x_ref, mod_ref, att_ref, attc_ref, of_ref, ob_ref, gr_ref, ym_ref, gg_ref, w_ref, o_ref = refs
        att = jnp.where(pl.program_id(0) >= split_at, attc_ref[...], att_ref[...])
    o = of_ref[...] + ob_ref[...]
    r = gr_ref[...]
    y = _dot(att, w_ref[0:ATTN_Q_W, :])
    for hh in range(GLA_HEADS):
        sl = slice(hh * GLA_DV, (hh + 1) * GLA_DV)
        gh = (_rms(o[:, sl], gg_ref[:, sl]) * _silu(r[:, sl])).astype(BF16)
        y += _dot(gh, w_ref[ATTN_Q_W + hh * GLA_DV:ATTN_Q_W + (hh + 1) * GLA_DV, :])
    y += _dot(ym_ref[...], w_ref[ATTN_Q_W + GLA_V_W:, :])
    o_ref[...] = x_ref[...] + mod_ref[0, 5:6, :] * y


def _outproj(xs, mod, att, att_c, o_f, o_b, gr, ym, gla_g, w_out, *, layer, n_tiles, mod_row):
    t, d = xs.shape
    row = lambda i: (i, 0)
    const2 = lambda i: (0, 0)
    if att_c is None:
        split_at = None
        att_specs = [pl.BlockSpec((TM, ATTN_Q_W), row)]
        att_args = (att,)
    else:
        split_at = att.shape[0] // TM
        att_specs = [pl.BlockSpec((TM, ATTN_Q_W), lambda i: (jnp.minimum(i, split_at - 1), 0)),
                     pl.BlockSpec((TM, ATTN_Q_W), lambda i: (jnp.maximum(i - split_at, 0), 0))]
        att_args = (att, att_c)
    return pl.pallas_call(
        functools.partial(_outproj_kernel, split_at=split_at),
        grid=(n_tiles,),
        in_specs=[
            pl.BlockSpec((TM, d), row),
            pl.BlockSpec((1, N_MOD, d), lambda i: (mod_row(i), 0, 0))] + att_specs + [
            pl.BlockSpec((TM, GLA_V_W), row),
            pl.BlockSpec((TM, GLA_V_W), row),
            pl.BlockSpec((TM, GLA_V_W), row),
            pl.BlockSpec((TM, GMLP_W), row),
            pl.BlockSpec((1, GLA_V_W), const2),
            pl.BlockSpec((None,) + w_out.shape[1:], lambda i: (layer, 0, 0), pipeline_mode=pl.Buffered(1)),
        ],
        out_specs=pl.BlockSpec((TM, d), row),
        out_shape=jax.ShapeDtypeStruct((n_tiles * TM, d), F32),
        compiler_params=pltpu.CompilerParams(
            dimension_semantics=("parallel",), vmem_limit_bytes=VMEM_LIMIT),
        name="outproj",
    )(xs, mod, *att_args, o_f, o_b, gr, ym, gla_g.reshape(1, GLA_V_W), w_out)


def _rope_tables(n_lat):
    rows = n_lat // GRID_W
    row = jnp.repeat(jnp.arange(rows, dtype=F32), GRID_W)
    col = jnp.broadcast_to(jnp.arange(GRID_W, dtype=F32), (rows, GRID_W)).reshape(-1)
    nf = HEAD_DIM // 4
    inv = ROPE_THETA ** (-jnp.arange(nf, dtype=F32) / nf)
    ar, ac = row[:, None] * inv, col[:, None] * inv
    cos = jnp.concatenate([jnp.cos(ar), jnp.cos(ar), jnp.cos(ac), jnp.cos(ac)], axis=-1)
    sin = jnp.concatenate([-jnp.sin(ar), jnp.sin(ar), -jnp.sin(ac), jnp.sin(ac)], axis=-1)
    cos = jnp.concatenate([cos, jnp.ones((TM, HEAD_DIM), F32)], axis=0)
    sin = jnp.concatenate([sin, jnp.zeros((TM, HEAD_DIM), F32)], axis=0)
    return cos, sin


def kernel(x, c, ctx, c_ctx, mod_w, mod_b, norm_g, ffn1_w_gu, ffn1_w_down, ffn2_w_gu, ffn2_w_down, w_in, w_out,
           qk_norm_g, gla_gate_w, gla_gate_b, gla_norm_g, gmlp_w_s, gmlp_b_s, gmlp_norm_g, final_norm_g):
    batch, n_lat, d = x.shape
    n_ctx = ctx.shape[1]
    depth = mod_w.shape[0]
    assert n_lat % TM == 0 and (batch * n_ctx) % TM == 0 and n_lat % TK == 0 and n_lat % TQ == 0
    assert n_ctx % (GLA_STEP_CHUNKS * CHUNK) == 0 and n_lat % n_ctx == 0 and batch + 1 <= 8 and TM == TK and TK % n_ctx == 0
    lat_tiles = batch * n_lat // TM
    all_tiles = lat_tiles + batch * n_ctx // TM
    tiles_per_batch = n_lat // TM

    def mod_row(i):
        return jnp.minimum(i // tiles_per_batch, batch)

    def rope_row(i):
        return jnp.where(i < lat_tiles, i % tiles_per_batch, tiles_per_batch)

    cc = jnp.zeros((8, d), F32).at[:batch].set(c).at[batch].set(c_ctx)
    mod_all = _modulation(cc, mod_w, mod_b).reshape(depth, 8, N_MOD, d)
    cos_t, sin_t = _rope_tables(n_lat)
    offs = np.cumsum([0, ATTN_Q_W, ATTN_KV_W, ATTN_KV_W, GLA_K_W, GLA_K_W, GLA_V_W, GLA_V_W, 2 * GLA_GATE_RANK,
                      GMLP_W, GMLP_W])
    lr0, lr1 = int(offs[7]), int(offs[8])
    w_in_r = jnp.concatenate(
        [w_in[:, :, :lr0], w_in[:, :, lr1:], w_in[:, :, lr0:lr1],
         jnp.zeros((depth, d, GLR_PAD - 2 * GLA_GATE_RANK), w_in.dtype)], axis=-1).astype(BF16)
    w_out_b = w_out.astype(BF16)
    f1gu, f1d = ffn1_w_gu.astype(BF16), ffn1_w_down.astype(BF16)
    f2gu, f2d = ffn2_w_gu.astype(BF16), ffn2_w_down.astype(BF16)
    xs = x.reshape(batch * n_lat, d)
    xc = ctx.reshape(batch * n_ctx, d)

    for l in range(depth):
        last = l == depth - 1
        mod = mod_all[l]
        gate_w_r = jnp.zeros((GLR_PAD, 2 * GLA_K_W), F32)
        gate_w_r = gate_w_r.at[:GLA_GATE_RANK, :GLA_K_W].set(gla_gate_w[l, 0])
        gate_w_r = gate_w_r.at[GLA_GATE_RANK:2 * GLA_GATE_RANK, GLA_K_W:].set(gla_gate_w[l, 1]).astype(BF16)
        gate_b_r = gla_gate_b[l].reshape(1, 2 * GLA_K_W)
        bs_b = jnp.broadcast_to(gmlp_b_s[l][..., None], gmlp_b_s.shape[1:] + (GMLP_GROUP_DIM,))

        xs = _ffn(xs, mod, norm_g[l, 0], f1gu, f1d, final_norm_g, layer=l,
                  sub=0, n_tiles=all_tiles, mod_row=mod_row, final=False, xc=xc)
        xc = None
        q, k, vt, gq, gk, gv, gr, gf, gb, ym = _inproj(
            xs, mod, norm_g[l, 1], w_in_r, qk_norm_g[l], cos_t, sin_t, gate_w_r, gate_b_r,
            gmlp_w_s[l].astype(BF16), bs_b, gmlp_norm_g[l], layer=l, mod_row=mod_row, rope_row=rope_row)
        att = _attention(q, k, vt, batch=batch, n_lat=n_lat, n_ctx=n_ctx, latent=True)
        o_f, o_b = _gla(gq, gk, gv, gf, gb, batch=batch, n_lat=n_lat, n_ctx=n_ctx)
        att_c = None if last else _attention(q, k, vt, batch=batch, n_lat=n_lat, n_ctx=n_ctx, latent=False)
        n_tiles = lat_tiles if last else all_tiles
        xs = _outproj(xs, mod, att, att_c, o_f, o_b, gr, ym, gla_norm_g[l], w_out_b,
                      layer=l, n_tiles=n_tiles, mod_row=mod_row)
        xs = _ffn(xs, mod, norm_g[l, 2], f2gu, f2d, final_norm_g, layer=l,
                  sub=2, n_tiles=n_tiles, mod_row=mod_row, final=last)
    return xs.reshape(batch, n_lat, d)
```

```python
import functools

import numpy as np
import jax
import jax.numpy as jnp
from jax import lax
from jax.experimental import pallas as pl
from jax.experimental.pallas import tpu as pltpu

F32 = jnp.float32
BF16 = jnp.bfloat16

EPS = 1e-6
N_MOD = 9
HEAD_DIM = 128
ATTN_HEADS = 8
ATTN_KV_HEADS = 2
ATTN_GROUP = ATTN_HEADS // ATTN_KV_HEADS
ROPE_THETA = 10000.0
GRID_W = 64
GLA_HEADS = 4
GLA_DK = 64
GLA_DV = 128
GLA_GATE_RANK = 16
GLA_TAU = 16.0
LOG2E = 1.4426950408889634
CHUNK = 128
GMLP_GROUPS = 4
GMLP_GROUP_DIM = 128

ATTN_Q_W = ATTN_HEADS * HEAD_DIM
ATTN_KV_W = ATTN_KV_HEADS * HEAD_DIM
GLA_K_W = GLA_HEADS * GLA_DK
GLA_V_W = GLA_HEADS * GLA_DV
GMLP_W = GMLP_GROUPS * GMLP_GROUP_DIM
LANES = 128
GLR_PAD = LANES
N_LEVELS = 7

OFF_AQ = 0
OFF_AK = OFF_AQ + ATTN_Q_W
OFF_AV = OFF_AK + ATTN_KV_W
OFF_GQ = OFF_AV + ATTN_KV_W
OFF_GK = OFF_GQ + GLA_K_W
OFF_GV = OFF_GK + GLA_K_W
OFF_GR = OFF_GV + GLA_V_W
OFF_MU = OFF_GR + GLA_V_W
OFF_MV = OFF_MU + GMLP_W
OFF_LR = OFF_MV + GMLP_W
IN_W_R = OFF_LR + GLR_PAD

TM = 512
TF = 512
TM_FFN = 1024
TQ = 256
TK = 512
GLA_STEP_CHUNKS = 2
SUM_ROWS = 16
ATTN_UNROLL = 16
MAX_SAFE_JUMP = 100.0
MOD_TN = 2048
VMEM_LIMIT = 56 * 1024 * 1024


def _sigmoid(x):
    return 1.0 / (1.0 + jnp.exp(-x))


def _silu(x):
    return x * _sigmoid(x)


def _rms(x, g):
    ms = jnp.mean(x * x, axis=-1, keepdims=True)
    return x * lax.rsqrt(ms + EPS) * g


def _norm_mod(x, g, mod_ref, i):
    ms = jnp.mean(x * x, axis=-1, keepdims=True)
    gain = g * (1.0 + mod_ref[0, 3 * i + 1:3 * i + 2, :])
    return x * lax.rsqrt(ms + EPS) * gain + mod_ref[0, 3 * i:3 * i + 1, :]


def _dot(a, b):
    return jnp.dot(a, b, preferred_element_type=F32)


def _dot_t(a, b):
    return lax.dot_general(a, b, (((1,), (1,)), ((), ())), preferred_element_type=F32)


def _tdot(a, b):
    return lax.dot_general(a, b, (((0,), (0,)), ((), ())), preferred_element_type=F32)


def _mod_kernel(c_ref, w_ref, b_ref, o_ref):
    sc = _silu(c_ref[...]).astype(BF16)
    o_ref[0] = _dot(sc, w_ref[0].astype(BF16)) + b_ref[0]


def _modulation(cc, mod_w, mod_b):
    depth, d, n = mod_w.shape
    return pl.pallas_call(
        _mod_kernel,
        grid=(depth, n // MOD_TN),
        in_specs=[
            pl.BlockSpec((8, d), lambda l, j: (0, 0)),
            pl.BlockSpec((1, d, MOD_TN), lambda l, j: (l, 0, j)),
            pl.BlockSpec((1, 1, MOD_TN), lambda l, j: (l, 0, j)),
        ],
        out_specs=pl.BlockSpec((1, 8, MOD_TN), lambda l, j: (l, 0, j)),
        out_shape=jax.ShapeDtypeStruct((depth, 8, n), F32),
        compiler_params=pltpu.CompilerParams(
            dimension_semantics=("parallel", "parallel"), vmem_limit_bytes=VMEM_LIMIT),
        name="modulation",
    )(cc, mod_w, mod_b.reshape(depth, 1, n))


def _ffn_kernel(*refs, sub, final, split_at):
    if split_at is None:
        x_ref, mod_ref, g_ref, wg_ref, wu_ref, wd_ref, fg_ref, o_ref, h_ref = refs
        read_x = lambda: x_ref[...]
    else:
        x_ref, xc_ref, mod_ref, g_ref, wg_ref, wu_ref, wd_ref, fg_ref, o_ref, h_ref = refs
        is_ctx = pl.program_id(0) >= split_at
        read_x = lambda: jnp.where(is_ctx, xc_ref[...], x_ref[...])
    j = pl.program_id(1)

    @pl.when(j == 0)
    def _():
        h_ref[...] = _norm_mod(read_x(), g_ref[...], mod_ref, sub).astype(BF16)

    h = h_ref[...]
    a = _silu(_dot(h, wg_ref[...])) * _dot(h, wu_ref[...])
    a = a.astype(BF16)
    for c in range(0, o_ref.shape[1], TF):
        d = _dot(a, wd_ref[:, c:c + TF])
        o_ref[:, c:c + TF] = jnp.where(j == 0, d, o_ref[:, c:c + TF] + d)

    @pl.when(j == pl.num_programs(1) - 1)
    def _():
        y = read_x() + mod_ref[0, 3 * sub + 2:3 * sub + 3, :] * (0.5 * o_ref[...])
        if final:
            y = _rms(y, fg_ref[...])
        o_ref[...] = y


def _ffn(xs, mod, g, w_gu, w_down, final_g, *, layer, sub, tm, n_tiles, mod_row, final, xc=None):
    t, d = xs.shape
    f = w_down.shape[1]
    nf = f // TF
    rows_out = t if xc is None else t + xc.shape[0]
    rows_out = min(rows_out, n_tiles * tm)
    if xc is None:
        split_at = None
        x_specs = [pl.BlockSpec((tm, d), lambda i, j: (i, 0))]
        x_args = (xs,)
    else:
        split_at = t // tm
        x_specs = [pl.BlockSpec((tm, d), lambda i, j: (jnp.minimum(i, split_at - 1), 0)),
                   pl.BlockSpec((tm, d), lambda i, j: (jnp.maximum(i - split_at, 0), 0))]
        x_args = (xs, xc)
    kern = functools.partial(_ffn_kernel, sub=sub, final=final, split_at=split_at)
    return pl.pallas_call(
        kern,
        grid=(n_tiles, nf),
        in_specs=x_specs + [
            pl.BlockSpec((1, N_MOD, d), lambda i, j: (mod_row(i), 0, 0)),
            pl.BlockSpec((1, d), lambda i, j: (0, 0)),
            pl.BlockSpec((None, d, TF), lambda i, j: (layer, 0, j)),
            pl.BlockSpec((None, d, TF), lambda i, j: (layer, 0, j + nf)),
            pl.BlockSpec((None, TF, d), lambda i, j: (layer, j, 0)),
            pl.BlockSpec((1, d), lambda i, j: (0, 0)),
        ],
        out_specs=pl.BlockSpec((tm, d), lambda i, j: (i, 0), pipeline_mode=pl.Buffered(1)),
        out_shape=jax.ShapeDtypeStruct((rows_out, d), F32),
        scratch_shapes=[pltpu.VMEM((tm, d), BF16)],
        compiler_params=pltpu.CompilerParams(
            dimension_semantics=("parallel", "arbitrary"), vmem_limit_bytes=VMEM_LIMIT),
        name="ffn_final" if final else "ffn",
    )(*x_args, mod, g.reshape(1, d), w_gu, w_gu, w_down, final_g.reshape(1, d))


def _rope(x, cos, sin_signed, lane_low):
    partner = jnp.where(lane_low, pltpu.roll(x, LANES - 32, 1), pltpu.roll(x, 32, 1))
    return x * cos + partner * sin_signed


def _log_sigmoid(x):
    return jnp.minimum(x, 0.0) - jnp.log(1.0 + jnp.exp(-jnp.abs(x)))


def _inproj_kernel(x_ref, mod_ref, g_ref, w_ref, qkg_ref, cos_ref, sin_ref, gw_ref, gbias_ref,
                   ws_ref, bs_ref, gmg_ref,
                   q_ref, k_ref, vt_ref, gq_ref, gk_ref, gv_ref, gr_ref, gf_ref, gb_ref, ym_ref):
    _inproj_rows(slice(0, x_ref.shape[0]), x_ref, mod_ref, g_ref, w_ref, qkg_ref, cos_ref, sin_ref,
                 gw_ref, gbias_ref, ws_ref, bs_ref, gmg_ref,
                 q_ref, k_ref, vt_ref, gq_ref, gk_ref, gv_ref, gr_ref, gf_ref, gb_ref, ym_ref)


def _inproj_rows(rs, x_ref, mod_ref, g_ref, w_ref, qkg_ref, cos_ref, sin_ref, gw_ref, gbias_ref,
                 ws_ref, bs_ref, gmg_ref,
                 q_ref, k_ref, vt_ref, gq_ref, gk_ref, gv_ref, gr_ref, gf_ref, gb_ref, ym_ref):
    h = _norm_mod(x_ref[rs, :], g_ref[...], mod_ref, 1).astype(BF16)
    cos = cos_ref[rs, :]
    sin = sin_ref[rs, :]
    lane = lax.broadcasted_iota(jnp.int32, cos.shape, 1)
    lane_low = (lane & 63) < 32

    def proj(off, width):
        return _dot(h, w_ref[:, off:off + width])

    scale = HEAD_DIM ** -0.5 * LOG2E
    zq = proj(OFF_AQ, ATTN_Q_W)
    for hh in range(ATTN_HEADS):
        sl = slice(hh * HEAD_DIM, (hh + 1) * HEAD_DIM)
        qh = _rope(_rms(zq[:, sl], qkg_ref[0:1, :]), cos, sin, lane_low)
        q_ref[rs, sl] = (qh * scale).astype(BF16)
    zk = proj(OFF_AK, ATTN_KV_W)
    for hh in range(ATTN_KV_HEADS):
        sl = slice(hh * HEAD_DIM, (hh + 1) * HEAD_DIM)
        k_ref[rs, sl] = _rope(_rms(zk[:, sl], qkg_ref[1:2, :]), cos, sin, lane_low).astype(BF16)
    vt_ref[0, :, rs] = proj(OFF_AV, ATTN_KV_W).T.astype(BF16)

    gq_ref[rs, :] = proj(OFF_GQ, GLA_K_W) * (GLA_DK ** -0.5)
    gk_ref[rs, :] = proj(OFF_GK, GLA_K_W)
    gv_ref[rs, :] = proj(OFF_GV, GLA_V_W).astype(BF16)
    gr_ref[rs, :] = proj(OFF_GR, GLA_V_W)
    lr = proj(OFF_LR, GLR_PAD).astype(BF16)
    logits = _dot(lr, gw_ref[...]) + gbias_ref[...]
    ld = _log_sigmoid(logits) * (1.0 / GLA_TAU)
    gf_ref[rs, :] = ld[:, :GLA_K_W]
    gb_ref[rs, :] = ld[:, GLA_K_W:]

    mu = proj(OFF_MU, GMLP_W)
    vn = _rms(proj(OFF_MV, GMLP_W), gmg_ref[...]).astype(BF16)
    for c in range((rs.stop - rs.start) // CHUNK):
        rows = slice(c * CHUNK, (c + 1) * CHUNK)
        out_rows = slice(rs.start + c * CHUNK, rs.start + (c + 1) * CHUNK)
        for gi in range(GMLP_GROUPS):
            cols = slice(gi * GMLP_GROUP_DIM, (gi + 1) * GMLP_GROUP_DIM)
            z = _dot(ws_ref[gi], vn[rows, cols]) + bs_ref[gi]
            ym_ref[out_rows, cols] = (mu[rows, cols] * z).astype(BF16)


def _inproj(xs, mod, g, w_in_r, qk_g, cos_t, sin_t, gate_w_r, gate_b_r, ws, bs_b, gm_g, *, layer, mod_row, rope_row):
    t, d = xs.shape
    n_tiles = t // TM
    row = lambda i: (i, 0)
    const2 = lambda i: (0, 0)
    const3 = lambda i: (0, 0, 0)
    widths = [(ATTN_Q_W, BF16), (ATTN_KV_W, BF16), None, (GLA_K_W, F32), (GLA_K_W, F32),
              (GLA_V_W, BF16), (GLA_V_W, F32), (GLA_K_W, F32), (GLA_K_W, F32), (GMLP_W, BF16)]
    out_specs = [pl.BlockSpec((1, ATTN_KV_W, TM), lambda i: (i, 0, 0)) if w is None else pl.BlockSpec((TM, w[0]), row)
                 for w in widths]
    out_shape = [jax.ShapeDtypeStruct((n_tiles, ATTN_KV_W, TM), BF16) if w is None
                 else jax.ShapeDtypeStruct((t, w[0]), w[1]) for w in widths]
    return pl.pallas_call(
        _inproj_kernel,
        grid=(n_tiles,),
        in_specs=[
            pl.BlockSpec((TM, d), row),
            pl.BlockSpec((1, N_MOD, d), lambda i: (mod_row(i), 0, 0)),
            pl.BlockSpec((1, d), const2),
            pl.BlockSpec((None, d, IN_W_R), lambda i: (layer, 0, 0), pipeline_mode=pl.Buffered(1)),
            pl.BlockSpec((2, HEAD_DIM), const2),
            pl.BlockSpec((TM, HEAD_DIM), lambda i: (rope_row(i), 0)),
            pl.BlockSpec((TM, HEAD_DIM), lambda i: (rope_row(i), 0)),
            pl.BlockSpec((GLR_PAD, 2 * GLA_K_W), const2),
            pl.BlockSpec((1, 2 * GLA_K_W), const2),
            pl.BlockSpec((GMLP_GROUPS, CHUNK, CHUNK), const3),
            pl.BlockSpec((GMLP_GROUPS, CHUNK, GMLP_GROUP_DIM), const3),
            pl.BlockSpec((1, GMLP_W), const2),
        ],
        out_specs=out_specs,
        out_shape=out_shape,
        compiler_params=pltpu.CompilerParams(
            dimension_semantics=("parallel",), vmem_limit_bytes=VMEM_LIMIT),
        name="inproj",
    )(xs, mod, g.reshape(1, d), w_in_r, qk_g, cos_t, sin_t, gate_w_r, gate_b_r, ws, bs_b, gm_g.reshape(1, GMLP_W))


def _attn_kernel(*refs, n_lat_tiles):
    if n_lat_tiles:
        q_ref, kc_ref, vtc_ref, kl_ref, vtl_ref, o_ref, s_ref, off_ref, jump_ref, m_ref, acc_ref = refs
    else:
        q_ref, kc_ref, vtc_ref, o_ref, m_ref, acc_ref = refs
    tq = q_ref.shape[0]
    q = jnp.concatenate([q_ref[:, g * HEAD_DIM:(g + 1) * HEAD_DIM] for g in range(ATTN_GROUP)], axis=0)

    def scores(k):
        return _dot_t(k, q)

    def with_ones(vt):
        return jnp.concatenate([vt, jnp.ones((SUM_ROWS, vt.shape[1]), BF16)], axis=0)

    def lat_keys(t):
        start = pl.multiple_of(t * TK, TK)
        return kl_ref[pl.ds(start, TK), :]

    def ctx_tile():
        s = scores(kc_ref[...])
        m_new = jnp.max(s, axis=0, keepdims=True)
        m_ref[...] = m_new
        acc_ref[...] = _dot(with_ones(vtc_ref[0]), jnp.exp2(s - m_new).astype(BF16))

    def lagged_tile(t, carry):
        c = m_ref[...]
        s = scores(lat_keys(t))
        tmax = jnp.max(s, axis=0, keepdims=True)
        p = jnp.exp2(s - c)
        alpha = jnp.exp2(off_ref[...] - c)
        acc_ref[...] = alpha * acc_ref[...] + _dot(with_ones(vtl_ref[t]), p.astype(BF16))
        off_ref[...] = c
        jump_ref[...] = jnp.maximum(jump_ref[...], tmax - c)
        m_ref[...] = jnp.maximum(c, tmax)
        return carry

    def exact_tile(t, carry):
        s_ref[...] = scores(lat_keys(t))
        m_old = m_ref[...]
        m_new = jnp.maximum(m_old, jnp.max(s_ref[...], axis=0, keepdims=True))
        alpha = jnp.exp2(m_old - m_new)
        p = jnp.exp2(s_ref[...] - m_new)
        acc_ref[...] = alpha * acc_ref[...] + _dot(with_ones(vtl_ref[t]), p.astype(BF16))
        m_ref[...] = m_new
        return carry

    ctx_tile()
    if n_lat_tiles:
        off_ref[...] = m_ref[...]
        jump_ref[...] = jnp.zeros_like(jump_ref)
        lax.fori_loop(0, n_lat_tiles, lagged_tile, 0, unroll=ATTN_UNROLL if n_lat_tiles % ATTN_UNROLL == 0 else 1)

        @pl.when(jnp.max(jump_ref[...]) > MAX_SAFE_JUMP)
        def _():
            ctx_tile()
            lax.fori_loop(0, n_lat_tiles, exact_tile, 0)
    out_t = acc_ref[0:HEAD_DIM, :] / acc_ref[HEAD_DIM:HEAD_DIM + 1, :]
    for g in range(ATTN_GROUP):
        o_ref[:, g * HEAD_DIM:(g + 1) * HEAD_DIM] = out_t[:, g * tq:(g + 1) * tq].T.astype(BF16)


def _attention(q, k, vt, *, batch, n_lat, n_ctx, latent):
    gw = ATTN_GROUP * HEAD_DIM
    ctx_blk0 = (batch * n_lat) // n_ctx
    lat_tiles = (batch * n_lat) // TK
    per_tile = TK // n_ctx
    kc_spec = pl.BlockSpec((n_ctx, HEAD_DIM), lambda b, kh, i: (ctx_blk0 + b, kh))
    vtc_spec = pl.BlockSpec((1, HEAD_DIM, n_ctx), lambda b, kh, i: (lat_tiles + b // per_tile, kh, b % per_tile))
    if latent:
        tq = TQ
        nq = n_lat // tq
        n_lat_tiles = n_lat // TK
        q_spec = pl.BlockSpec((tq, gw), lambda b, kh, i: (b * nq + i, kh))
        kl_spec = pl.BlockSpec((n_lat, HEAD_DIM), lambda b, kh, i: (b, kh))
        vtl_spec = pl.BlockSpec((n_lat_tiles, HEAD_DIM, TK), lambda b, kh, i: (b, kh, 0))
        in_specs = [q_spec, kc_spec, vtc_spec, kl_spec, vtl_spec]
        args = (q, k, vt, k, vt)
        out_rows = batch * n_lat
    else:
        tq = n_ctx
        nq = 1
        n_lat_tiles = 0
        q_spec = pl.BlockSpec((tq, gw), lambda b, kh, i: (ctx_blk0 + b, kh))
        in_specs = [q_spec, kc_spec, vtc_spec]
        args = (q, k, vt)
        out_rows = batch * n_ctx
    cols = ATTN_GROUP * tq
    scratch = [pltpu.VMEM((1, cols), F32), pltpu.VMEM((HEAD_DIM + SUM_ROWS, cols), F32)]
    if latent:
        scratch = [pltpu.VMEM((TK, cols), F32), pltpu.VMEM((1, cols), F32), pltpu.VMEM((1, cols), F32)] + scratch
    return pl.pallas_call(
        functools.partial(_attn_kernel, n_lat_tiles=n_lat_tiles),
        grid=(batch, ATTN_KV_HEADS, nq),
        in_specs=in_specs,
        out_specs=pl.BlockSpec((tq, gw), lambda b, kh, i: (b * nq + i, kh)),
        out_shape=jax.ShapeDtypeStruct((out_rows, ATTN_Q_W), BF16),
        scratch_shapes=scratch,
        compiler_params=pltpu.CompilerParams(
            dimension_semantics=("parallel", "parallel", "arbitrary"), vmem_limit_bytes=VMEM_LIMIT),
        name="attn_lat" if latent else "attn_ctx",
    )(*args)


def _gla_consts():
    idx = np.arange(CHUNK)
    tri = (idx[None, :] <= idx[:, None]).astype(np.float32)
    mats_f, mats_b = [tri], [tri.T]
    for lvl in range(1, N_LEVELS + 1):
        s = (2 * CHUNK) >> lvl
        base = (idx // s) * s
        mats_f.append(tri[base + s // 2 - 1])
        mats_b.append(tri.T[base + s // 2])
    cm = np.stack([np.concatenate(mats_f, 0), np.concatenate(mats_b, 0)])
    cm = np.concatenate([cm, cm], axis=-1)
    x = idx[:, None] ^ idx[None, :]
    hb = np.floor(np.log2(np.maximum(x, 1))).astype(np.int32)
    lv = np.where(x == 0, 0, N_LEVELS - hb)
    lv_f = np.where(idx[:, None] >= idx[None, :], lv, -1)
    lv_b = np.where(idx[:, None] <= idx[None, :], lv, -1)
    return cm, np.stack([lv_f, lv_b]).astype(np.int32)


def _gla_chunk(cm_ref, lv_ref, q_ref, k_ref, v_ref, g_ref, o_ref, st_ref, d, rows):
    g = g_ref[rows, :]
    g_hi = g.astype(BF16)
    g_lo = (g - g_hi.astype(F32)).astype(BF16)
    cums = _dot(cm_ref[d], jnp.concatenate([g_hi, g_lo], axis=0))
    cum = cums[0:CHUNK]
    q = q_ref[rows, :]
    k = k_ref[rows, :]
    lv = lv_ref[d]
    last = CHUNK - 1 if d == 0 else 0
    tail = cum[last:last + 1, :]
    lane = lax.broadcasted_iota(jnp.int32, (CHUNK, LANES), 1)
    low = lane < GLA_DK

    qs = [q.astype(BF16)]
    ks = [k.astype(BF16)]
    for lvl in range(1, N_LEVELS + 1):
        mid = cums[lvl * CHUNK:(lvl + 1) * CHUNK]
        qs.append((q * jnp.exp(jnp.minimum(cum - mid, 0.0))).astype(BF16))
        ks.append((k * jnp.exp(jnp.minimum(mid - cum, 0.0))).astype(BF16))
    q_in = (q * jnp.exp(cum)).astype(BF16)
    k_out = (k * jnp.exp(tail - cum)).astype(BF16)
    zero = jnp.zeros((CHUNK, LANES), BF16)
    lv2 = jnp.concatenate([lv, lv], axis=1)

    def per_head_rows(x):
        return jnp.concatenate([jnp.where(low, x, zero), jnp.where(low, zero, x)], axis=0)

    for p in range(GLA_HEADS // 2):
        pl_sl = slice(p * LANES, (p + 1) * LANES)
        pv_sl = slice(2 * p * GLA_DV, 2 * (p + 1) * GLA_DV)
        st = st_ref[d, p]
        a = jnp.zeros((CHUNK, 2 * CHUNK), F32)
        for lvl in range(N_LEVELS + 1):
            a = jnp.where(lv2 == lvl, _dot_t(qs[lvl][:, pl_sl], per_head_rows(ks[lvl][:, pl_sl])), a)
        v2 = v_ref[rows, pv_sl]
        zv = jnp.zeros((CHUNK, GLA_DV), BF16)
        v_bd = jnp.concatenate([jnp.concatenate([v2[:, :GLA_DV], zv], axis=1),
                                jnp.concatenate([zv, v2[:, GLA_DV:]], axis=1)], axis=0)
        inter = _dot_t(per_head_rows(q_in[:, pl_sl]), st.astype(BF16))
        o_ref[rows, pv_sl] = (_dot(a.astype(BF16), v_bd)
                              + jnp.concatenate([inter[:CHUNK], inter[CHUNK:]], axis=1))
        upd = _tdot(v2, k_out[:, pl_sl])
        st_ref[d, p] = st * jnp.exp(tail[:, pl_sl]) + jnp.where(low, upd[:GLA_DV], upd[GLA_DV:])


def _gla_kernel(cm_ref, lv_ref, qf, kf, vf, gf, qb, kb, vb, gb, of_ref, ob_ref, st_ref):
    @pl.when(pl.program_id(1) == 0)
    def _():
        st_ref[...] = jnp.zeros_like(st_ref)

    n = qf.shape[0] // CHUNK
    for c in range(n):
        _gla_chunk(cm_ref, lv_ref, qf, kf, vf, gf, of_ref, st_ref, 0, slice(c * CHUNK, (c + 1) * CHUNK))
        cb = n - 1 - c
        _gla_chunk(cm_ref, lv_ref, qb, kb, vb, gb, ob_ref, st_ref, 1, slice(cb * CHUNK, (cb + 1) * CHUNK))


def _gla(gq, gk, gv, gf, gb, *, batch, n_lat, n_ctx):
    t = gq.shape[0]
    rows = GLA_STEP_CHUNKS * CHUNK
    cl, cc = n_lat // rows, n_ctx // rows
    ctx0 = batch * cl
    cm_np, lv_np = _gla_consts()
    cm = jnp.asarray(cm_np, BF16)
    lv = jnp.asarray(lv_np)

    def fwd(b, s):
        return (jnp.where(s < cc, ctx0 + b * cc + s, b * cl + s - cc), 0)

    def bwd(b, s):
        return (jnp.where(s < cc, ctx0 + b * cc + (cc - 1 - s), b * cl + (cl - 1 - (s - cc))), 0)

    def specs(m):
        return [pl.BlockSpec((rows, GLA_K_W), m), pl.BlockSpec((rows, GLA_K_W), m),
                pl.BlockSpec((rows, GLA_V_W), m), pl.BlockSpec((rows, GLA_K_W), m)]

    return pl.pallas_call(
        _gla_kernel,
        grid=(batch, cc + cl),
        in_specs=[pl.BlockSpec(cm.shape, lambda b, s: (0, 0, 0)), pl.BlockSpec(lv.shape, lambda b, s: (0, 0, 0))]
        + specs(fwd) + specs(bwd),
        out_specs=[pl.BlockSpec((rows, GLA_V_W), fwd), pl.BlockSpec((rows, GLA_V_W), bwd)],
        out_shape=[jax.ShapeDtypeStruct((t, GLA_V_W), F32)] * 2,
        scratch_shapes=[pltpu.VMEM((2, GLA_HEADS // 2, GLA_DV, LANES), F32)],
        compiler_params=pltpu.CompilerParams(
            dimension_semantics=("parallel", "arbitrary"), vmem_limit_bytes=VMEM_LIMIT),
        name="gla",
    )(cm, lv, gq, gk, gv, gf, gq, gk, gv, gb)


def _outproj_kernel(*refs, split_at):
    if split_at is None:
        x_ref, mod_ref, att_ref, of_ref, ob_ref, gr_ref, ym_ref, gg_ref, w_ref, o_ref = refs
        att = att_ref[...]
    else:
        x_ref, mod_ref, att_ref, attc_ref, of_ref, ob_ref, gr_ref, ym_ref, gg_ref, w_ref, o_ref = refs
        att = jnp.where(pl.program_id(0) >= split_at, attc_ref[...], att_ref[...])
    o = of_ref[...] + ob_ref[...]
    r = gr_ref[...]
    y = _dot(att, w_ref[0:ATTN_Q_W, :])
    for hh in range(GLA_HEADS):
        sl = slice(hh * GLA_DV, (hh + 1) * GLA_DV)
        gh = (_rms(o[:, sl], gg_ref[:, sl]) * _silu(r[:, sl])).astype(BF16)
        y += _dot(gh, w_ref[ATTN_Q_W + hh * GLA_DV:ATTN_Q_W + (hh + 1) * GLA_DV, :])
    y += _dot(ym_ref[...], w_ref[ATTN_Q_W + GLA_V_W:, :])
    o_ref[...] = x_ref[...] + mod_ref[0, 5:6, :] * y


def _outproj(xs, mod, att, att_c, o_f, o_b, gr, ym, gla_g, w_out, *, layer, n_tiles, mod_row):
    t, d = xs.shape
    row = lambda i: (i, 0)
    const2 = lambda i: (0, 0)
    if att_c is None:
        split_at = None
        att_specs = [pl.BlockSpec((TM, ATTN_Q_W), row)]
        att_args = (att,)
    else:
        split_at = att.shape[0] // TM
        att_specs = [pl.BlockSpec((TM, ATTN_Q_W), lambda i: (jnp.minimum(i, split_at - 1), 0)),
                     pl.BlockSpec((TM, ATTN_Q_W), lambda i: (jnp.maximum(i - split_at, 0), 0))]
        att_args = (att, att_c)
    return pl.pallas_call(
        functools.partial(_outproj_kernel, split_at=split_at),
        grid=(n_tiles,),
        in_specs=[
            pl.BlockSpec((TM, d), row),
            pl.BlockSpec((1, N_MOD, d), lambda i: (mod_row(i), 0, 0))] + att_specs + [
            pl.BlockSpec((TM, GLA_V_W), row),
            pl.BlockSpec((TM, GLA_V_W), row),
            pl.BlockSpec((TM, GLA_V_W), row),
            pl.BlockSpec((TM, GMLP_W), row),
            pl.BlockSpec((1, GLA_V_W), const2),
            pl.BlockSpec((None,) + w_out.shape[1:], lambda i: (layer, 0, 0), pipeline_mode=pl.Buffered(1)),
        ],
        out_specs=pl.BlockSpec((TM, d), row),
        out_shape=jax.ShapeDtypeStruct((n_tiles * TM, d), F32),
        compiler_params=pltpu.CompilerParams(
            dimension_semantics=("parallel",), vmem_limit_bytes=VMEM_LIMIT),
        name="outproj",
    )(xs, mod, *att_args, o_f, o_b, gr, ym, gla_g.reshape(1, GLA_V_W), w_out)


def _rope_tables(n_lat):
    rows = n_lat // GRID_W
    row = jnp.repeat(jnp.arange(rows, dtype=F32), GRID_W)
    col = jnp.broadcast_to(jnp.arange(GRID_W, dtype=F32), (rows, GRID_W)).reshape(-1)
    nf = HEAD_DIM // 4
    inv = ROPE_THETA ** (-jnp.arange(nf, dtype=F32) / nf)
    ar, ac = row[:, None] * inv, col[:, None] * inv
    cos = jnp.concatenate([jnp.cos(ar), jnp.cos(ar), jnp.cos(ac), jnp.cos(ac)], axis=-1)
    sin = jnp.concatenate([-jnp.sin(ar), jnp.sin(ar), -jnp.sin(ac), jnp.sin(ac)], axis=-1)
    cos = jnp.concatenate([cos, jnp.ones((TM, HEAD_DIM), F32)], axis=0)
    sin = jnp.concatenate([sin, jnp.zeros((TM, HEAD_DIM), F32)], axis=0)
    return cos, sin


def kernel(x, c, ctx, c_ctx, mod_w, mod_b, norm_g, ffn1_w_gu, ffn1_w_down, ffn2_w_gu, ffn2_w_down, w_in, w_out,
           qk_norm_g, gla_gate_w, gla_gate_b, gla_norm_g, gmlp_w_s, gmlp_b_s, gmlp_norm_g, final_norm_g):
    batch, n_lat, d = x.shape
    n_ctx = ctx.shape[1]
    depth = mod_w.shape[0]
    assert n_lat % TM == 0 and (batch * n_ctx) % TM == 0 and n_lat % TK == 0 and n_lat % TQ == 0
    assert n_ctx % (GLA_STEP_CHUNKS * CHUNK) == 0 and n_lat % n_ctx == 0 and batch + 1 <= 8 and TM == TK and TK % n_ctx == 0
    lat_tiles = batch * n_lat // TM
    all_tiles = lat_tiles + batch * n_ctx // TM
    tiles_per_batch = n_lat // TM

    def mod_row(i):
        return jnp.minimum(i // tiles_per_batch, batch)

    tm_ffn = TM_FFN if n_lat % TM_FFN == 0 else TM
    ffn_lat_tiles = batch * n_lat // tm_ffn
    ffn_all_tiles = ffn_lat_tiles + pl.cdiv(batch * n_ctx, tm_ffn)

    def mod_row_ffn(i):
        return jnp.minimum(i // (n_lat // tm_ffn), batch)

    def rope_row(i):
        return jnp.where(i < lat_tiles, i % tiles_per_batch, tiles_per_batch)

    cc = jnp.zeros((8, d), F32).at[:batch].set(c).at[batch].set(c_ctx)
    mod_all = _modulation(cc, mod_w, mod_b).reshape(depth, 8, N_MOD, d)
    cos_t, sin_t = _rope_tables(n_lat)
    offs = np.cumsum([0, ATTN_Q_W, ATTN_KV_W, ATTN_KV_W, GLA_K_W, GLA_K_W, GLA_V_W, GLA_V_W, 2 * GLA_GATE_RANK,
                      GMLP_W, GMLP_W])
    lr0, lr1 = int(offs[7]), int(offs[8])
    w_in_r = jnp.concatenate(
        [w_in[:, :, :lr0], w_in[:, :, lr1:], w_in[:, :, lr0:lr1],
         jnp.zeros((depth, d, GLR_PAD - 2 * GLA_GATE_RANK), w_in.dtype)], axis=-1).astype(BF16)
    w_out_b = w_out.astype(BF16)
    f1gu, f1d = ffn1_w_gu.astype(BF16), ffn1_w_down.astype(BF16)
    f2gu, f2d = ffn2_w_gu.astype(BF16), ffn2_w_down.astype(BF16)
    xs = x.reshape(batch * n_lat, d)
    xc = ctx.reshape(batch * n_ctx, d)

    for l in range(depth):
        last = l == depth - 1
        mod = mod_all[l]
        gate_w_r = jnp.zeros((GLR_PAD, 2 * GLA_K_W), F32)
        gate_w_r = gate_w_r.at[:GLA_GATE_RANK, :GLA_K_W].set(gla_gate_w[l, 0])
        gate_w_r = gate_w_r.at[GLA_GATE_RANK:2 * GLA_GATE_RANK, GLA_K_W:].set(gla_gate_w[l, 1]).astype(BF16)
        gate_b_r = gla_gate_b[l].reshape(1, 2 * GLA_K_W)
        bs_b = jnp.broadcast_to(gmlp_b_s[l][..., None], gmlp_b_s.shape[1:] + (GMLP_GROUP_DIM,))

        if xc is not None:
            xs = _ffn(xs, mod, norm_g[l, 0], f1gu, f1d, final_norm_g, layer=l, sub=0, tm=TM,
                      n_tiles=all_tiles, mod_row=mod_row, final=False, xc=xc)
        else:
            xs = _ffn(xs, mod, norm_g[l, 0], f1gu, f1d, final_norm_g, layer=l, sub=0, tm=tm_ffn,
                      n_tiles=ffn_all_tiles, mod_row=mod_row_ffn, final=False)
        xc = None
        q, k, vt, gq, gk, gv, gr, gf, gb, ym = _inproj(
            xs, mod, norm_g[l, 1], w_in_r, qk_norm_g[l], cos_t, sin_t, gate_w_r, gate_b_r,
            gmlp_w_s[l].astype(BF16), bs_b, gmlp_norm_g[l], layer=l, mod_row=mod_row, rope_row=rope_row)
        att = _attention(q, k, vt, batch=batch, n_lat=n_lat, n_ctx=n_ctx, latent=True)
        o_f, o_b = _gla(gq, gk, gv, gf, gb, batch=batch, n_lat=n_lat, n_ctx=n_ctx)
        att_c = None if last else _attention(q, k, vt, batch=batch, n_lat=n_lat, n_ctx=n_ctx, latent=False)
        n_tiles = lat_tiles if last else all_tiles
        xs = _outproj(xs, mod, att, att_c, o_f, o_b, gr, ym, gla_norm_g[l], w_out_b,
                      layer=l, n_tiles=n_tiles, mod_row=mod_row)
        xs = _ffn(xs, mod, norm_g[l, 2], f2gu, f2d, final_norm_g, layer=l, sub=2, tm=tm_ffn,
                  n_tiles=ffn_lat_tiles if last else ffn_all_tiles, mod_row=mod_row_ffn, final=last)
    return xs.reshape(batch, n_lat, d)
```

```python
import functools

import numpy as np
import jax
import jax.numpy as jnp
from jax import lax
from jax.experimental import pallas as pl
from jax.experimental.pallas import tpu as pltpu

F32 = jnp.float32
BF16 = jnp.bfloat16

EPS = 1e-6
N_MOD = 9
HEAD_DIM = 128
ATTN_HEADS = 8
ATTN_KV_HEADS = 2
ATTN_GROUP = ATTN_HEADS // ATTN_KV_HEADS
ROPE_THETA = 10000.0
GRID_W = 64
GLA_HEADS = 4
GLA_DK = 64
GLA_DV = 128
GLA_GATE_RANK = 16
GLA_TAU = 16.0
LOG2E = 1.4426950408889634
CHUNK = 128
GMLP_GROUPS = 4
GMLP_GROUP_DIM = 128

ATTN_Q_W = ATTN_HEADS * HEAD_DIM
ATTN_KV_W = ATTN_KV_HEADS * HEAD_DIM
GLA_K_W = GLA_HEADS * GLA_DK
GLA_V_W = GLA_HEADS * GLA_DV
GMLP_W = GMLP_GROUPS * GMLP_GROUP_DIM
LANES = 128
GLR_PAD = LANES
N_LEVELS = 7

OFF_AQ = 0
OFF_AK = OFF_AQ + ATTN_Q_W
OFF_AV = OFF_AK + ATTN_KV_W
OFF_GQ = OFF_AV + ATTN_KV_W
OFF_GK = OFF_GQ + GLA_K_W
OFF_GV = OFF_GK + GLA_K_W
OFF_GR = OFF_GV + GLA_V_W
OFF_MU = OFF_GR + GLA_V_W
OFF_MV = OFF_MU + GMLP_W
OFF_LR = OFF_MV + GMLP_W
IN_W_R = OFF_LR + GLR_PAD

TM = 512
TF = 512
TM_FFN = 512
FFN_SUB = 512
TQ = 256
TK = 512
TKA = 512
GLA_STEP_CHUNKS = 2
SUM_ROWS = 16
ATTN_UNROLL = 32
MAX_SAFE_JUMP = 100.0
MOD_TN = 2048
VMEM_LIMIT = 56 * 1024 * 1024


def _sigmoid(x):
    return 1.0 / (1.0 + jnp.exp(-x))


def _silu(x):
    return x * _sigmoid(x)


def _rms(x, g):
    ms = jnp.mean(x * x, axis=-1, keepdims=True)
    return x * lax.rsqrt(ms + EPS) * g


def _norm_mod(x, g, mod_ref, i):
    ms = jnp.mean(x * x, axis=-1, keepdims=True)
    gain = g * (1.0 + mod_ref[0, 3 * i + 1:3 * i + 2, :])
    return x * lax.rsqrt(ms + EPS) * gain + mod_ref[0, 3 * i:3 * i + 1, :]


def _dot(a, b):
    return jnp.dot(a, b, preferred_element_type=F32)


def _dot_t(a, b):
    return lax.dot_general(a, b, (((1,), (1,)), ((), ())), preferred_element_type=F32)


def _tdot(a, b):
    return lax.dot_general(a, b, (((0,), (0,)), ((), ())), preferred_element_type=F32)


def _mod_kernel(c_ref, w_ref, b_ref, o_ref):
    sc = _silu(c_ref[...]).astype(BF16)
    o_ref[0] = _dot(sc, w_ref[0].astype(BF16)) + b_ref[0]


def _modulation(cc, mod_w, mod_b):
    depth, d, n = mod_w.shape
    return pl.pallas_call(
        _mod_kernel,
        grid=(depth, n // MOD_TN),
        in_specs=[
            pl.BlockSpec((8, d), lambda l, j: (0, 0)),
            pl.BlockSpec((1, d, MOD_TN), lambda l, j: (l, 0, j)),
            pl.BlockSpec((1, 1, MOD_TN), lambda l, j: (l, 0, j)),
        ],
        out_specs=pl.BlockSpec((1, 8, MOD_TN), lambda l, j: (l, 0, j)),
        out_shape=jax.ShapeDtypeStruct((depth, 8, n), F32),
        compiler_params=pltpu.CompilerParams(
            dimension_semantics=("parallel", "parallel"), vmem_limit_bytes=VMEM_LIMIT),
        name="modulation",
    )(cc, mod_w, mod_b.reshape(depth, 1, n))


def _ffn_kernel(*refs, sub, final, split_at):
    if split_at is None:
        x_ref, mod_ref, g_ref, wg_ref, wu_ref, wd_ref, fg_ref, o_ref, h_ref = refs
        sources = [(None, x_ref)]
    else:
        x_ref, xc_ref, mod_ref, g_ref, wg_ref, wu_ref, wd_ref, fg_ref, o_ref, h_ref = refs
        is_ctx = pl.program_id(0) >= split_at
        sources = [(jnp.logical_not(is_ctx), x_ref), (is_ctx, xc_ref)]
    j = pl.program_id(1)
    last_j = pl.num_programs(1) - 1
    subs = [slice(r, r + FFN_SUB) for r in range(0, o_ref.shape[0], FFN_SUB)]

    def prologue(src_ref):
        for rs in subs:
            h_ref[rs, :] = _norm_mod(src_ref[rs, :], g_ref[...], mod_ref, sub).astype(BF16)

    def epilogue(src_ref):
        for rs in subs:
            y = src_ref[rs, :] + mod_ref[0, 3 * sub + 2:3 * sub + 3, :] * (0.5 * o_ref[rs, :])
            if final:
                y = _rms(y, fg_ref[...])
            o_ref[rs, :] = y

    for cond, src_ref in sources:
        first = j == 0
        pl.when(first if cond is None else jnp.logical_and(first, cond))(functools.partial(prologue, src_ref))

    for rs in subs:
        h = h_ref[rs, :]
        a = (_silu(_dot(h, wg_ref[...])) * _dot(h, wu_ref[...])).astype(BF16)
        for c in range(0, o_ref.shape[1], TF):
            d = _dot(a, wd_ref[:, c:c + TF])
            o_ref[rs, c:c + TF] = jnp.where(j == 0, d, o_ref[rs, c:c + TF] + d)

    for cond, src_ref in sources:
        last = j == last_j
        pl.when(last if cond is None else jnp.logical_and(last, cond))(functools.partial(epilogue, src_ref))


def _ffn(xs, mod, g, w_gu, w_down, final_g, *, layer, sub, tm, n_tiles, mod_row, final, xc=None):
    t, d = xs.shape
    f = w_down.shape[1]
    nf = f // TF
    rows_out = t if xc is None else t + xc.shape[0]
    rows_out = min(rows_out, n_tiles * tm)
    if xc is None:
        split_at = None
        x_specs = [pl.BlockSpec((tm, d), lambda i, j: (i, 0))]
        x_args = (xs,)
    else:
        split_at = t // tm
        x_specs = [pl.BlockSpec((tm, d), lambda i, j: (jnp.minimum(i, split_at - 1), 0)),
                   pl.BlockSpec((tm, d), lambda i, j: (jnp.maximum(i - split_at, 0), 0))]
        x_args = (xs, xc)
    kern = functools.partial(_ffn_kernel, sub=sub, final=final, split_at=split_at)
    return pl.pallas_call(
        kern,
        grid=(n_tiles, nf),
        in_specs=x_specs + [
            pl.BlockSpec((1, N_MOD, d), lambda i, j: (mod_row(i), 0, 0)),
            pl.BlockSpec((1, d), lambda i, j: (0, 0)),
            pl.BlockSpec((None, d, TF), lambda i, j: (layer, 0, j)),
            pl.BlockSpec((None, d, TF), lambda i, j: (layer, 0, j + nf)),
            pl.BlockSpec((None, TF, d), lambda i, j: (layer, j, 0)),
            pl.BlockSpec((1, d), lambda i, j: (0, 0)),
        ],
        out_specs=pl.BlockSpec((tm, d), lambda i, j: (i, 0)),
        out_shape=jax.ShapeDtypeStruct((rows_out, d), F32),
        scratch_shapes=[pltpu.VMEM((tm, d), BF16)],
        compiler_params=pltpu.CompilerParams(
            dimension_semantics=("parallel", "arbitrary"), vmem_limit_bytes=VMEM_LIMIT),
        name="ffn_final" if final else "ffn",
    )(*x_args, mod, g.reshape(1, d), w_gu, w_gu, w_down, final_g.reshape(1, d))


def _rope(x, cos, sin_signed, lane_low):
    partner = jnp.where(lane_low, pltpu.roll(x, LANES - 32, 1), pltpu.roll(x, 32, 1))
    return x * cos + partner * sin_signed


def _log_sigmoid(x):
    return jnp.minimum(x, 0.0) - jnp.log(1.0 + jnp.exp(-jnp.abs(x)))


def _inproj_kernel(x_ref, mod_ref, g_ref, w_ref, wt_ref, qkg_ref, cos_ref, sin_ref, gw_ref, gbias_ref,
                   ws_ref, bs_ref, gmg_ref,
                   q_ref, k_ref, vt_ref, gq_ref, gk_ref, gv_ref, gr_ref, gf_ref, gb_ref, ym_ref):
    _inproj_rows(slice(0, x_ref.shape[0]), x_ref, mod_ref, g_ref, w_ref, wt_ref, qkg_ref, cos_ref, sin_ref,
                 gw_ref, gbias_ref, ws_ref, bs_ref, gmg_ref,
                 q_ref, k_ref, vt_ref, gq_ref, gk_ref, gv_ref, gr_ref, gf_ref, gb_ref, ym_ref)


def _inproj_rows(rs, x_ref, mod_ref, g_ref, w_ref, wt_ref, qkg_ref, cos_ref, sin_ref, gw_ref, gbias_ref,
                 ws_ref, bs_ref, gmg_ref,
                 q_ref, k_ref, vt_ref, gq_ref, gk_ref, gv_ref, gr_ref, gf_ref, gb_ref, ym_ref):
    h = _norm_mod(x_ref[rs, :], g_ref[...], mod_ref, 1).astype(BF16)
    cos = cos_ref[rs, :]
    sin = sin_ref[rs, :]
    lane = lax.broadcasted_iota(jnp.int32, cos.shape, 1)
    lane_low = (lane & 63) < 32

    def proj(off, width):
        if off < OFF_MU:
            return _dot(h, w_ref[:, off:off + width])
        return _dot(h, wt_ref[:, off - OFF_MU:off - OFF_MU + width])

    scale = HEAD_DIM ** -0.5 * LOG2E
    zq = proj(OFF_AQ, ATTN_Q_W)
    for hh in range(ATTN_HEADS):
        sl = slice(hh * HEAD_DIM, (hh + 1) * HEAD_DIM)
        qh = _rope(_rms(zq[:, sl], qkg_ref[0:1, :]), cos, sin, lane_low)
        q_ref[rs, sl] = (qh * scale).astype(BF16)
    zk = proj(OFF_AK, ATTN_KV_W)
    for hh in range(ATTN_KV_HEADS):
        sl = slice(hh * HEAD_DIM, (hh + 1) * HEAD_DIM)
        k_ref[rs, sl] = _rope(_rms(zk[:, sl], qkg_ref[1:2, :]), cos, sin, lane_low).astype(BF16)
    vt_ref[0, :, rs] = proj(OFF_AV, ATTN_KV_W).T.astype(BF16)

    gq_ref[rs, :] = proj(OFF_GQ, GLA_K_W) * (GLA_DK ** -0.5)
    gk_ref[rs, :] = proj(OFF_GK, GLA_K_W)
    gv_ref[rs, :] = proj(OFF_GV, GLA_V_W).astype(BF16)
    gr_ref[rs, :] = proj(OFF_GR, GLA_V_W)
    lr = proj(OFF_LR, GLR_PAD).astype(BF16)
    logits = _dot(lr, gw_ref[...]) + gbias_ref[...]
    ld = _log_sigmoid(logits) * (1.0 / GLA_TAU)
    gf_ref[rs, :] = ld[:, :GLA_K_W]
    gb_ref[rs, :] = ld[:, GLA_K_W:]

    mu = proj(OFF_MU, GMLP_W)
    vn = _rms(proj(OFF_MV, GMLP_W), gmg_ref[...]).astype(BF16)
    for c in range((rs.stop - rs.start) // CHUNK):
        rows = slice(c * CHUNK, (c + 1) * CHUNK)
        out_rows = slice(rs.start + c * CHUNK, rs.start + (c + 1) * CHUNK)
        for gi in range(GMLP_GROUPS):
            cols = slice(gi * GMLP_GROUP_DIM, (gi + 1) * GMLP_GROUP_DIM)
            z = _dot(ws_ref[gi], vn[rows, cols]) + bs_ref[gi]
            ym_ref[out_rows, cols] = (mu[rows, cols] * z).astype(BF16)


def _inproj(xs, mod, g, w_head, w_tail, qk_g, cos_t, sin_t, gate_w_r, gate_b_r, ws, bs_b, gm_g, *, layer, mod_row,
            rope_row):
    t, d = xs.shape
    n_tiles = t // TM
    row = lambda i: (i, 0)
    const2 = lambda i: (0, 0)
    const3 = lambda i: (0, 0, 0)
    widths = [(ATTN_Q_W, BF16), (ATTN_KV_W, BF16), None, (GLA_K_W, F32), (GLA_K_W, F32),
              (GLA_V_W, BF16), (GLA_V_W, F32), (GLA_K_W, F32), (GLA_K_W, F32), (GMLP_W, BF16)]
    out_specs = [pl.BlockSpec((1, ATTN_KV_W, TM), lambda i: (i, 0, 0)) if w is None else pl.BlockSpec((TM, w[0]), row)
                 for w in widths]
    out_shape = [jax.ShapeDtypeStruct((n_tiles, ATTN_KV_W, TM), BF16) if w is None
                 else jax.ShapeDtypeStruct((t, w[0]), w[1]) for w in widths]
    return pl.pallas_call(
        _inproj_kernel,
        grid=(n_tiles,),
        in_specs=[
            pl.BlockSpec((TM, d), row),
            pl.BlockSpec((1, N_MOD, d), lambda i: (mod_row(i), 0, 0)),
            pl.BlockSpec((1, d), const2),
            pl.BlockSpec((None, d, OFF_MU), lambda i: (layer, 0, 0), pipeline_mode=pl.Buffered(1)),
            pl.BlockSpec((None, d, IN_W_R - OFF_MU), lambda i: (layer, 0, 0), pipeline_mode=pl.Buffered(1)),
            pl.BlockSpec((2, HEAD_DIM), const2),
            pl.BlockSpec((TM, HEAD_DIM), lambda i: (rope_row(i), 0)),
            pl.BlockSpec((TM, HEAD_DIM), lambda i: (rope_row(i), 0)),
            pl.BlockSpec((GLR_PAD, 2 * GLA_K_W), const2),
            pl.BlockSpec((1, 2 * GLA_K_W), const2),
            pl.BlockSpec((GMLP_GROUPS, CHUNK, CHUNK), const3),
            pl.BlockSpec((GMLP_GROUPS, CHUNK, GMLP_GROUP_DIM), const3),
            pl.BlockSpec((1, GMLP_W), const2),
        ],
        out_specs=out_specs,
        out_shape=out_shape,
        compiler_params=pltpu.CompilerParams(
            dimension_semantics=("parallel",), vmem_limit_bytes=VMEM_LIMIT),
        name="inproj",
    )(xs, mod, g.reshape(1, d), w_head, w_tail, qk_g, cos_t, sin_t, gate_w_r, gate_b_r, ws, bs_b,
      gm_g.reshape(1, GMLP_W))


def _attn_kernel(*refs, n_lat_tiles):
    if n_lat_tiles:
        q_ref, kc_ref, vtc_ref, kl_ref, vtl_ref, o_ref, s_ref, off_ref, jump_ref, m_ref, acc_ref = refs
    else:
        q_ref, kc_ref, vtc_ref, o_ref, m_ref, acc_ref = refs
    tq = q_ref.shape[0]
    q = jnp.concatenate([q_ref[:, g * HEAD_DIM:(g + 1) * HEAD_DIM] for g in range(ATTN_GROUP)], axis=0)

    def scores(k):
        return _dot_t(k, q)

    def with_ones(vt):
        return jnp.concatenate([vt, jnp.ones((SUM_ROWS, vt.shape[1]), BF16)], axis=0)

    def lat_keys(t):
        start = pl.multiple_of(t * TKA, TKA)
        return kl_ref[pl.ds(start, TKA), :]

    def lat_values(t):
        n = TKA // TK
        return jnp.concatenate([vtl_ref[n * t + u] for u in range(n)], axis=1) if n > 1 else vtl_ref[t]

    def ctx_tile():
        s = scores(kc_ref[...])
        m_new = jnp.max(s, axis=0, keepdims=True)
        m_ref[...] = m_new
        acc_ref[...] = _dot(with_ones(vtc_ref[0]), jnp.exp2(s - m_new).astype(BF16))

    def lagged_tile(t, carry):
        c = m_ref[...]
        s = scores(lat_keys(t))
        tmax = jnp.max(s, axis=0, keepdims=True)
        p = jnp.exp2(s - c)
        alpha = jnp.exp2(off_ref[...] - c)
        acc_ref[...] = alpha * acc_ref[...] + _dot(with_ones(lat_values(t)), p.astype(BF16))
        off_ref[...] = c
        jump_ref[...] = jnp.maximum(jump_ref[...], tmax - c)
        m_ref[...] = jnp.maximum(c, tmax)
        return carry

    def exact_tile(t, carry):
        s_ref[...] = scores(lat_keys(t))
        m_old = m_ref[...]
        m_new = jnp.maximum(m_old, jnp.max(s_ref[...], axis=0, keepdims=True))
        alpha = jnp.exp2(m_old - m_new)
        p = jnp.exp2(s_ref[...] - m_new)
        acc_ref[...] = alpha * acc_ref[...] + _dot(with_ones(lat_values(t)), p.astype(BF16))
        m_ref[...] = m_new
        return carry

    ctx_tile()
    if n_lat_tiles:
        off_ref[...] = m_ref[...]
        jump_ref[...] = jnp.zeros_like(jump_ref)
        lax.fori_loop(0, n_lat_tiles, lagged_tile, 0, unroll=ATTN_UNROLL if n_lat_tiles % ATTN_UNROLL == 0 else 1)

        @pl.when(jnp.max(jump_ref[...]) > MAX_SAFE_JUMP)
        def _():
            ctx_tile()
            lax.fori_loop(0, n_lat_tiles, exact_tile, 0)
    out_t = acc_ref[0:HEAD_DIM, :] / acc_ref[HEAD_DIM:HEAD_DIM + 1, :]
    for g in range(ATTN_GROUP):
        o_ref[:, g * HEAD_DIM:(g + 1) * HEAD_DIM] = out_t[:, g * tq:(g + 1) * tq].T.astype(BF16)


def _attention(q, k, vt, *, batch, n_lat, n_ctx, latent):
    gw = ATTN_GROUP * HEAD_DIM
    ctx_blk0 = (batch * n_lat) // n_ctx
    lat_tiles = (batch * n_lat) // TK
    per_tile = TK // n_ctx
    kc_spec = pl.BlockSpec((n_ctx, HEAD_DIM), lambda b, kh, i: (ctx_blk0 + b, kh))
    vtc_spec = pl.BlockSpec((1, HEAD_DIM, n_ctx), lambda b, kh, i: (lat_tiles + b // per_tile, kh, b % per_tile))
    if latent:
        tq = TQ
        nq = n_lat // tq
        n_lat_tiles = n_lat // TKA
        q_spec = pl.BlockSpec((tq, gw), lambda b, kh, i: (b * nq + i, kh))
        kl_spec = pl.BlockSpec((n_lat, HEAD_DIM), lambda b, kh, i: (b, kh))
        vtl_spec = pl.BlockSpec((n_lat // TK, HEAD_DIM, TK), lambda b, kh, i: (b, kh, 0))
        in_specs = [q_spec, kc_spec, vtc_spec, kl_spec, vtl_spec]
        args = (q, k, vt, k, vt)
        out_rows = batch * n_lat
    else:
        tq = n_ctx
        nq = 1
        n_lat_tiles = 0
        q_spec = pl.BlockSpec((tq, gw), lambda b, kh, i: (ctx_blk0 + b, kh))
        in_specs = [q_spec, kc_spec, vtc_spec]
        args = (q, k, vt)
        out_rows = batch * n_ctx
    cols = ATTN_GROUP * tq
    scratch = [pltpu.VMEM((1, cols), F32), pltpu.VMEM((HEAD_DIM + SUM_ROWS, cols), F32)]
    if latent:
        scratch = [pltpu.VMEM((TKA, cols), F32), pltpu.VMEM((1, cols), F32), pltpu.VMEM((1, cols), F32)] + scratch
    return pl.pallas_call(
        functools.partial(_attn_kernel, n_lat_tiles=n_lat_tiles),
        grid=(batch, ATTN_KV_HEADS, nq),
        in_specs=in_specs,
        out_specs=pl.BlockSpec((tq, gw), lambda b, kh, i: (b * nq + i, kh)),
        out_shape=jax.ShapeDtypeStruct((out_rows, ATTN_Q_W), BF16),
        scratch_shapes=scratch,
        compiler_params=pltpu.CompilerParams(
            dimension_semantics=("parallel", "parallel", "arbitrary"), vmem_limit_bytes=VMEM_LIMIT),
        name="attn_lat" if latent else "attn_ctx",
    )(*args)


def _gla_consts():
    idx = np.arange(CHUNK)
    tri = (idx[None, :] <= idx[:, None]).astype(np.float32)
    mats_f, mats_b = [tri], [tri.T]
    for lvl in range(1, N_LEVELS + 1):
        s = (2 * CHUNK) >> lvl
        base = (idx // s) * s
        mats_f.append(tri[base + s // 2 - 1])
        mats_b.append(tri.T[base + s // 2])
    cm = np.stack([np.concatenate(mats_f, 0), np.concatenate(mats_b, 0)])
    cm = np.concatenate([cm, cm], axis=-1)
    x = idx[:, None] ^ idx[None, :]
    hb = np.floor(np.log2(np.maximum(x, 1))).astype(np.int32)
    lv = np.where(x == 0, 0, N_LEVELS - hb)
    lv_f = np.where(idx[:, None] >= idx[None, :], lv, -1)
    lv_b = np.where(idx[:, None] <= idx[None, :], lv, -1)
    return cm, np.stack([lv_f, lv_b]).astype(np.int32)


def _gla_chunk(cm_ref, lv_ref, q_ref, k_ref, v_ref, g_ref, o_ref, st_ref, d, rows):
    g = g_ref[rows, :]
    g_hi = g.astype(BF16)
    g_lo = (g - g_hi.astype(F32)).astype(BF16)
    cums = _dot(cm_ref[d], jnp.concatenate([g_hi, g_lo], axis=0))
    cum = cums[0:CHUNK]
    q = q_ref[rows, :]
    k = k_ref[rows, :]
    lv = lv_ref[d]
    last = CHUNK - 1 if d == 0 else 0
    tail = cum[last:last + 1, :]
    lane = lax.broadcasted_iota(jnp.int32, (CHUNK, LANES), 1)
    low = lane < GLA_DK

    qs = [q.astype(BF16)]
    ks = [k.astype(BF16)]
    for lvl in range(1, N_LEVELS + 1):
        mid = cums[lvl * CHUNK:(lvl + 1) * CHUNK]
        qs.append((q * jnp.exp(jnp.minimum(cum - mid, 0.0))).astype(BF16))
        ks.append((k * jnp.exp(jnp.minimum(mid - cum, 0.0))).astype(BF16))
    q_in = (q * jnp.exp(cum)).astype(BF16)
    k_out = (k * jnp.exp(tail - cum)).astype(BF16)
    zero = jnp.zeros((CHUNK, LANES), BF16)
    lv2 = jnp.concatenate([lv, lv], axis=1)

    def per_head_rows(x):
        return jnp.concatenate([jnp.where(low, x, zero), jnp.where(low, zero, x)], axis=0)

    for p in range(GLA_HEADS // 2):
        pl_sl = slice(p * LANES, (p + 1) * LANES)
        pv_sl = slice(2 * p * GLA_DV, 2 * (p + 1) * GLA_DV)
        st = st_ref[d, p]
        a = jnp.zeros((CHUNK, 2 * CHUNK), F32)
        for lvl in range(N_LEVELS + 1):
            a = jnp.where(lv2 == lvl, _dot_t(qs[lvl][:, pl_sl], per_head_rows(ks[lvl][:, pl_sl])), a)
        v2 = v_ref[rows, pv_sl]
        zv = jnp.zeros((CHUNK, GLA_DV), BF16)
        v_bd = jnp.concatenate([jnp.concatenate([v2[:, :GLA_DV], zv], axis=1),
                                jnp.concatenate([zv, v2[:, GLA_DV:]], axis=1)], axis=0)
        inter = _dot_t(per_head_rows(q_in[:, pl_sl]), st.astype(BF16))
        o_ref[rows, pv_sl] = (_dot(a.astype(BF16), v_bd)
                              + jnp.concatenate([inter[:CHUNK], inter[CHUNK:]], axis=1))
        upd = _tdot(v2, k_out[:, pl_sl])
        st_ref[d, p] = st * jnp.exp(tail[:, pl_sl]) + jnp.where(low, upd[:GLA_DV], upd[GLA_DV:])


def _gla_kernel(cm_ref, lv_ref, qf, kf, vf, gf, qb, kb, vb, gb, of_ref, ob_ref, st_ref):
    @pl.when(pl.program_id(1) == 0)
    def _():
        st_ref[...] = jnp.zeros_like(st_ref)

    n = qf.shape[0] // CHUNK
    for c in range(n):
        _gla_chunk(cm_ref, lv_ref, qf, kf, vf, gf, of_ref, st_ref, 0, slice(c * CHUNK, (c + 1) * CHUNK))
        cb = n - 1 - c
        _gla_chunk(cm_ref, lv_ref, qb, kb, vb, gb, ob_ref, st_ref, 1, slice(cb * CHUNK, (cb + 1) * CHUNK))


def _gla(gq, gk, gv, gf, gb, *, batch, n_lat, n_ctx):
    t = gq.shape[0]
    rows = GLA_STEP_CHUNKS * CHUNK
    cl, cc = n_lat // rows, n_ctx // rows
    ctx0 = batch * cl
    cm_np, lv_np = _gla_consts()
    cm = jnp.asarray(cm_np, BF16)
    lv = jnp.asarray(lv_np)

    def fwd(b, s):
        return (jnp.where(s < cc, ctx0 + b * cc + s, b * cl + s - cc), 0)

    def bwd(b, s):
        return (jnp.where(s < cc, ctx0 + b * cc + (cc - 1 - s), b * cl + (cl - 1 - (s - cc))), 0)

    def specs(m):
        return [pl.BlockSpec((rows, GLA_K_W), m), pl.BlockSpec((rows, GLA_K_W), m),
                pl.BlockSpec((rows, GLA_V_W), m), pl.BlockSpec((rows, GLA_K_W), m)]

    return pl.pallas_call(
        _gla_kernel,
        grid=(batch, cc + cl),
        in_specs=[pl.BlockSpec(cm.shape, lambda b, s: (0, 0, 0)), pl.BlockSpec(lv.shape, lambda b, s: (0, 0, 0))]
        + specs(fwd) + specs(bwd),
        out_specs=[pl.BlockSpec((rows, GLA_V_W), fwd), pl.BlockSpec((rows, GLA_V_W), bwd)],
        out_shape=[jax.ShapeDtypeStruct((t, GLA_V_W), F32)] * 2,
        scratch_shapes=[pltpu.VMEM((2, GLA_HEADS // 2, GLA_DV, LANES), F32)],
        compiler_params=pltpu.CompilerParams(
            dimension_semantics=("parallel", "arbitrary"), vmem_limit_bytes=VMEM_LIMIT),
        name="gla",
    )(cm, lv, gq, gk, gv, gf, gq, gk, gv, gb)


def _outproj_kernel(*refs, split_at):
    if split_at is None:
        x_ref, mod_ref, att_ref, of_ref, ob_ref, gr_ref, ym_ref, gg_ref, w_ref, o_ref = refs
        att = att_ref[...]
    else:
        x_ref, mod_ref, att_ref, attc_ref, of_ref, ob_ref, gr_ref, ym_ref, gg_ref, w_ref, o_ref = refs
        att = jnp.where(pl.program_id(0) >= split_at, attc_ref[...], att_ref[...])
    o = of_ref[...] + ob_ref[...]
    r = gr_ref[...]
    y = _dot(att, w_ref[0:ATTN_Q_W, :])
    for hh in range(GLA_HEADS):
        sl = slice(hh * GLA_DV, (hh + 1) * GLA_DV)
        gh = (_rms(o[:, sl], gg_ref[:, sl]) * _silu(r[:, sl])).astype(BF16)
        y += _dot(gh, w_ref[ATTN_Q_W + hh * GLA_DV:ATTN_Q_W + (hh + 1) * GLA_DV, :])
    y += _dot(ym_ref[...], w_ref[ATTN_Q_W + GLA_V_W:, :])
    o_ref[...] = x_ref[...] + mod_ref[0, 5:6, :] * y


def _outproj(xs, mod, att, att_c, o_f, o_b, gr, ym, gla_g, w_out, *, layer, n_tiles, mod_row):
    t, d = xs.shape
    row = lambda i: (i, 0)
    const2 = lambda i: (0, 0)
    if att_c is None:
        split_at = None
        att_specs = [pl.BlockSpec((TM, ATTN_Q_W), row)]
        att_args = (att,)
    else:
        split_at = att.shape[0] // TM
        att_specs = [pl.BlockSpec((TM, ATTN_Q_W), lambda i: (jnp.minimum(i, split_at - 1), 0)),
                     pl.BlockSpec((TM, ATTN_Q_W), lambda i: (jnp.maximum(i - split_at, 0), 0))]
        att_args = (att, att_c)
    return pl.pallas_call(
        functools.partial(_outproj_kernel, split_at=split_at),
        grid=(n_tiles,),
        in_specs=[
            pl.BlockSpec((TM, d), row),
            pl.BlockSpec((1, N_MOD, d), lambda i: (mod_row(i), 0, 0))] + att_specs + [
            pl.BlockSpec((TM, GLA_V_W), row),
            pl.BlockSpec((TM, GLA_V_W), row),
            pl.BlockSpec((TM, GLA_V_W), row),
            pl.BlockSpec((TM, GMLP_W), row),
            pl.BlockSpec((1, GLA_V_W), const2),
            pl.BlockSpec((None,) + w_out.shape[1:], lambda i: (layer, 0, 0), pipeline_mode=pl.Buffered(1)),
        ],
        out_specs=pl.BlockSpec((TM, d), row),
        out_shape=jax.ShapeDtypeStruct((n_tiles * TM, d), F32),
        compiler_params=pltpu.CompilerParams(
            dimension_semantics=("parallel",), vmem_limit_bytes=VMEM_LIMIT),
        name="outproj",
    )(xs, mod, *att_args, o_f, o_b, gr, ym, gla_g.reshape(1, GLA_V_W), w_out)


def _rope_tables(n_lat):
    rows = n_lat // GRID_W
    row = jnp.repeat(jnp.arange(rows, dtype=F32), GRID_W)
    col = jnp.broadcast_to(jnp.arange(GRID_W, dtype=F32), (rows, GRID_W)).reshape(-1)
    nf = HEAD_DIM // 4
    inv = ROPE_THETA ** (-jnp.arange(nf, dtype=F32) / nf)
    ar, ac = row[:, None] * inv, col[:, None] * inv
    cos = jnp.concatenate([jnp.cos(ar), jnp.cos(ar), jnp.cos(ac), jnp.cos(ac)], axis=-1)
    sin = jnp.concatenate([-jnp.sin(ar), jnp.sin(ar), -jnp.sin(ac), jnp.sin(ac)], axis=-1)
    cos = jnp.concatenate([cos, jnp.ones((TM, HEAD_DIM), F32)], axis=0)
    sin = jnp.concatenate([sin, jnp.zeros((TM, HEAD_DIM), F32)], axis=0)
    return cos, sin


def kernel(x, c, ctx, c_ctx, mod_w, mod_b, norm_g, ffn1_w_gu, ffn1_w_down, ffn2_w_gu, ffn2_w_down, w_in, w_out,
           qk_norm_g, gla_gate_w, gla_gate_b, gla_norm_g, gmlp_w_s, gmlp_b_s, gmlp_norm_g, final_norm_g):
    batch, n_lat, d = x.shape
    n_ctx = ctx.shape[1]
    depth = mod_w.shape[0]
    assert n_lat % TM == 0 and (batch * n_ctx) % TM == 0 and n_lat % TKA == 0 and n_lat % TQ == 0
    assert n_ctx % (GLA_STEP_CHUNKS * CHUNK) == 0 and n_lat % n_ctx == 0 and batch + 1 <= 8 and TM == TK and TK % n_ctx == 0
    lat_tiles = batch * n_lat // TM
    all_tiles = lat_tiles + batch * n_ctx // TM
    tiles_per_batch = n_lat // TM

    def mod_row(i):
        return jnp.minimum(i // tiles_per_batch, batch)

    tm_ffn = TM_FFN if n_lat % TM_FFN == 0 else TM
    ffn_lat_tiles = batch * n_lat // tm_ffn
    ffn_all_tiles = ffn_lat_tiles + pl.cdiv(batch * n_ctx, tm_ffn)

    def mod_row_ffn(i):
        return jnp.minimum(i // (n_lat // tm_ffn), batch)

    def rope_row(i):
        return jnp.where(i < lat_tiles, i % tiles_per_batch, tiles_per_batch)

    cc = jnp.zeros((8, d), F32).at[:batch].set(c).at[batch].set(c_ctx)
    mod_all = _modulation(cc, mod_w, mod_b).reshape(depth, 8, N_MOD, d)
    cos_t, sin_t = _rope_tables(n_lat)
    offs = np.cumsum([0, ATTN_Q_W, ATTN_KV_W, ATTN_KV_W, GLA_K_W, GLA_K_W, GLA_V_W, GLA_V_W, 2 * GLA_GATE_RANK,
                      GMLP_W, GMLP_W])
    lr0, lr1 = int(offs[7]), int(offs[8])
    w_head = w_in[:, :, :lr0].astype(BF16)
    w_tail = jnp.concatenate(
        [w_in[:, :, lr1:], w_in[:, :, lr0:lr1],
         jnp.zeros((depth, d, GLR_PAD - 2 * GLA_GATE_RANK), w_in.dtype)], axis=-1).astype(BF16)
    w_out_b = w_out.astype(BF16)
    f1gu, f1d = ffn1_w_gu.astype(BF16), ffn1_w_down.astype(BF16)
    f2gu, f2d = ffn2_w_gu.astype(BF16), ffn2_w_down.astype(BF16)
    xs = x.reshape(batch * n_lat, d)
    xc = ctx.reshape(batch * n_ctx, d)

    for l in range(depth):
        last = l == depth - 1
        mod = mod_all[l]
        gate_w_r = jnp.zeros((GLR_PAD, 2 * GLA_K_W), F32)
        gate_w_r = gate_w_r.at[:GLA_GATE_RANK, :GLA_K_W].set(gla_gate_w[l, 0])
        gate_w_r = gate_w_r.at[GLA_GATE_RANK:2 * GLA_GATE_RANK, GLA_K_W:].set(gla_gate_w[l, 1]).astype(BF16)
        gate_b_r = gla_gate_b[l].reshape(1, 2 * GLA_K_W)
        bs_b = jnp.broadcast_to(gmlp_b_s[l][..., None], gmlp_b_s.shape[1:] + (GMLP_GROUP_DIM,))

        if xc is not None:
            xs = _ffn(xs, mod, norm_g[l, 0], f1gu, f1d, final_norm_g, layer=l, sub=0, tm=TM,
                      n_tiles=all_tiles, mod_row=mod_row, final=False, xc=xc)
        else:
            xs = _ffn(xs, mod, norm_g[l, 0], f1gu, f1d, final_norm_g, layer=l, sub=0, tm=tm_ffn,
                      n_tiles=ffn_all_tiles, mod_row=mod_row_ffn, final=False)
        xc = None
        q, k, vt, gq, gk, gv, gr, gf, gb, ym = _inproj(
            xs, mod, norm_g[l, 1], w_head, w_tail, qk_norm_g[l], cos_t, sin_t, gate_w_r, gate_b_r,
            gmlp_w_s[l].astype(BF16), bs_b, gmlp_norm_g[l], layer=l, mod_row=mod_row, rope_row=rope_row)
        att = _attention(q, k, vt, batch=batch, n_lat=n_lat, n_ctx=n_ctx, latent=True)
        o_f, o_b = _gla(gq, gk, gv, gf, gb, batch=batch, n_lat=n_lat, n_ctx=n_ctx)
        att_c = None if last else _attention(q, k, vt, batch=batch, n_lat=n_lat, n_ctx=n_ctx, latent=False)
        n_tiles = lat_tiles if last else all_tiles
        xs = _outproj(xs, mod, att, att_c, o_f, o_b, gr, ym, gla_norm_g[l], w_out_b,
                      layer=l, n_tiles=n_tiles, mod_row=mod_row)
        xs = _ffn(xs, mod, norm_g[l, 2], f2gu, f2d, final_norm_g, layer=l, sub=2, tm=tm_ffn,
                  n_tiles=ffn_lat_tiles if last else ffn_all_tiles, mod_row=mod_row_ffn, final=last)
    return xs.reshape(batch, n_lat, d)
```

```python
import functools

import numpy as np
import jax
import jax.numpy as jnp
from jax import lax
from jax.experimental import pallas as pl
from jax.experimental.pallas import tpu as pltpu

F32 = jnp.float32
BF16 = jnp.bfloat16

EPS = 1e-6
N_MOD = 9
HEAD_DIM = 128
ATTN_HEADS = 8
ATTN_KV_HEADS = 2
ATTN_GROUP = ATTN_HEADS // ATTN_KV_HEADS
ROPE_THETA = 10000.0
GRID_W = 64
GLA_HEADS = 4
GLA_DK = 64
GLA_DV = 128
GLA_GATE_RANK = 16
GLA_TAU = 16.0
LOG2E = 1.4426950408889634
CHUNK = 128
GMLP_GROUPS = 4
GMLP_GROUP_DIM = 128

ATTN_Q_W = ATTN_HEADS * HEAD_DIM
ATTN_KV_W = ATTN_KV_HEADS * HEAD_DIM
GLA_K_W = GLA_HEADS * GLA_DK
GLA_V_W = GLA_HEADS * GLA_DV
GMLP_W = GMLP_GROUPS * GMLP_GROUP_DIM
LANES = 128
GLR_PAD = LANES
N_LEVELS = 7

OFF_AQ = 0
OFF_AK = OFF_AQ + ATTN_Q_W
OFF_AV = OFF_AK + ATTN_KV_W
OFF_GQ = OFF_AV + ATTN_KV_W
OFF_GK = OFF_GQ + GLA_K_W
OFF_GV = OFF_GK + GLA_K_W
OFF_GR = OFF_GV + GLA_V_W
OFF_MU = OFF_GR + GLA_V_W
OFF_MV = OFF_MU + GMLP_W
OFF_LR = OFF_MV + GMLP_W
IN_W_R = OFF_LR + GLR_PAD

TM = 512
TF = 512
TM_FFN = 512
FFN_SUB = 512
TQ = 256
TK = 512
TKA = 512
GLA_STEP_CHUNKS = 2
SUM_ROWS = 16
ATTN_UNROLL = 32
MAX_SAFE_JUMP = 100.0
MOD_TN = 2048
VMEM_LIMIT = 56 * 1024 * 1024


def _sigmoid(x):
    return 1.0 / (1.0 + jnp.exp(-x))


def _silu(x):
    return x * _sigmoid(x)


def _rms(x, g):
    ms = jnp.mean(x * x, axis=-1, keepdims=True)
    return x * lax.rsqrt(ms + EPS) * g


def _norm_mod(x, g, mod_ref, i):
    ms = jnp.mean(x * x, axis=-1, keepdims=True)
    gain = g * (1.0 + mod_ref[0, 3 * i + 1:3 * i + 2, :])
    return x * lax.rsqrt(ms + EPS) * gain + mod_ref[0, 3 * i:3 * i + 1, :]


def _dot(a, b):
    return jnp.dot(a, b, preferred_element_type=F32)


def _dot_t(a, b):
    return lax.dot_general(a, b, (((1,), (1,)), ((), ())), preferred_element_type=F32)


def _tdot(a, b):
    return lax.dot_general(a, b, (((0,), (0,)), ((), ())), preferred_element_type=F32)


def _mod_kernel(c_ref, w_ref, b_ref, o_ref):
    sc = _silu(c_ref[...]).astype(BF16)
    o_ref[0] = _dot(sc, w_ref[0].astype(BF16)) + b_ref[0]


def _modulation(cc, mod_w, mod_b):
    depth, d, n = mod_w.shape
    return pl.pallas_call(
        _mod_kernel,
        grid=(depth, n // MOD_TN),
        in_specs=[
            pl.BlockSpec((8, d), lambda l, j: (0, 0)),
            pl.BlockSpec((1, d, MOD_TN), lambda l, j: (l, 0, j)),
            pl.BlockSpec((1, 1, MOD_TN), lambda l, j: (l, 0, j)),
        ],
        out_specs=pl.BlockSpec((1, 8, MOD_TN), lambda l, j: (l, 0, j)),
        out_shape=jax.ShapeDtypeStruct((depth, 8, n), F32),
        compiler_params=pltpu.CompilerParams(
            dimension_semantics=("parallel", "parallel"), vmem_limit_bytes=VMEM_LIMIT),
        name="modulation",
    )(cc, mod_w, mod_b.reshape(depth, 1, n))


def _ffn_kernel(*refs, sub, final, split_at):
    if split_at is None:
        x_ref, mod_ref, g_ref, wg_ref, wu_ref, wd_ref, fg_ref, o_ref, h_ref = refs
        sources = [(None, x_ref)]
    else:
        x_ref, xc_ref, mod_ref, g_ref, wg_ref, wu_ref, wd_ref, fg_ref, o_ref, h_ref = refs
        is_ctx = pl.program_id(0) >= split_at
        sources = [(jnp.logical_not(is_ctx), x_ref), (is_ctx, xc_ref)]
    j = pl.program_id(1)
    last_j = pl.num_programs(1) - 1
    subs = [slice(r, r + FFN_SUB) for r in range(0, o_ref.shape[0], FFN_SUB)]

    def prologue(src_ref):
        for rs in subs:
            h_ref[rs, :] = _norm_mod(src_ref[rs, :], g_ref[...], mod_ref, sub).astype(BF16)

    def epilogue(src_ref):
        for rs in subs:
            y = src_ref[rs, :] + mod_ref[0, 3 * sub + 2:3 * sub + 3, :] * (0.5 * o_ref[rs, :])
            if final:
                y = _rms(y, fg_ref[...])
            o_ref[rs, :] = y

    for cond, src_ref in sources:
        first = j == 0
        pl.when(first if cond is None else jnp.logical_and(first, cond))(functools.partial(prologue, src_ref))

    for rs in subs:
        h = h_ref[rs, :]
        a = (_silu(_dot(h, wg_ref[...])) * _dot(h, wu_ref[...])).astype(BF16)
        for c in range(0, o_ref.shape[1], TF):
            d = _dot(a, wd_ref[:, c:c + TF])
            o_ref[rs, c:c + TF] = jnp.where(j == 0, d, o_ref[rs, c:c + TF] + d)

    for cond, src_ref in sources:
        last = j == last_j
        pl.when(last if cond is None else jnp.logical_and(last, cond))(functools.partial(epilogue, src_ref))


def _ffn(xs, mod, g, w_gu, w_down, final_g, *, layer, sub, tm, n_tiles, mod_row, final, xc=None):
    t, d = xs.shape
    f = w_down.shape[1]
    nf = f // TF
    rows_out = t if xc is None else t + xc.shape[0]
    rows_out = min(rows_out, n_tiles * tm)
    if xc is None:
        split_at = None
        x_specs = [pl.BlockSpec((tm, d), lambda i, j: (i, 0))]
        x_args = (xs,)
    else:
        split_at = t // tm
        x_specs = [pl.BlockSpec((tm, d), lambda i, j: (jnp.minimum(i, split_at - 1), 0)),
                   pl.BlockSpec((tm, d), lambda i, j: (jnp.maximum(i - split_at, 0), 0))]
        x_args = (xs, xc)
    kern = functools.partial(_ffn_kernel, sub=sub, final=final, split_at=split_at)
    return pl.pallas_call(
        kern,
        grid=(n_tiles, nf),
        in_specs=x_specs + [
            pl.BlockSpec((1, N_MOD, d), lambda i, j: (mod_row(i), 0, 0)),
            pl.BlockSpec((1, d), lambda i, j: (0, 0)),
            pl.BlockSpec((None, d, TF), lambda i, j: (layer, 0, j)),
            pl.BlockSpec((None, d, TF), lambda i, j: (layer, 0, j + nf)),
            pl.BlockSpec((None, TF, d), lambda i, j: (layer, j, 0)),
            pl.BlockSpec((1, d), lambda i, j: (0, 0)),
        ],
        out_specs=pl.BlockSpec((tm, d), lambda i, j: (i, 0)),
        out_shape=jax.ShapeDtypeStruct((rows_out, d), F32),
        scratch_shapes=[pltpu.VMEM((tm, d), BF16)],
        compiler_params=pltpu.CompilerParams(
            dimension_semantics=("parallel", "arbitrary"), vmem_limit_bytes=VMEM_LIMIT),
        name="ffn_final" if final else "ffn",
    )(*x_args, mod, g.reshape(1, d), w_gu, w_gu, w_down, final_g.reshape(1, d))


def _rope(x, cos, sin_signed, lane_low):
    partner = jnp.where(lane_low, pltpu.roll(x, LANES - 32, 1), pltpu.roll(x, 32, 1))
    return x * cos + partner * sin_signed


def _log_sigmoid(x):
    return jnp.minimum(x, 0.0) - jnp.log(1.0 + jnp.exp(-jnp.abs(x)))


def _inproj_kernel(x_ref, mod_ref, g_ref, w_ref, wt_ref, qkg_ref, cos_ref, sin_ref, gw_ref, gbias_ref,
                   ws_ref, bs_ref, gmg_ref,
                   q_ref, k_ref, vt_ref, gq_ref, gk_ref, gv_ref, gr_ref, gf_ref, gb_ref, ym_ref):
    _inproj_rows(slice(0, x_ref.shape[0]), x_ref, mod_ref, g_ref, w_ref, wt_ref, qkg_ref, cos_ref, sin_ref,
                 gw_ref, gbias_ref, ws_ref, bs_ref, gmg_ref,
                 q_ref, k_ref, vt_ref, gq_ref, gk_ref, gv_ref, gr_ref, gf_ref, gb_ref, ym_ref)


def _inproj_rows(rs, x_ref, mod_ref, g_ref, w_ref, wt_ref, qkg_ref, cos_ref, sin_ref, gw_ref, gbias_ref,
                 ws_ref, bs_ref, gmg_ref,
                 q_ref, k_ref, vt_ref, gq_ref, gk_ref, gv_ref, gr_ref, gf_ref, gb_ref, ym_ref):
    h = _norm_mod(x_ref[rs, :], g_ref[...], mod_ref, 1).astype(BF16)
    cos = cos_ref[rs, :]
    sin = sin_ref[rs, :]
    lane = lax.broadcasted_iota(jnp.int32, cos.shape, 1)
    lane_low = (lane & 63) < 32

    def proj(off, width):
        if off < OFF_MU:
            return _dot(h, w_ref[:, off:off + width])
        return _dot(h, wt_ref[:, off - OFF_MU:off - OFF_MU + width])

    scale = HEAD_DIM ** -0.5 * LOG2E
    zq = proj(OFF_AQ, ATTN_Q_W)
    for hh in range(ATTN_HEADS):
        sl = slice(hh * HEAD_DIM, (hh + 1) * HEAD_DIM)
        qh = _rope(_rms(zq[:, sl], qkg_ref[0:1, :]), cos, sin, lane_low)
        q_ref[rs, sl] = (qh * scale).astype(BF16)
    zk = proj(OFF_AK, ATTN_KV_W)
    for hh in range(ATTN_KV_HEADS):
        sl = slice(hh * HEAD_DIM, (hh + 1) * HEAD_DIM)
        k_ref[rs, sl] = _rope(_rms(zk[:, sl], qkg_ref[1:2, :]), cos, sin, lane_low).astype(BF16)
    vt_ref[0, :, rs] = proj(OFF_AV, ATTN_KV_W).T.astype(BF16)

    gq_ref[rs, :] = proj(OFF_GQ, GLA_K_W) * (GLA_DK ** -0.5)
    gk_ref[rs, :] = proj(OFF_GK, GLA_K_W)
    gv_ref[rs, :] = proj(OFF_GV, GLA_V_W).astype(BF16)
    gr_ref[rs, :] = proj(OFF_GR, GLA_V_W)
    lr = proj(OFF_LR, GLR_PAD).astype(BF16)
    logits = _dot(lr, gw_ref[...]) + gbias_ref[...]
    ld = _log_sigmoid(logits) * (1.0 / GLA_TAU)
    gf_ref[rs, :] = ld[:, :GLA_K_W]
    gb_ref[rs, :] = ld[:, GLA_K_W:]

    mu = proj(OFF_MU, GMLP_W)
    vn = _rms(proj(OFF_MV, GMLP_W), gmg_ref[...]).astype(BF16)
    for c in range((rs.stop - rs.start) // CHUNK):
        rows = slice(c * CHUNK, (c + 1) * CHUNK)
        out_rows = slice(rs.start + c * CHUNK, rs.start + (c + 1) * CHUNK)
        for gi in range(GMLP_GROUPS):
            cols = slice(gi * GMLP_GROUP_DIM, (gi + 1) * GMLP_GROUP_DIM)
            z = _dot(ws_ref[gi], vn[rows, cols]) + bs_ref[gi]
            ym_ref[out_rows, cols] = (mu[rows, cols] * z).astype(BF16)


def _inproj(xs, mod, g, w_head, w_tail, qk_g, cos_t, sin_t, gate_w_r, gate_b_r, ws, bs_b, gm_g, *, layer, mod_row,
            rope_row):
    t, d = xs.shape
    n_tiles = t // TM
    row = lambda i: (i, 0)
    const2 = lambda i: (0, 0)
    const3 = lambda i: (0, 0, 0)
    widths = [(ATTN_Q_W, BF16), (ATTN_KV_W, BF16), None, (GLA_K_W, F32), (GLA_K_W, F32),
              (GLA_V_W, BF16), (GLA_V_W, F32), (GLA_K_W, F32), (GLA_K_W, F32), (GMLP_W, BF16)]
    out_specs = [pl.BlockSpec((1, ATTN_KV_W, TM), lambda i: (i, 0, 0)) if w is None else pl.BlockSpec((TM, w[0]), row)
                 for w in widths]
    out_shape = [jax.ShapeDtypeStruct((n_tiles, ATTN_KV_W, TM), BF16) if w is None
                 else jax.ShapeDtypeStruct((t, w[0]), w[1]) for w in widths]
    return pl.pallas_call(
        _inproj_kernel,
        grid=(n_tiles,),
        in_specs=[
            pl.BlockSpec((TM, d), row),
            pl.BlockSpec((1, N_MOD, d), lambda i: (mod_row(i), 0, 0)),
            pl.BlockSpec((1, d), const2),
            pl.BlockSpec((None, d, OFF_MU), lambda i: (layer, 0, 0), pipeline_mode=pl.Buffered(1)),
            pl.BlockSpec((None, d, IN_W_R - OFF_MU), lambda i: (layer, 0, 0), pipeline_mode=pl.Buffered(1)),
            pl.BlockSpec((2, HEAD_DIM), const2),
            pl.BlockSpec((TM, HEAD_DIM), lambda i: (rope_row(i), 0)),
            pl.BlockSpec((TM, HEAD_DIM), lambda i: (rope_row(i), 0)),
            pl.BlockSpec((GLR_PAD, 2 * GLA_K_W), const2),
            pl.BlockSpec((1, 2 * GLA_K_W), const2),
            pl.BlockSpec((GMLP_GROUPS, CHUNK, CHUNK), const3),
            pl.BlockSpec((GMLP_GROUPS, CHUNK, GMLP_GROUP_DIM), const3),
            pl.BlockSpec((1, GMLP_W), const2),
        ],
        out_specs=out_specs,
        out_shape=out_shape,
        compiler_params=pltpu.CompilerParams(
            dimension_semantics=("parallel",), vmem_limit_bytes=VMEM_LIMIT),
        name="inproj",
    )(xs, mod, g.reshape(1, d), w_head, w_tail, qk_g, cos_t, sin_t, gate_w_r, gate_b_r, ws, bs_b,
      gm_g.reshape(1, GMLP_W))


def _attn_kernel(*refs, n_lat_tiles):
    if n_lat_tiles:
        q_ref, kc_ref, vtc_ref, kl_ref, vtl_ref, o_ref, s_ref, off_ref, jump_ref, m_ref, acc_ref = refs
    else:
        q_ref, kc_ref, vtc_ref, o_ref, m_ref, acc_ref = refs
    tq = q_ref.shape[0]
    q = jnp.concatenate([q_ref[:, g * HEAD_DIM:(g + 1) * HEAD_DIM] for g in range(ATTN_GROUP)], axis=0)

    def scores(k):
        return _dot_t(k, q)

    def with_ones(vt):
        return jnp.concatenate([vt, jnp.ones((SUM_ROWS, vt.shape[1]), BF16)], axis=0)

    def lat_keys(t):
        start = pl.multiple_of(t * TKA, TKA)
        return kl_ref[pl.ds(start, TKA), :]

    def lat_values(t):
        n = TKA // TK
        return jnp.concatenate([vtl_ref[n * t + u] for u in range(n)], axis=1) if n > 1 else vtl_ref[t]

    def ctx_tile():
        s = scores(kc_ref[...])
        m_new = jnp.max(s, axis=0, keepdims=True)
        m_ref[...] = m_new
        acc_ref[...] = _dot(with_ones(vtc_ref[0]), jnp.exp2(s - m_new).astype(BF16))

    def lagged_tile(t, carry):
        c = m_ref[...]
        s = scores(lat_keys(t))
        tmax = jnp.max(s, axis=0, keepdims=True)
        p = jnp.exp2(s - c)
        alpha = jnp.exp2(off_ref[...] - c)
        acc_ref[...] = alpha * acc_ref[...] + _dot(with_ones(lat_values(t)), p.astype(BF16))
        off_ref[...] = c
        jump_ref[...] = jnp.maximum(jump_ref[...], tmax - c)
        m_ref[...] = jnp.maximum(c, tmax)
        return carry

    def exact_tile(t, carry):
        s_ref[...] = scores(lat_keys(t))
        m_old = m_ref[...]
        m_new = jnp.maximum(m_old, jnp.max(s_ref[...], axis=0, keepdims=True))
        alpha = jnp.exp2(m_old - m_new)
        p = jnp.exp2(s_ref[...] - m_new)
        acc_ref[...] = alpha * acc_ref[...] + _dot(with_ones(lat_values(t)), p.astype(BF16))
        m_ref[...] = m_new
        return carry

    ctx_tile()
    if n_lat_tiles:
        off_ref[...] = m_ref[...]
        jump_ref[...] = jnp.zeros_like(jump_ref)
        lax.fori_loop(0, n_lat_tiles, lagged_tile, 0, unroll=ATTN_UNROLL if n_lat_tiles % ATTN_UNROLL == 0 else 1)

        @pl.when(jnp.max(jump_ref[...]) > MAX_SAFE_JUMP)
        def _():
            ctx_tile()
            lax.fori_loop(0, n_lat_tiles, exact_tile, 0)
    out_t = acc_ref[0:HEAD_DIM, :] / acc_ref[HEAD_DIM:HEAD_DIM + 1, :]
    for g in range(ATTN_GROUP):
        o_ref[:, g * HEAD_DIM:(g + 1) * HEAD_DIM] = out_t[:, g * tq:(g + 1) * tq].T.astype(BF16)


def _attention(q, k, vt, *, batch, n_lat, n_ctx, latent):
    gw = ATTN_GROUP * HEAD_DIM
    ctx_blk0 = (batch * n_lat) // n_ctx
    lat_tiles = (batch * n_lat) // TK
    per_tile = TK // n_ctx
    kc_spec = pl.BlockSpec((n_ctx, HEAD_DIM), lambda b, kh, i: (ctx_blk0 + b, kh))
    vtc_spec = pl.BlockSpec((1, HEAD_DIM, n_ctx), lambda b, kh, i: (lat_tiles + b // per_tile, kh, b % per_tile))
    if latent:
        tq = TQ
        nq = n_lat // tq
        n_lat_tiles = n_lat // TKA
        q_spec = pl.BlockSpec((tq, gw), lambda b, kh, i: (b * nq + i, kh))
        kl_spec = pl.BlockSpec((n_lat, HEAD_DIM), lambda b, kh, i: (b, kh))
        vtl_spec = pl.BlockSpec((n_lat // TK, HEAD_DIM, TK), lambda b, kh, i: (b, kh, 0))
        in_specs = [q_spec, kc_spec, vtc_spec, kl_spec, vtl_spec]
        args = (q, k, vt, k, vt)
        out_rows = batch * n_lat
    else:
        tq = n_ctx
        nq = 1
        n_lat_tiles = 0
        q_spec = pl.BlockSpec((tq, gw), lambda b, kh, i: (ctx_blk0 + b, kh))
        in_specs = [q_spec, kc_spec, vtc_spec]
        args = (q, k, vt)
        out_rows = batch * n_ctx
    cols = ATTN_GROUP * tq
    scratch = [pltpu.VMEM((1, cols), F32), pltpu.VMEM((HEAD_DIM + SUM_ROWS, cols), F32)]
    if latent:
        scratch = [pltpu.VMEM((TKA, cols), F32), pltpu.VMEM((1, cols), F32), pltpu.VMEM((1, cols), F32)] + scratch
    return pl.pallas_call(
        functools.partial(_attn_kernel, n_lat_tiles=n_lat_tiles),
        grid=(batch, ATTN_KV_HEADS, nq),
        in_specs=in_specs,
        out_specs=pl.BlockSpec((tq, gw), lambda b, kh, i: (b * nq + i, kh)),
        out_shape=jax.ShapeDtypeStruct((out_rows, ATTN_Q_W), BF16),
        scratch_shapes=scratch,
        compiler_params=pltpu.CompilerParams(
            dimension_semantics=("parallel", "parallel", "arbitrary"), vmem_limit_bytes=VMEM_LIMIT),
        name="attn_lat" if latent else "attn_ctx",
    )(*args)


def _gla_consts():
    idx = np.arange(CHUNK)
    tri = (idx[None, :] <= idx[:, None]).astype(np.float32)
    mats_f, mats_b = [tri], [tri.T]
    for lvl in range(1, N_LEVELS + 1):
        s = (2 * CHUNK) >> lvl
        base = (idx // s) * s
        mats_f.append(tri[base + s // 2 - 1])
        mats_b.append(tri.T[base + s // 2])
    cm = np.stack([np.concatenate(mats_f, 0), np.concatenate(mats_b, 0)])
    cm = np.concatenate([cm, cm], axis=-1)
    x = idx[:, None] ^ idx[None, :]
    hb = np.floor(np.log2(np.maximum(x, 1))).astype(np.int32)
    lv = np.where(x == 0, 0, N_LEVELS - hb)
    lv_f = np.where(idx[:, None] >= idx[None, :], lv, -1)
    lv_b = np.where(idx[:, None] <= idx[None, :], lv, -1)
    return cm, np.stack([lv_f, lv_b]).astype(np.int32)


def _gla_chunk(cm_ref, lv_ref, q_ref, k_ref, v_ref, g_ref, o_ref, st_ref, d, rows):
    g = g_ref[rows, :]
    g_hi = g.astype(BF16)
    g_lo = (g - g_hi.astype(F32)).astype(BF16)
    cums = _dot(cm_ref[d], jnp.concatenate([g_hi, g_lo], axis=0))
    cum = cums[0:CHUNK]
    q = q_ref[rows, :]
    k = k_ref[rows, :]
    lv = lv_ref[d]
    last = CHUNK - 1 if d == 0 else 0
    tail = cum[last:last + 1, :]
    lane = lax.broadcasted_iota(jnp.int32, (CHUNK, LANES), 1)
    low = lane < GLA_DK

    qs = [q.astype(BF16)]
    ks = [k.astype(BF16)]
    for lvl in range(1, N_LEVELS + 1):
        decay = jnp.exp(-jnp.abs(cum - cums[lvl * CHUNK:(lvl + 1) * CHUNK]))
        qs.append((q * decay).astype(BF16))
        ks.append((k * decay).astype(BF16))
    q_in = (q * jnp.exp(cum)).astype(BF16)
    k_out = (k * jnp.exp(tail - cum)).astype(BF16)
    zero = jnp.zeros((CHUNK, LANES), BF16)
    lv2 = jnp.concatenate([lv, lv], axis=1)

    def per_head_rows(x):
        return jnp.concatenate([jnp.where(low, x, zero), jnp.where(low, zero, x)], axis=0)

    for p in range(GLA_HEADS // 2):
        pl_sl = slice(p * LANES, (p + 1) * LANES)
        pv_sl = slice(2 * p * GLA_DV, 2 * (p + 1) * GLA_DV)
        st = st_ref[d, p]
        a = jnp.zeros((CHUNK, 2 * CHUNK), F32)
        for lvl in range(N_LEVELS + 1):
            a = jnp.where(lv2 == lvl, _dot_t(qs[lvl][:, pl_sl], per_head_rows(ks[lvl][:, pl_sl])), a)
        v2 = v_ref[rows, pv_sl]
        zv = jnp.zeros((CHUNK, GLA_DV), BF16)
        v_bd = jnp.concatenate([jnp.concatenate([v2[:, :GLA_DV], zv], axis=1),
                                jnp.concatenate([zv, v2[:, GLA_DV:]], axis=1)], axis=0)
        inter = _dot_t(per_head_rows(q_in[:, pl_sl]), st.astype(BF16))
        o_ref[rows, pv_sl] = (_dot(a.astype(BF16), v_bd)
                              + jnp.concatenate([inter[:CHUNK], inter[CHUNK:]], axis=1))
        upd = _tdot(v2, k_out[:, pl_sl])
        st_ref[d, p] = st * jnp.exp(tail[:, pl_sl]) + jnp.where(low, upd[:GLA_DV], upd[GLA_DV:])


def _gla_kernel(cm_ref, lv_ref, qf, kf, vf, gf, qb, kb, vb, gb, of_ref, ob_ref, st_ref):
    @pl.when(pl.program_id(1) == 0)
    def _():
        st_ref[...] = jnp.zeros_like(st_ref)

    n = qf.shape[0] // CHUNK
    for c in range(n):
        _gla_chunk(cm_ref, lv_ref, qf, kf, vf, gf, of_ref, st_ref, 0, slice(c * CHUNK, (c + 1) * CHUNK))
        cb = n - 1 - c
        _gla_chunk(cm_ref, lv_ref, qb, kb, vb, gb, ob_ref, st_ref, 1, slice(cb * CHUNK, (cb + 1) * CHUNK))


def _gla(gq, gk, gv, gf, gb, *, batch, n_lat, n_ctx):
    t = gq.shape[0]
    rows = GLA_STEP_CHUNKS * CHUNK
    cl, cc = n_lat // rows, n_ctx // rows
    ctx0 = batch * cl
    cm_np, lv_np = _gla_consts()
    cm = jnp.asarray(cm_np, BF16)
    lv = jnp.asarray(lv_np)

    def fwd(b, s):
        return (jnp.where(s < cc, ctx0 + b * cc + s, b * cl + s - cc), 0)

    def bwd(b, s):
        return (jnp.where(s < cc, ctx0 + b * cc + (cc - 1 - s), b * cl + (cl - 1 - (s - cc))), 0)

    def specs(m):
        return [pl.BlockSpec((rows, GLA_K_W), m), pl.BlockSpec((rows, GLA_K_W), m),
                pl.BlockSpec((rows, GLA_V_W), m), pl.BlockSpec((rows, GLA_K_W), m)]

    return pl.pallas_call(
        _gla_kernel,
        grid=(batch, cc + cl),
        in_specs=[pl.BlockSpec(cm.shape, lambda b, s: (0, 0, 0)), pl.BlockSpec(lv.shape, lambda b, s: (0, 0, 0))]
        + specs(fwd) + specs(bwd),
        out_specs=[pl.BlockSpec((rows, GLA_V_W), fwd), pl.BlockSpec((rows, GLA_V_W), bwd)],
        out_shape=[jax.ShapeDtypeStruct((t, GLA_V_W), F32)] * 2,
        scratch_shapes=[pltpu.VMEM((2, GLA_HEADS // 2, GLA_DV, LANES), F32)],
        compiler_params=pltpu.CompilerParams(
            dimension_semantics=("parallel", "arbitrary"), vmem_limit_bytes=VMEM_LIMIT),
        name="gla",
    )(cm, lv, gq, gk, gv, gf, gq, gk, gv, gb)


def _outproj_kernel(*refs, split_at):
    if split_at is None:
        x_ref, mod_ref, att_ref, of_ref, ob_ref, gr_ref, ym_ref, gg_ref, w_ref, o_ref = refs
        att = att_ref[...]
    else:
        x_ref, mod_ref, att_ref, attc_ref, of_ref, ob_ref, gr_ref, ym_ref, gg_ref, w_ref, o_ref = refs
        att = jnp.where(pl.program_id(0) >= split_at, attc_ref[...], att_ref[...])
    o = of_ref[...] + ob_ref[...]
    r = gr_ref[...]
    mix = [att]
    for hh in range(GLA_HEADS):
        sl = slice(hh * GLA_DV, (hh + 1) * GLA_DV)
        mix.append((_rms(o[:, sl], gg_ref[:, sl]) * _silu(r[:, sl])).astype(BF16))
    mix.append(ym_ref[...])
    y = _dot(jnp.concatenate(mix, axis=1), w_ref[...])
    o_ref[...] = x_ref[...] + mod_ref[0, 5:6, :] * y


def _outproj(xs, mod, att, att_c, o_f, o_b, gr, ym, gla_g, w_out, *, layer, n_tiles, mod_row):
    t, d = xs.shape
    row = lambda i: (i, 0)
    const2 = lambda i: (0, 0)
    if att_c is None:
        split_at = None
        att_specs = [pl.BlockSpec((TM, ATTN_Q_W), row)]
        att_args = (att,)
    else:
        split_at = att.shape[0] // TM
        att_specs = [pl.BlockSpec((TM, ATTN_Q_W), lambda i: (jnp.minimum(i, split_at - 1), 0)),
                     pl.BlockSpec((TM, ATTN_Q_W), lambda i: (jnp.maximum(i - split_at, 0), 0))]
        att_args = (att, att_c)
    return pl.pallas_call(
        functools.partial(_outproj_kernel, split_at=split_at),
        grid=(n_tiles,),
        in_specs=[
            pl.BlockSpec((TM, d), row),
            pl.BlockSpec((1, N_MOD, d), lambda i: (mod_row(i), 0, 0))] + att_specs + [
            pl.BlockSpec((TM, GLA_V_W), row),
            pl.BlockSpec((TM, GLA_V_W), row),
            pl.BlockSpec((TM, GLA_V_W), row),
            pl.BlockSpec((TM, GMLP_W), row),
            pl.BlockSpec((1, GLA_V_W), const2),
            pl.BlockSpec((None,) + w_out.shape[1:], lambda i: (layer, 0, 0), pipeline_mode=pl.Buffered(1)),
        ],
        out_specs=pl.BlockSpec((TM, d), row),
        out_shape=jax.ShapeDtypeStruct((n_tiles * TM, d), F32),
        compiler_params=pltpu.CompilerParams(
            dimension_semantics=("parallel",), vmem_limit_bytes=VMEM_LIMIT),
        name="outproj",
    )(xs, mod, *att_args, o_f, o_b, gr, ym, gla_g.reshape(1, GLA_V_W), w_out)


def _rope_tables(n_lat):
    rows = n_lat // GRID_W
    row = jnp.repeat(jnp.arange(rows, dtype=F32), GRID_W)
    col = jnp.broadcast_to(jnp.arange(GRID_W, dtype=F32), (rows, GRID_W)).reshape(-1)
    nf = HEAD_DIM // 4
    inv = ROPE_THETA ** (-jnp.arange(nf, dtype=F32) / nf)
    ar, ac = row[:, None] * inv, col[:, None] * inv
    cos = jnp.concatenate([jnp.cos(ar), jnp.cos(ar), jnp.cos(ac), jnp.cos(ac)], axis=-1)
    sin = jnp.concatenate([-jnp.sin(ar), jnp.sin(ar), -jnp.sin(ac), jnp.sin(ac)], axis=-1)
    cos = jnp.concatenate([cos, jnp.ones((TM, HEAD_DIM), F32)], axis=0)
    sin = jnp.concatenate([sin, jnp.zeros((TM, HEAD_DIM), F32)], axis=0)
    return cos, sin


def kernel(x, c, ctx, c_ctx, mod_w, mod_b, norm_g, ffn1_w_gu, ffn1_w_down, ffn2_w_gu, ffn2_w_down, w_in, w_out,
           qk_norm_g, gla_gate_w, gla_gate_b, gla_norm_g, gmlp_w_s, gmlp_b_s, gmlp_norm_g, final_norm_g):
    batch, n_lat, d = x.shape
    n_ctx = ctx.shape[1]
    depth = mod_w.shape[0]
    assert n_lat % TM == 0 and (batch * n_ctx) % TM == 0 and n_lat % TKA == 0 and n_lat % TQ == 0
    assert n_ctx % (GLA_STEP_CHUNKS * CHUNK) == 0 and n_lat % n_ctx == 0 and batch + 1 <= 8 and TM == TK and TK % n_ctx == 0
    lat_tiles = batch * n_lat // TM
    all_tiles = lat_tiles + batch * n_ctx // TM
    tiles_per_batch = n_lat // TM

    def mod_row(i):
        return jnp.minimum(i // tiles_per_batch, batch)

    tm_ffn = TM_FFN if n_lat % TM_FFN == 0 else TM
    ffn_lat_tiles = batch * n_lat // tm_ffn
    ffn_all_tiles = ffn_lat_tiles + pl.cdiv(batch * n_ctx, tm_ffn)

    def mod_row_ffn(i):
        return jnp.minimum(i // (n_lat // tm_ffn), batch)

    def rope_row(i):
        return jnp.where(i < lat_tiles, i % tiles_per_batch, tiles_per_batch)

    cc = jnp.zeros((8, d), F32).at[:batch].set(c).at[batch].set(c_ctx)
    mod_all = _modulation(cc, mod_w, mod_b).reshape(depth, 8, N_MOD, d)
    cos_t, sin_t = _rope_tables(n_lat)
    offs = np.cumsum([0, ATTN_Q_W, ATTN_KV_W, ATTN_KV_W, GLA_K_W, GLA_K_W, GLA_V_W, GLA_V_W, 2 * GLA_GATE_RANK,
                      GMLP_W, GMLP_W])
    lr0, lr1 = int(offs[7]), int(offs[8])
    w_head = w_in[:, :, :lr0].astype(BF16)
    w_tail = jnp.concatenate(
        [w_in[:, :, lr1:], w_in[:, :, lr0:lr1],
         jnp.zeros((depth, d, GLR_PAD - 2 * GLA_GATE_RANK), w_in.dtype)], axis=-1).astype(BF16)
    w_out_b = w_out.astype(BF16)
    f1gu, f1d = ffn1_w_gu.astype(BF16), ffn1_w_down.astype(BF16)
    f2gu, f2d = ffn2_w_gu.astype(BF16), ffn2_w_down.astype(BF16)
    xs = x.reshape(batch * n_lat, d)
    xc = ctx.reshape(batch * n_ctx, d)

    for l in range(depth):
        last = l == depth - 1
        mod = mod_all[l]
        gate_w_r = jnp.zeros((GLR_PAD, 2 * GLA_K_W), F32)
        gate_w_r = gate_w_r.at[:GLA_GATE_RANK, :GLA_K_W].set(gla_gate_w[l, 0])
        gate_w_r = gate_w_r.at[GLA_GATE_RANK:2 * GLA_GATE_RANK, GLA_K_W:].set(gla_gate_w[l, 1]).astype(BF16)
        gate_b_r = gla_gate_b[l].reshape(1, 2 * GLA_K_W)
        bs_b = jnp.broadcast_to(gmlp_b_s[l][..., None], gmlp_b_s.shape[1:] + (GMLP_GROUP_DIM,))

        if xc is not None:
            xs = _ffn(xs, mod, norm_g[l, 0], f1gu, f1d, final_norm_g, layer=l, sub=0, tm=TM,
                      n_tiles=all_tiles, mod_row=mod_row, final=False, xc=xc)
        else:
            xs = _ffn(xs, mod, norm_g[l, 0], f1gu, f1d, final_norm_g, layer=l, sub=0, tm=tm_ffn,
                      n_tiles=ffn_all_tiles, mod_row=mod_row_ffn, final=False)
        xc = None
        q, k, vt, gq, gk, gv, gr, gf, gb, ym = _inproj(
            xs, mod, norm_g[l, 1], w_head, w_tail, qk_norm_g[l], cos_t, sin_t, gate_w_r, gate_b_r,
            gmlp_w_s[l].astype(BF16), bs_b, gmlp_norm_g[l], layer=l, mod_row=mod_row, rope_row=rope_row)
        att = _attention(q, k, vt, batch=batch, n_lat=n_lat, n_ctx=n_ctx, latent=True)
        o_f, o_b = _gla(gq, gk, gv, gf, gb, batch=batch, n_lat=n_lat, n_ctx=n_ctx)
        att_c = None if last else _attention(q, k, vt, batch=batch, n_lat=n_lat, n_ctx=n_ctx, latent=False)
        n_tiles = lat_tiles if last else all_tiles
        xs = _outproj(xs, mod, att, att_c, o_f, o_b, gr, ym, gla_norm_g[l], w_out_b,
                      layer=l, n_tiles=n_tiles, mod_row=mod_row)
        xs = _ffn(xs, mod, norm_g[l, 2], f2gu, f2d, final_norm_g, layer=l, sub=2, tm=tm_ffn,
                  n_tiles=ffn_lat_tiles if last else ffn_all_tiles, mod_row=mod_row_ffn, final=last)
    return xs.reshape(batch, n_lat, d)
```

```python
import functools

import numpy as np
import jax
import jax.numpy as jnp
from jax import lax
from jax.experimental import pallas as pl
from jax.experimental.pallas import tpu as pltpu

F32 = jnp.float32
BF16 = jnp.bfloat16

EPS = 1e-6
N_MOD = 9
HEAD_DIM = 128
ATTN_HEADS = 8
ATTN_KV_HEADS = 2
ATTN_GROUP = ATTN_HEADS // ATTN_KV_HEADS
ROPE_THETA = 10000.0
GRID_W = 64
GLA_HEADS = 4
GLA_DK = 64
GLA_DV = 128
GLA_GATE_RANK = 16
GLA_TAU = 16.0
LOG2E = 1.4426950408889634
CHUNK = 128
GMLP_GROUPS = 4
GMLP_GROUP_DIM = 128

ATTN_Q_W = ATTN_HEADS * HEAD_DIM
ATTN_KV_W = ATTN_KV_HEADS * HEAD_DIM
GLA_K_W = GLA_HEADS * GLA_DK
GLA_V_W = GLA_HEADS * GLA_DV
GMLP_W = GMLP_GROUPS * GMLP_GROUP_DIM
LANES = 128
GLR_PAD = LANES
N_LEVELS = 7

OFF_AQ = 0
OFF_AK = OFF_AQ + ATTN_Q_W
OFF_AV = OFF_AK + ATTN_KV_W
OFF_GQ = OFF_AV + ATTN_KV_W
OFF_GK = OFF_GQ + GLA_K_W
OFF_GV = OFF_GK + GLA_K_W
OFF_GR = OFF_GV + GLA_V_W
OFF_MU = OFF_GR + GLA_V_W
OFF_MV = OFF_MU + GMLP_W
OFF_LR = OFF_MV + GMLP_W
IN_W_R = OFF_LR + GLR_PAD

TM = 512
TF = 512
TM_FFN = 512
FFN_SUB = 512
TQ = 256
TK = 512
TKA = 512
GLA_STEP_CHUNKS = 2
SUM_ROWS = 16
ATTN_UNROLL = 32
MAX_SAFE_JUMP = 100.0
MOD_TN = 2048
VMEM_LIMIT = 56 * 1024 * 1024


def _sigmoid(x):
    return 1.0 / (1.0 + jnp.exp(-x))


def _silu(x):
    return x * _sigmoid(x)


def _rms(x, g):
    ms = jnp.mean(x * x, axis=-1, keepdims=True)
    return x * lax.rsqrt(ms + EPS) * g


def _norm_mod(x, g, mod_ref, i):
    ms = jnp.mean(x * x, axis=-1, keepdims=True)
    gain = g * (1.0 + mod_ref[0, 3 * i + 1:3 * i + 2, :])
    return x * lax.rsqrt(ms + EPS) * gain + mod_ref[0, 3 * i:3 * i + 1, :]


def _dot(a, b):
    return jnp.dot(a, b, preferred_element_type=F32)


def _dot_t(a, b):
    return lax.dot_general(a, b, (((1,), (1,)), ((), ())), preferred_element_type=F32)


def _tdot(a, b):
    return lax.dot_general(a, b, (((0,), (0,)), ((), ())), preferred_element_type=F32)


def _mod_kernel(c_ref, w_ref, b_ref, o_ref):
    sc = _silu(c_ref[...]).astype(BF16)
    o_ref[0] = _dot(sc, w_ref[0].astype(BF16)) + b_ref[0]


def _modulation(cc, mod_w, mod_b):
    depth, d, n = mod_w.shape
    return pl.pallas_call(
        _mod_kernel,
        grid=(depth, n // MOD_TN),
        in_specs=[
            pl.BlockSpec((8, d), lambda l, j: (0, 0)),
            pl.BlockSpec((1, d, MOD_TN), lambda l, j: (l, 0, j)),
            pl.BlockSpec((1, 1, MOD_TN), lambda l, j: (l, 0, j)),
        ],
        out_specs=pl.BlockSpec((1, 8, MOD_TN), lambda l, j: (l, 0, j)),
        out_shape=jax.ShapeDtypeStruct((depth, 8, n), F32),
        compiler_params=pltpu.CompilerParams(
            dimension_semantics=("parallel", "parallel"), vmem_limit_bytes=VMEM_LIMIT),
        name="modulation",
    )(cc, mod_w, mod_b.reshape(depth, 1, n))


def _ffn_kernel(*refs, sub, final, split_at):
    if split_at is None:
        x_ref, mod_ref, g_ref, wg_ref, wu_ref, wd_ref, fg_ref, o_ref, h_ref = refs
        sources = [(None, x_ref)]
    else:
        x_ref, xc_ref, mod_ref, g_ref, wg_ref, wu_ref, wd_ref, fg_ref, o_ref, h_ref = refs
        is_ctx = pl.program_id(0) >= split_at
        sources = [(jnp.logical_not(is_ctx), x_ref), (is_ctx, xc_ref)]
    j = pl.program_id(1)
    last_j = pl.num_programs(1) - 1
    subs = [slice(r, r + FFN_SUB) for r in range(0, o_ref.shape[0], FFN_SUB)]

    def prologue(src_ref):
        for rs in subs:
            h_ref[rs, :] = _norm_mod(src_ref[rs, :], g_ref[...], mod_ref, sub).astype(BF16)

    def epilogue(src_ref):
        for rs in subs:
            y = src_ref[rs, :] + mod_ref[0, 3 * sub + 2:3 * sub + 3, :] * (0.5 * o_ref[rs, :])
            if final:
                y = _rms(y, fg_ref[...])
            o_ref[rs, :] = y

    for cond, src_ref in sources:
        first = j == 0
        pl.when(first if cond is None else jnp.logical_and(first, cond))(functools.partial(prologue, src_ref))

    for rs in subs:
        h = h_ref[rs, :]
        hw = TF // 2
        parts = [(_silu(_dot(h, wg_ref[:, s0:s0 + hw])) * _dot(h, wu_ref[:, s0:s0 + hw])).astype(BF16)
                 for s0 in range(0, TF, hw)]
        for c in range(0, o_ref.shape[1], TF):
            d = _dot(parts[0], wd_ref[0:hw, c:c + TF]) + _dot(parts[1], wd_ref[hw:TF, c:c + TF])
            o_ref[rs, c:c + TF] = jnp.where(j == 0, d, o_ref[rs, c:c + TF] + d)

    for cond, src_ref in sources:
        last = j == last_j
        pl.when(last if cond is None else jnp.logical_and(last, cond))(functools.partial(epilogue, src_ref))


def _ffn(xs, mod, g, w_gu, w_down, final_g, *, layer, sub, tm, n_tiles, mod_row, final, xc=None):
    t, d = xs.shape
    f = w_down.shape[1]
    nf = f // TF
    rows_out = t if xc is None else t + xc.shape[0]
    rows_out = min(rows_out, n_tiles * tm)
    if xc is None:
        split_at = None
        x_specs = [pl.BlockSpec((tm, d), lambda i, j: (i, 0))]
        x_args = (xs,)
    else:
        split_at = t // tm
        x_specs = [pl.BlockSpec((tm, d), lambda i, j: (jnp.minimum(i, split_at - 1), 0)),
                   pl.BlockSpec((tm, d), lambda i, j: (jnp.maximum(i - split_at, 0), 0))]
        x_args = (xs, xc)
    kern = functools.partial(_ffn_kernel, sub=sub, final=final, split_at=split_at)
    return pl.pallas_call(
        kern,
        grid=(n_tiles, nf),
        in_specs=x_specs + [
            pl.BlockSpec((1, N_MOD, d), lambda i, j: (mod_row(i), 0, 0)),
            pl.BlockSpec((1, d), lambda i, j: (0, 0)),
            pl.BlockSpec((None, d, TF), lambda i, j: (layer, 0, j)),
            pl.BlockSpec((None, d, TF), lambda i, j: (layer, 0, j + nf)),
            pl.BlockSpec((None, TF, d), lambda i, j: (layer, j, 0)),
            pl.BlockSpec((1, d), lambda i, j: (0, 0)),
        ],
        out_specs=pl.BlockSpec((tm, d), lambda i, j: (i, 0)),
        out_shape=jax.ShapeDtypeStruct((rows_out, d), F32),
        scratch_shapes=[pltpu.VMEM((tm, d), BF16)],
        compiler_params=pltpu.CompilerParams(
            dimension_semantics=("parallel", "arbitrary"), vmem_limit_bytes=VMEM_LIMIT),
        name="ffn_final" if final else "ffn",
    )(*x_args, mod, g.reshape(1, d), w_gu, w_gu, w_down, final_g.reshape(1, d))


def _rope(x, cos, sin_signed, lane_low):
    partner = jnp.where(lane_low, pltpu.roll(x, LANES - 32, 1), pltpu.roll(x, 32, 1))
    return x * cos + partner * sin_signed


def _log_sigmoid(x):
    return jnp.minimum(x, 0.0) - jnp.log(1.0 + jnp.exp(-jnp.abs(x)))


def _inproj_kernel(x_ref, mod_ref, g_ref, w_ref, wt_ref, qkg_ref, cos_ref, sin_ref, gw_ref, gbias_ref,
                   ws_ref, bs_ref, gmg_ref,
                   q_ref, k_ref, vt_ref, gq_ref, gk_ref, gv_ref, gr_ref, gf_ref, gb_ref, ym_ref):
    _inproj_rows(slice(0, x_ref.shape[0]), x_ref, mod_ref, g_ref, w_ref, wt_ref, qkg_ref, cos_ref, sin_ref,
                 gw_ref, gbias_ref, ws_ref, bs_ref, gmg_ref,
                 q_ref, k_ref, vt_ref, gq_ref, gk_ref, gv_ref, gr_ref, gf_ref, gb_ref, ym_ref)


def _inproj_rows(rs, x_ref, mod_ref, g_ref, w_ref, wt_ref, qkg_ref, cos_ref, sin_ref, gw_ref, gbias_ref,
                 ws_ref, bs_ref, gmg_ref,
                 q_ref, k_ref, vt_ref, gq_ref, gk_ref, gv_ref, gr_ref, gf_ref, gb_ref, ym_ref):
    h = _norm_mod(x_ref[rs, :], g_ref[...], mod_ref, 1).astype(BF16)
    cos = cos_ref[rs, :]
    sin = sin_ref[rs, :]
    lane = lax.broadcasted_iota(jnp.int32, cos.shape, 1)
    lane_low = (lane & 63) < 32

    def proj(off, width):
        if off < OFF_MU:
            return _dot(h, w_ref[:, off:off + width])
        return _dot(h, wt_ref[:, off - OFF_MU:off - OFF_MU + width])

    scale = HEAD_DIM ** -0.5 * LOG2E
    zq = proj(OFF_AQ, ATTN_Q_W)
    for hh in range(ATTN_HEADS):
        sl = slice(hh * HEAD_DIM, (hh + 1) * HEAD_DIM)
        qh = _rope(_rms(zq[:, sl], qkg_ref[0:1, :]), cos, sin, lane_low)
        q_ref[rs, sl] = (qh * scale).astype(BF16)
    zk = proj(OFF_AK, ATTN_KV_W)
    for hh in range(ATTN_KV_HEADS):
        sl = slice(hh * HEAD_DIM, (hh + 1) * HEAD_DIM)
        k_ref[rs, sl] = _rope(_rms(zk[:, sl], qkg_ref[1:2, :]), cos, sin, lane_low).astype(BF16)
    vt_ref[0, :, rs] = proj(OFF_AV, ATTN_KV_W).T.astype(BF16)

    gq_ref[rs, :] = proj(OFF_GQ, GLA_K_W) * (GLA_DK ** -0.5)
    gk_ref[rs, :] = proj(OFF_GK, GLA_K_W)
    gv_ref[rs, :] = proj(OFF_GV, GLA_V_W).astype(BF16)
    gr_ref[rs, :] = proj(OFF_GR, GLA_V_W)
    lr = proj(OFF_LR, GLR_PAD).astype(BF16)
    logits = _dot(lr, gw_ref[...]) + gbias_ref[...]
    ld = _log_sigmoid(logits) * (1.0 / GLA_TAU)
    gf_ref[rs, :] = ld[:, :GLA_K_W]
    gb_ref[rs, :] = ld[:, GLA_K_W:]

    mu = proj(OFF_MU, GMLP_W)
    vn = _rms(proj(OFF_MV, GMLP_W), gmg_ref[...]).astype(BF16)
    for c in range((rs.stop - rs.start) // CHUNK):
        rows = slice(c * CHUNK, (c + 1) * CHUNK)
        out_rows = slice(rs.start + c * CHUNK, rs.start + (c + 1) * CHUNK)
        for gi in range(GMLP_GROUPS):
            cols = slice(gi * GMLP_GROUP_DIM, (gi + 1) * GMLP_GROUP_DIM)
            z = _dot(ws_ref[gi], vn[rows, cols]) + bs_ref[gi]
            ym_ref[out_rows, cols] = (mu[rows, cols] * z).astype(BF16)


def _inproj(xs, mod, g, w_head, w_tail, qk_g, cos_t, sin_t, gate_w_r, gate_b_r, ws, bs_b, gm_g, *, layer, mod_row,
            rope_row):
    t, d = xs.shape
    n_tiles = t // TM
    row = lambda i: (i, 0)
    const2 = lambda i: (0, 0)
    const3 = lambda i: (0, 0, 0)
    widths = [(ATTN_Q_W, BF16), (ATTN_KV_W, BF16), None, (GLA_K_W, F32), (GLA_K_W, F32),
              (GLA_V_W, BF16), (GLA_V_W, F32), (GLA_K_W, F32), (GLA_K_W, F32), (GMLP_W, BF16)]
    out_specs = [pl.BlockSpec((1, ATTN_KV_W, TM), lambda i: (i, 0, 0)) if w is None else pl.BlockSpec((TM, w[0]), row)
                 for w in widths]
    out_shape = [jax.ShapeDtypeStruct((n_tiles, ATTN_KV_W, TM), BF16) if w is None
                 else jax.ShapeDtypeStruct((t, w[0]), w[1]) for w in widths]
    return pl.pallas_call(
        _inproj_kernel,
        grid=(n_tiles,),
        in_specs=[
            pl.BlockSpec((TM, d), row),
            pl.BlockSpec((1, N_MOD, d), lambda i: (mod_row(i), 0, 0)),
            pl.BlockSpec((1, d), const2),
            pl.BlockSpec((None, d, OFF_MU), lambda i: (layer, 0, 0), pipeline_mode=pl.Buffered(1)),
            pl.BlockSpec((None, d, IN_W_R - OFF_MU), lambda i: (layer, 0, 0), pipeline_mode=pl.Buffered(1)),
            pl.BlockSpec((2, HEAD_DIM), const2),
            pl.BlockSpec((TM, HEAD_DIM), lambda i: (rope_row(i), 0)),
            pl.BlockSpec((TM, HEAD_DIM), lambda i: (rope_row(i), 0)),
            pl.BlockSpec((GLR_PAD, 2 * GLA_K_W), const2),
            pl.BlockSpec((1, 2 * GLA_K_W), const2),
            pl.BlockSpec((GMLP_GROUPS, CHUNK, CHUNK), const3),
            pl.BlockSpec((GMLP_GROUPS, CHUNK, GMLP_GROUP_DIM), const3),
            pl.BlockSpec((1, GMLP_W), const2),
        ],
        out_specs=out_specs,
        out_shape=out_shape,
        compiler_params=pltpu.CompilerParams(
            dimension_semantics=("parallel",), vmem_limit_bytes=VMEM_LIMIT),
        name="inproj",
    )(xs, mod, g.reshape(1, d), w_head, w_tail, qk_g, cos_t, sin_t, gate_w_r, gate_b_r, ws, bs_b,
      gm_g.reshape(1, GMLP_W))


def _attn_kernel(*refs, n_lat_tiles):
    if n_lat_tiles:
        q_ref, kc_ref, vtc_ref, kl_ref, vtl_ref, o_ref, s_ref, off_ref, jump_ref, m_ref, acc_ref = refs
    else:
        q_ref, kc_ref, vtc_ref, o_ref, m_ref, acc_ref = refs
    tq = q_ref.shape[0]
    q = jnp.concatenate([q_ref[:, g * HEAD_DIM:(g + 1) * HEAD_DIM] for g in range(ATTN_GROUP)], axis=0)

    def scores(k):
        return _dot_t(k, q)

    def with_ones(vt):
        return jnp.concatenate([vt, jnp.ones((SUM_ROWS, vt.shape[1]), BF16)], axis=0)

    def lat_keys(t):
        start = pl.multiple_of(t * TKA, TKA)
        return kl_ref[pl.ds(start, TKA), :]

    def lat_values(t):
        n = TKA // TK
        return jnp.concatenate([vtl_ref[n * t + u] for u in range(n)], axis=1) if n > 1 else vtl_ref[t]

    def ctx_tile():
        s = scores(kc_ref[...])
        m_new = jnp.max(s, axis=0, keepdims=True)
        m_ref[...] = m_new
        acc_ref[...] = _dot(with_ones(vtc_ref[0]), jnp.exp2(s - m_new).astype(BF16))

    def lagged_tile(t, carry):
        c = m_ref[...]
        s = scores(lat_keys(t))
        tmax = jnp.max(s, axis=0, keepdims=True)
        p = jnp.exp2(s - c)
        alpha = jnp.exp2(off_ref[...] - c)
        acc_ref[...] = alpha * acc_ref[...] + _dot(with_ones(lat_values(t)), p.astype(BF16))
        off_ref[...] = c
        jump_ref[...] = jnp.maximum(jump_ref[...], tmax - c)
        m_ref[...] = jnp.maximum(c, tmax)
        return carry

    def exact_tile(t, carry):
        s_ref[...] = scores(lat_keys(t))
        m_old = m_ref[...]
        m_new = jnp.maximum(m_old, jnp.max(s_ref[...], axis=0, keepdims=True))
        alpha = jnp.exp2(m_old - m_new)
        p = jnp.exp2(s_ref[...] - m_new)
        acc_ref[...] = alpha * acc_ref[...] + _dot(with_ones(lat_values(t)), p.astype(BF16))
        m_ref[...] = m_new
        return carry

    ctx_tile()
    if n_lat_tiles:
        off_ref[...] = m_ref[...]
        jump_ref[...] = jnp.zeros_like(jump_ref)
        lax.fori_loop(0, n_lat_tiles, lagged_tile, 0, unroll=ATTN_UNROLL if n_lat_tiles % ATTN_UNROLL == 0 else 1)

        @pl.when(jnp.max(jump_ref[...]) > MAX_SAFE_JUMP)
        def _():
            ctx_tile()
            lax.fori_loop(0, n_lat_tiles, exact_tile, 0)
    out_t = acc_ref[0:HEAD_DIM, :] / acc_ref[HEAD_DIM:HEAD_DIM + 1, :]
    for g in range(ATTN_GROUP):
        o_ref[:, g * HEAD_DIM:(g + 1) * HEAD_DIM] = out_t[:, g * tq:(g + 1) * tq].T.astype(BF16)


def _attention(q, k, vt, *, batch, n_lat, n_ctx, latent):
    gw = ATTN_GROUP * HEAD_DIM
    ctx_blk0 = (batch * n_lat) // n_ctx
    lat_tiles = (batch * n_lat) // TK
    per_tile = TK // n_ctx
    kc_spec = pl.BlockSpec((n_ctx, HEAD_DIM), lambda b, kh, i: (ctx_blk0 + b, kh))
    vtc_spec = pl.BlockSpec((1, HEAD_DIM, n_ctx), lambda b, kh, i: (lat_tiles + b // per_tile, kh, b % per_tile))
    if latent:
        tq = TQ
        nq = n_lat // tq
        n_lat_tiles = n_lat // TKA
        q_spec = pl.BlockSpec((tq, gw), lambda b, kh, i: (b * nq + i, kh))
        kl_spec = pl.BlockSpec((n_lat, HEAD_DIM), lambda b, kh, i: (b, kh))
        vtl_spec = pl.BlockSpec((n_lat // TK, HEAD_DIM, TK), lambda b, kh, i: (b, kh, 0))
        in_specs = [q_spec, kc_spec, vtc_spec, kl_spec, vtl_spec]
        args = (q, k, vt, k, vt)
        out_rows = batch * n_lat
    else:
        tq = n_ctx
        nq = 1
        n_lat_tiles = 0
        q_spec = pl.BlockSpec((tq, gw), lambda b, kh, i: (ctx_blk0 + b, kh))
        in_specs = [q_spec, kc_spec, vtc_spec]
        args = (q, k, vt)
        out_rows = batch * n_ctx
    cols = ATTN_GROUP * tq
    scratch = [pltpu.VMEM((1, cols), F32), pltpu.VMEM((HEAD_DIM + SUM_ROWS, cols), F32)]
    if latent:
        scratch = [pltpu.VMEM((TKA, cols), F32), pltpu.VMEM((1, cols), F32), pltpu.VMEM((1, cols), F32)] + scratch
    return pl.pallas_call(
        functools.partial(_attn_kernel, n_lat_tiles=n_lat_tiles),
        grid=(batch, ATTN_KV_HEADS, nq),
        in_specs=in_specs,
        out_specs=pl.BlockSpec((tq, gw), lambda b, kh, i: (b * nq + i, kh)),
        out_shape=jax.ShapeDtypeStruct((out_rows, ATTN_Q_W), BF16),
        scratch_shapes=scratch,
        compiler_params=pltpu.CompilerParams(
            dimension_semantics=("parallel", "parallel", "arbitrary"), vmem_limit_bytes=VMEM_LIMIT),
        name="attn_lat" if latent else "attn_ctx",
    )(*args)


def _gla_consts():
    idx = np.arange(CHUNK)
    tri = (idx[None, :] <= idx[:, None]).astype(np.float32)
    mats_f, mats_b = [tri], [tri.T]
    for lvl in range(1, N_LEVELS + 1):
        s = (2 * CHUNK) >> lvl
        base = (idx // s) * s
        mats_f.append(tri[base + s // 2 - 1])
        mats_b.append(tri.T[base + s // 2])
    cm = np.stack([np.concatenate(mats_f, 0), np.concatenate(mats_b, 0)])
    cm = np.concatenate([cm, cm], axis=-1)
    x = idx[:, None] ^ idx[None, :]
    hb = np.floor(np.log2(np.maximum(x, 1))).astype(np.int32)
    lv = np.where(x == 0, 0, N_LEVELS - hb)
    lv_f = np.where(idx[:, None] >= idx[None, :], lv, -1)
    lv_b = np.where(idx[:, None] <= idx[None, :], lv, -1)
    return cm, np.stack([lv_f, lv_b]).astype(np.int32)


def _gla_chunk(cm_ref, lv_ref, q_ref, k_ref, v_ref, g_ref, o_ref, st_ref, d, rows):
    g = g_ref[rows, :]
    g_hi = g.astype(BF16)
    g_lo = (g - g_hi.astype(F32)).astype(BF16)
    cums = _dot(cm_ref[d], jnp.concatenate([g_hi, g_lo], axis=0))
    cum = cums[0:CHUNK]
    q = q_ref[rows, :]
    k = k_ref[rows, :]
    lv = lv_ref[d]
    last = CHUNK - 1 if d == 0 else 0
    tail = cum[last:last + 1, :]
    lane = lax.broadcasted_iota(jnp.int32, (CHUNK, LANES), 1)
    low = lane < GLA_DK

    qs = [q.astype(BF16)]
    ks = [k.astype(BF16)]
    for lvl in range(1, N_LEVELS + 1):
        decay = jnp.exp(-jnp.abs(cum - cums[lvl * CHUNK:(lvl + 1) * CHUNK]))
        qs.append((q * decay).astype(BF16))
        ks.append((k * decay).astype(BF16))
    q_in = (q * jnp.exp(cum)).astype(BF16)
    k_out = (k * jnp.exp(tail - cum)).astype(BF16)
    zero = jnp.zeros((CHUNK, LANES), BF16)
    lv2 = jnp.concatenate([lv, lv], axis=1)

    def per_head_rows(x):
        return jnp.concatenate([jnp.where(low, x, zero), jnp.where(low, zero, x)], axis=0)

    for p in range(GLA_HEADS // 2):
        pl_sl = slice(p * LANES, (p + 1) * LANES)
        pv_sl = slice(2 * p * GLA_DV, 2 * (p + 1) * GLA_DV)
        st = st_ref[d, p]
        a = jnp.zeros((CHUNK, 2 * CHUNK), F32)
        for lvl in range(N_LEVELS + 1):
            a = jnp.where(lv2 == lvl, _dot_t(qs[lvl][:, pl_sl], per_head_rows(ks[lvl][:, pl_sl])), a)
        v2 = v_ref[rows, pv_sl]
        zv = jnp.zeros((CHUNK, GLA_DV), BF16)
        v_bd = jnp.concatenate([jnp.concatenate([v2[:, :GLA_DV], zv], axis=1),
                                jnp.concatenate([zv, v2[:, GLA_DV:]], axis=1)], axis=0)
        inter = _dot_t(per_head_rows(q_in[:, pl_sl]), st.astype(BF16))
        o_ref[rows, pv_sl] = (_dot(a.astype(BF16), v_bd)
                              + jnp.concatenate([inter[:CHUNK], inter[CHUNK:]], axis=1))
        upd = _tdot(v2, k_out[:, pl_sl])
        st_ref[d, p] = st * jnp.exp(tail[:, pl_sl]) + jnp.where(low, upd[:GLA_DV], upd[GLA_DV:])


def _gla_kernel(cm_ref, lv_ref, qf, kf, vf, gf, qb, kb, vb, gb, of_ref, ob_ref, st_ref):
    @pl.when(pl.program_id(1) == 0)
    def _():
        st_ref[...] = jnp.zeros_like(st_ref)

    n = qf.shape[0] // CHUNK
    for c in range(n):
        _gla_chunk(cm_ref, lv_ref, qf, kf, vf, gf, of_ref, st_ref, 0, slice(c * CHUNK, (c + 1) * CHUNK))
        cb = n - 1 - c
        _gla_chunk(cm_ref, lv_ref, qb, kb, vb, gb, ob_ref, st_ref, 1, slice(cb * CHUNK, (cb + 1) * CHUNK))


def _gla(gq, gk, gv, gf, gb, *, batch, n_lat, n_ctx):
    t = gq.shape[0]
    rows = GLA_STEP_CHUNKS * CHUNK
    cl, cc = n_lat // rows, n_ctx // rows
    ctx0 = batch * cl
    cm_np, lv_np = _gla_consts()
    cm = jnp.asarray(cm_np, BF16)
    lv = jnp.asarray(lv_np)

    def fwd(b, s):
        return (jnp.where(s < cc, ctx0 + b * cc + s, b * cl + s - cc), 0)

    def bwd(b, s):
        return (jnp.where(s < cc, ctx0 + b * cc + (cc - 1 - s), b * cl + (cl - 1 - (s - cc))), 0)

    def specs(m):
        return [pl.BlockSpec((rows, GLA_K_W), m), pl.BlockSpec((rows, GLA_K_W), m),
                pl.BlockSpec((rows, GLA_V_W), m), pl.BlockSpec((rows, GLA_K_W), m)]

    return pl.pallas_call(
        _gla_kernel,
        grid=(batch, cc + cl),
        in_specs=[pl.BlockSpec(cm.shape, lambda b, s: (0, 0, 0)), pl.BlockSpec(lv.shape, lambda b, s: (0, 0, 0))]
        + specs(fwd) + specs(bwd),
        out_specs=[pl.BlockSpec((rows, GLA_V_W), fwd), pl.BlockSpec((rows, GLA_V_W), bwd)],
        out_shape=[jax.ShapeDtypeStruct((t, GLA_V_W), F32)] * 2,
        scratch_shapes=[pltpu.VMEM((2, GLA_HEADS // 2, GLA_DV, LANES), F32)],
        compiler_params=pltpu.CompilerParams(
            dimension_semantics=("parallel", "arbitrary"), vmem_limit_bytes=VMEM_LIMIT),
        name="gla",
    )(cm, lv, gq, gk, gv, gf, gq, gk, gv, gb)


def _outproj_kernel(*refs, split_at):
    if split_at is None:
        x_ref, mod_ref, att_ref, of_ref, ob_ref, gr_ref, ym_ref, gg_ref, w_ref, o_ref = refs
        att = att_ref[...]
    else:
        x_ref, mod_ref, att_ref, attc_ref, of_ref, ob_ref, gr_ref, ym_ref, gg_ref, w_ref, o_ref = refs
        att = jnp.where(pl.program_id(0) >= split_at, attc_ref[...], att_ref[...])
    o = of_ref[...] + ob_ref[...]
    r = gr_ref[...]
    mix = [att]
    for hh in range(GLA_HEADS):
        sl = slice(hh * GLA_DV, (hh + 1) * GLA_DV)
        mix.append((_rms(o[:, sl], gg_ref[:, sl]) * _silu(r[:, sl])).astype(BF16))
    mix.append(ym_ref[...])
    y = _dot(jnp.concatenate(mix, axis=1), w_ref[...])
    o_ref[...] = x_ref[...] + mod_ref[0, 5:6, :] * y


def _outproj(xs, mod, att, att_c, o_f, o_b, gr, ym, gla_g, w_out, *, layer, n_tiles, mod_row):
    t, d = xs.shape
    row = lambda i: (i, 0)
    const2 = lambda i: (0, 0)
    if att_c is None:
        split_at = None
        att_specs = [pl.BlockSpec((TM, ATTN_Q_W), row)]
        att_args = (att,)
    else:
        split_at = att.shape[0] // TM
        att_specs = [pl.BlockSpec((TM, ATTN_Q_W), lambda i: (jnp.minimum(i, split_at - 1), 0)),
                     pl.BlockSpec((TM, ATTN_Q_W), lambda i: (jnp.maximum(i - split_at, 0), 0))]
        att_args = (att, att_c)
    return pl.pallas_call(
        functools.partial(_outproj_kernel, split_at=split_at),
        grid=(n_tiles,),
        in_specs=[
            pl.BlockSpec((TM, d), row),
            pl.BlockSpec((1, N_MOD, d), lambda i: (mod_row(i), 0, 0))] + att_specs + [
            pl.BlockSpec((TM, GLA_V_W), row),
            pl.BlockSpec((TM, GLA_V_W), row),
            pl.BlockSpec((TM, GLA_V_W), row),
            pl.BlockSpec((TM, GMLP_W), row),
            pl.BlockSpec((1, GLA_V_W), const2),
            pl.BlockSpec((None,) + w_out.shape[1:], lambda i: (layer, 0, 0), pipeline_mode=pl.Buffered(1)),
        ],
        out_specs=pl.BlockSpec((TM, d), row),
        out_shape=jax.ShapeDtypeStruct((n_tiles * TM, d), F32),
        compiler_params=pltpu.CompilerParams(
            dimension_semantics=("parallel",), vmem_limit_bytes=VMEM_LIMIT),
        name="outproj",
    )(xs, mod, *att_args, o_f, o_b, gr, ym, gla_g.reshape(1, GLA_V_W), w_out)


def _rope_tables(n_lat):
    rows = n_lat // GRID_W
    row = jnp.repeat(jnp.arange(rows, dtype=F32), GRID_W)
    col = jnp.broadcast_to(jnp.arange(GRID_W, dtype=F32), (rows, GRID_W)).reshape(-1)
    nf = HEAD_DIM // 4
    inv = ROPE_THETA ** (-jnp.arange(nf, dtype=F32) / nf)
    ar, ac = row[:, None] * inv, col[:, None] * inv
    cos = jnp.concatenate([jnp.cos(ar), jnp.cos(ar), jnp.cos(ac), jnp.cos(ac)], axis=-1)
    sin = jnp.concatenate([-jnp.sin(ar), jnp.sin(ar), -jnp.sin(ac), jnp.sin(ac)], axis=-1)
    cos = jnp.concatenate([cos, jnp.ones((TM, HEAD_DIM), F32)], axis=0)
    sin = jnp.concatenate([sin, jnp.zeros((TM, HEAD_DIM), F32)], axis=0)
    return cos, sin


def kernel(x, c, ctx, c_ctx, mod_w, mod_b, norm_g, ffn1_w_gu, ffn1_w_down, ffn2_w_gu, ffn2_w_down, w_in, w_out,
           qk_norm_g, gla_gate_w, gla_gate_b, gla_norm_g, gmlp_w_s, gmlp_b_s, gmlp_norm_g, final_norm_g):
    batch, n_lat, d = x.shape
    n_ctx = ctx.shape[1]
    depth = mod_w.shape[0]
    assert n_lat % TM == 0 and (batch * n_ctx) % TM == 0 and n_lat % TKA == 0 and n_lat % TQ == 0
    assert n_ctx % (GLA_STEP_CHUNKS * CHUNK) == 0 and n_lat % n_ctx == 0 and batch + 1 <= 8 and TM == TK and TK % n_ctx == 0
    lat_tiles = batch * n_lat // TM
    all_tiles = lat_tiles + batch * n_ctx // TM
    tiles_per_batch = n_lat // TM

    def mod_row(i):
        return jnp.minimum(i // tiles_per_batch, batch)

    tm_ffn = TM_FFN if n_lat % TM_FFN == 0 else TM
    ffn_lat_tiles = batch * n_lat // tm_ffn
    ffn_all_tiles = ffn_lat_tiles + pl.cdiv(batch * n_ctx, tm_ffn)

    def mod_row_ffn(i):
        return jnp.minimum(i // (n_lat // tm_ffn), batch)

    def rope_row(i):
        return jnp.where(i < lat_tiles, i % tiles_per_batch, tiles_per_batch)

    cc = jnp.zeros((8, d), F32).at[:batch].set(c).at[batch].set(c_ctx)
    mod_all = _modulation(cc, mod_w, mod_b).reshape(depth, 8, N_MOD, d)
    cos_t, sin_t = _rope_tables(n_lat)
    offs = np.cumsum([0, ATTN_Q_W, ATTN_KV_W, ATTN_KV_W, GLA_K_W, GLA_K_W, GLA_V_W, GLA_V_W, 2 * GLA_GATE_RANK,
                      GMLP_W, GMLP_W])
    lr0, lr1 = int(offs[7]), int(offs[8])
    w_head = w_in[:, :, :lr0].astype(BF16)
    w_tail = jnp.concatenate(
        [w_in[:, :, lr1:], w_in[:, :, lr0:lr1],
         jnp.zeros((depth, d, GLR_PAD - 2 * GLA_GATE_RANK), w_in.dtype)], axis=-1).astype(BF16)
    w_out_b = w_out.astype(BF16)
    f1gu, f1d = ffn1_w_gu.astype(BF16), ffn1_w_down.astype(BF16)
    f2gu, f2d = ffn2_w_gu.astype(BF16), ffn2_w_down.astype(BF16)
    xs = x.reshape(batch * n_lat, d)
    xc = ctx.reshape(batch * n_ctx, d)

    for l in range(depth):
        last = l == depth - 1
        mod = mod_all[l]
        gate_w_r = jnp.zeros((GLR_PAD, 2 * GLA_K_W), F32)
        gate_w_r = gate_w_r.at[:GLA_GATE_RANK, :GLA_K_W].set(gla_gate_w[l, 0])
        gate_w_r = gate_w_r.at[GLA_GATE_RANK:2 * GLA_GATE_RANK, GLA_K_W:].set(gla_gate_w[l, 1]).astype(BF16)
        gate_b_r = gla_gate_b[l].reshape(1, 2 * GLA_K_W)
        bs_b = jnp.broadcast_to(gmlp_b_s[l][..., None], gmlp_b_s.shape[1:] + (GMLP_GROUP_DIM,))

        if xc is not None:
            xs = _ffn(xs, mod, norm_g[l, 0], f1gu, f1d, final_norm_g, layer=l, sub=0, tm=TM,
                      n_tiles=all_tiles, mod_row=mod_row, final=False, xc=xc)
        else:
            xs = _ffn(xs, mod, norm_g[l, 0], f1gu, f1d, final_norm_g, layer=l, sub=0, tm=tm_ffn,
                      n_tiles=ffn_all_tiles, mod_row=mod_row_ffn, final=False)
        xc = None
        q, k, vt, gq, gk, gv, gr, gf, gb, ym = _inproj(
            xs, mod, norm_g[l, 1], w_head, w_tail, qk_norm_g[l], cos_t, sin_t, gate_w_r, gate_b_r,
            gmlp_w_s[l].astype(BF16), bs_b, gmlp_norm_g[l], layer=l, mod_row=mod_row, rope_row=rope_row)
        att = _attention(q, k, vt, batch=batch, n_lat=n_lat, n_ctx=n_ctx, latent=True)
        o_f, o_b = _gla(gq, gk, gv, gf, gb, batch=batch, n_lat=n_lat, n_ctx=n_ctx)
        att_c = None if last else _attention(q, k, vt, batch=batch, n_lat=n_lat, n_ctx=n_ctx, latent=False)
        n_tiles = lat_tiles if last else all_tiles
        xs = _outproj(xs, mod, att, att_c, o_f, o_b, gr, ym, gla_norm_g[l], w_out_b,
                      layer=l, n_tiles=n_tiles, mod_row=mod_row)
        xs = _ffn(xs, mod, norm_g[l, 2], f2gu, f2d, final_norm_g, layer=l, sub=2, tm=tm_ffn,
                  n_tiles=ffn_lat_tiles if last else ffn_all_tiles, mod_row=mod_row_ffn, final=last)
    return xs.reshape(batch, n_lat, d)
```

```python
import functools

import numpy as np
import jax
import jax.numpy as jnp
from jax import lax
from jax.experimental import pallas as pl
from jax.experimental.pallas import tpu as pltpu

F32 = jnp.float32
BF16 = jnp.bfloat16

EPS = 1e-6
N_MOD = 9
HEAD_DIM = 128
ATTN_HEADS = 8
ATTN_KV_HEADS = 2
ATTN_GROUP = ATTN_HEADS // ATTN_KV_HEADS
ROPE_THETA = 10000.0
GRID_W = 64
GLA_HEADS = 4
GLA_DK = 64
GLA_DV = 128
GLA_GATE_RANK = 16
GLA_TAU = 16.0
LOG2E = 1.4426950408889634
CHUNK = 128
GMLP_GROUPS = 4
GMLP_GROUP_DIM = 128

ATTN_Q_W = ATTN_HEADS * HEAD_DIM
ATTN_KV_W = ATTN_KV_HEADS * HEAD_DIM
GLA_K_W = GLA_HEADS * GLA_DK
GLA_V_W = GLA_HEADS * GLA_DV
GMLP_W = GMLP_GROUPS * GMLP_GROUP_DIM
LANES = 128
GLR_PAD = LANES
N_LEVELS = 7

OFF_AQ = 0
OFF_AK = OFF_AQ + ATTN_Q_W
OFF_AV = OFF_AK + ATTN_KV_W
OFF_GQ = OFF_AV + ATTN_KV_W
OFF_GK = OFF_GQ + GLA_K_W
OFF_GV = OFF_GK + GLA_K_W
OFF_GR = OFF_GV + GLA_V_W
OFF_MU = OFF_GR + GLA_V_W
OFF_MV = OFF_MU + GMLP_W
OFF_LR = OFF_MV + GMLP_W
IN_W_R = OFF_LR + GLR_PAD

TM = 512
TF = 512
TM_FFN = 512
FFN_SUB = 512
TQ = 256
TK = 512
TKA = 512
GLA_STEP_CHUNKS = 2
SUM_ROWS = 16
ATTN_UNROLL = 32
MAX_SAFE_JUMP = 100.0
MOD_TN = 2048
VMEM_LIMIT = 56 * 1024 * 1024


def _sigmoid(x):
    return 1.0 / (1.0 + jnp.exp(-x))


def _silu(x):
    return x * _sigmoid(x)


def _rms(x, g):
    ms = jnp.mean(x * x, axis=-1, keepdims=True)
    return x * lax.rsqrt(ms + EPS) * g


def _norm_mod(x, g, mod_ref, i):
    ms = jnp.mean(x * x, axis=-1, keepdims=True)
    gain = g * (1.0 + mod_ref[0, 3 * i + 1:3 * i + 2, :])
    return x * lax.rsqrt(ms + EPS) * gain + mod_ref[0, 3 * i:3 * i + 1, :]


def _dot(a, b):
    return jnp.dot(a, b, preferred_element_type=F32)


def _dot_t(a, b):
    return lax.dot_general(a, b, (((1,), (1,)), ((), ())), preferred_element_type=F32)


def _tdot(a, b):
    return lax.dot_general(a, b, (((0,), (0,)), ((), ())), preferred_element_type=F32)


def _mod_kernel(c_ref, w_ref, b_ref, o_ref):
    sc = _silu(c_ref[...]).astype(BF16)
    o_ref[0] = _dot(sc, w_ref[0].astype(BF16)) + b_ref[0]


def _modulation(cc, mod_w, mod_b):
    depth, d, n = mod_w.shape
    return pl.pallas_call(
        _mod_kernel,
        grid=(depth, n // MOD_TN),
        in_specs=[
            pl.BlockSpec((8, d), lambda l, j: (0, 0)),
            pl.BlockSpec((1, d, MOD_TN), lambda l, j: (l, 0, j)),
            pl.BlockSpec((1, 1, MOD_TN), lambda l, j: (l, 0, j)),
        ],
        out_specs=pl.BlockSpec((1, 8, MOD_TN), lambda l, j: (l, 0, j)),
        out_shape=jax.ShapeDtypeStruct((depth, 8, n), F32),
        compiler_params=pltpu.CompilerParams(
            dimension_semantics=("parallel", "parallel"), vmem_limit_bytes=VMEM_LIMIT),
        name="modulation",
    )(cc, mod_w, mod_b.reshape(depth, 1, n))


def _ffn_kernel(*refs, sub, final, split_at):
    if split_at is None:
        x_ref, mod_ref, g_ref, wg_ref, wu_ref, wd_ref, fg_ref, o_ref, h_ref = refs
        sources = [(None, x_ref)]
    else:
        x_ref, xc_ref, mod_ref, g_ref, wg_ref, wu_ref, wd_ref, fg_ref, o_ref, h_ref = refs
        is_ctx = pl.program_id(0) >= split_at
        sources = [(jnp.logical_not(is_ctx), x_ref), (is_ctx, xc_ref)]
    j = pl.program_id(1)
    last_j = pl.num_programs(1) - 1
    subs = [slice(r, r + FFN_SUB) for r in range(0, o_ref.shape[0], FFN_SUB)]

    def prologue(src_ref):
        for rs in subs:
            h_ref[rs, :] = _norm_mod(src_ref[rs, :], g_ref[...], mod_ref, sub).astype(BF16)

    def epilogue(src_ref):
        for rs in subs:
            y = src_ref[rs, :] + mod_ref[0, 3 * sub + 2:3 * sub + 3, :] * (0.5 * o_ref[rs, :])
            if final:
                y = _rms(y, fg_ref[...])
            o_ref[rs, :] = y

    for cond, src_ref in sources:
        first = j == 0
        pl.when(first if cond is None else jnp.logical_and(first, cond))(functools.partial(prologue, src_ref))

    for rs in subs:
        h = h_ref[rs, :]
        hw = TF // 2
        parts = [(_silu(_dot(h, wg_ref[:, s0:s0 + hw])) * _dot(h, wu_ref[:, s0:s0 + hw])).astype(BF16)
                 for s0 in range(0, TF, hw)]
        for c in range(0, o_ref.shape[1], TF):
            d = _dot(parts[0], wd_ref[0:hw, c:c + TF]) + _dot(parts[1], wd_ref[hw:TF, c:c + TF])
            o_ref[rs, c:c + TF] = jnp.where(j == 0, d, o_ref[rs, c:c + TF] + d)

    for cond, src_ref in sources:
        last = j == last_j
        pl.when(last if cond is None else jnp.logical_and(last, cond))(functools.partial(epilogue, src_ref))


def _ffn(xs, mod, g, w_gu, w_down, final_g, *, layer, sub, tm, n_tiles, mod_row, final, xc=None):
    t, d = xs.shape
    f = w_down.shape[1]
    nf = f // TF
    rows_out = t if xc is None else t + xc.shape[0]
    rows_out = min(rows_out, n_tiles * tm)
    if xc is None:
        split_at = None
        x_specs = [pl.BlockSpec((tm, d), lambda i, j: (i, 0))]
        x_args = (xs,)
    else:
        split_at = t // tm
        x_specs = [pl.BlockSpec((tm, d), lambda i, j: (jnp.minimum(i, split_at - 1), 0)),
                   pl.BlockSpec((tm, d), lambda i, j: (jnp.maximum(i - split_at, 0), 0))]
        x_args = (xs, xc)
    kern = functools.partial(_ffn_kernel, sub=sub, final=final, split_at=split_at)
    return pl.pallas_call(
        kern,
        grid=(n_tiles, nf),
        in_specs=x_specs + [
            pl.BlockSpec((1, N_MOD, d), lambda i, j: (mod_row(i), 0, 0)),
            pl.BlockSpec((1, d), lambda i, j: (0, 0)),
            pl.BlockSpec((None, d, TF), lambda i, j: (layer, 0, j)),
            pl.BlockSpec((None, d, TF), lambda i, j: (layer, 0, j + nf)),
            pl.BlockSpec((None, TF, d), lambda i, j: (layer, j, 0)),
            pl.BlockSpec((1, d), lambda i, j: (0, 0)),
        ],
        out_specs=pl.BlockSpec((tm, d), lambda i, j: (i, 0)),
        out_shape=jax.ShapeDtypeStruct((rows_out, d), F32),
        scratch_shapes=[pltpu.VMEM((tm, d), BF16)],
        compiler_params=pltpu.CompilerParams(
            dimension_semantics=("parallel", "arbitrary"), vmem_limit_bytes=VMEM_LIMIT),
        name="ffn_final" if final else "ffn",
    )(*x_args, mod, g.reshape(1, d), w_gu, w_gu, w_down, final_g.reshape(1, d))


def _rope(x, cos, sin_signed, lane_low):
    partner = jnp.where(lane_low, pltpu.roll(x, LANES - 32, 1), pltpu.roll(x, 32, 1))
    return x * cos + partner * sin_signed


def _log_sigmoid(x):
    return jnp.minimum(x, 0.0) - jnp.log(1.0 + jnp.exp(-jnp.abs(x)))


def _inproj_kernel(x_ref, mod_ref, g_ref, w_ref, wt_ref, qkg_ref, cos_ref, sin_ref, gw_ref, gbias_ref,
                   ws_ref, bs_ref, gmg_ref,
                   q_ref, k_ref, vt_ref, gq_ref, gk_ref, gv_ref, gr_ref, gf_ref, gb_ref, ym_ref):
    _inproj_rows(slice(0, x_ref.shape[0]), x_ref, mod_ref, g_ref, w_ref, wt_ref, qkg_ref, cos_ref, sin_ref,
                 gw_ref, gbias_ref, ws_ref, bs_ref, gmg_ref,
                 q_ref, k_ref, vt_ref, gq_ref, gk_ref, gv_ref, gr_ref, gf_ref, gb_ref, ym_ref)


def _inproj_rows(rs, x_ref, mod_ref, g_ref, w_ref, wt_ref, qkg_ref, cos_ref, sin_ref, gw_ref, gbias_ref,
                 ws_ref, bs_ref, gmg_ref,
                 q_ref, k_ref, vt_ref, gq_ref, gk_ref, gv_ref, gr_ref, gf_ref, gb_ref, ym_ref):
    h = _norm_mod(x_ref[rs, :], g_ref[...], mod_ref, 1).astype(BF16)
    cos = cos_ref[rs, :]
    sin = sin_ref[rs, :]
    lane = lax.broadcasted_iota(jnp.int32, cos.shape, 1)
    lane_low = (lane & 63) < 32

    def proj(off, width):
        if off < OFF_MU:
            return _dot(h, w_ref[:, off:off + width])
        return _dot(h, wt_ref[:, off - OFF_MU:off - OFF_MU + width])

    scale = HEAD_DIM ** -0.5 * LOG2E
    zq = proj(OFF_AQ, ATTN_Q_W)
    for hh in range(ATTN_HEADS):
        sl = slice(hh * HEAD_DIM, (hh + 1) * HEAD_DIM)
        qh = _rope(_rms(zq[:, sl], qkg_ref[0:1, :]), cos, sin, lane_low)
        q_ref[rs, sl] = (qh * scale).astype(BF16)
    zk = proj(OFF_AK, ATTN_KV_W)
    for hh in range(ATTN_KV_HEADS):
        sl = slice(hh * HEAD_DIM, (hh + 1) * HEAD_DIM)
        k_ref[rs, sl] = _rope(_rms(zk[:, sl], qkg_ref[1:2, :]), cos, sin, lane_low).astype(BF16)
    vt_ref[0, :, rs] = proj(OFF_AV, ATTN_KV_W).T.astype(BF16)

    gq_ref[rs, :] = proj(OFF_GQ, GLA_K_W) * (GLA_DK ** -0.5)
    gk_ref[rs, :] = proj(OFF_GK, GLA_K_W)
    gv_ref[rs, :] = proj(OFF_GV, GLA_V_W).astype(BF16)
    gr_ref[rs, :] = proj(OFF_GR, GLA_V_W)
    lr = proj(OFF_LR, GLR_PAD).astype(BF16)
    logits = _dot(lr, gw_ref[...]) + gbias_ref[...]
    ld = _log_sigmoid(logits) * (1.0 / GLA_TAU)
    gf_ref[rs, :] = ld[:, :GLA_K_W]
    gb_ref[rs, :] = ld[:, GLA_K_W:]

    mu = proj(OFF_MU, GMLP_W)
    vn = _rms(proj(OFF_MV, GMLP_W), gmg_ref[...]).astype(BF16)
    for c in range((rs.stop - rs.start) // CHUNK):
        rows = slice(c * CHUNK, (c + 1) * CHUNK)
        out_rows = slice(rs.start + c * CHUNK, rs.start + (c + 1) * CHUNK)
        for gi in range(GMLP_GROUPS):
            cols = slice(gi * GMLP_GROUP_DIM, (gi + 1) * GMLP_GROUP_DIM)
            z = _dot(ws_ref[gi], vn[rows, cols]) + bs_ref[gi]
            ym_ref[out_rows, cols] = (mu[rows, cols] * z).astype(BF16)


def _inproj(xs, mod, g, w_head, w_tail, qk_g, cos_t, sin_t, gate_w_r, gate_b_r, ws, bs_b, gm_g, *, layer, mod_row,
            rope_row):
    t, d = xs.shape
    n_tiles = t // TM
    row = lambda i: (i, 0)
    const2 = lambda i: (0, 0)
    const3 = lambda i: (0, 0, 0)
    widths = [(ATTN_Q_W, BF16), (ATTN_KV_W, BF16), None, (GLA_K_W, F32), (GLA_K_W, F32),
              (GLA_V_W, BF16), (GLA_V_W, F32), (GLA_K_W, F32), (GLA_K_W, F32), (GMLP_W, BF16)]
    out_specs = [pl.BlockSpec((1, ATTN_KV_W, TM), lambda i: (i, 0, 0)) if w is None else pl.BlockSpec((TM, w[0]), row)
                 for w in widths]
    out_shape = [jax.ShapeDtypeStruct((n_tiles, ATTN_KV_W, TM), BF16) if w is None
                 else jax.ShapeDtypeStruct((t, w[0]), w[1]) for w in widths]
    return pl.pallas_call(
        _inproj_kernel,
        grid=(n_tiles,),
        in_specs=[
            pl.BlockSpec((TM, d), row),
            pl.BlockSpec((1, N_MOD, d), lambda i: (mod_row(i), 0, 0)),
            pl.BlockSpec((1, d), const2),
            pl.BlockSpec((None, d, OFF_MU), lambda i: (layer, 0, 0), pipeline_mode=pl.Buffered(1)),
            pl.BlockSpec((None, d, IN_W_R - OFF_MU), lambda i: (layer, 0, 0), pipeline_mode=pl.Buffered(1)),
            pl.BlockSpec((2, HEAD_DIM), const2),
            pl.BlockSpec((TM, HEAD_DIM), lambda i: (rope_row(i), 0)),
            pl.BlockSpec((TM, HEAD_DIM), lambda i: (rope_row(i), 0)),
            pl.BlockSpec((GLR_PAD, 2 * GLA_K_W), const2),
            pl.BlockSpec((1, 2 * GLA_K_W), const2),
            pl.BlockSpec((GMLP_GROUPS, CHUNK, CHUNK), const3),
            pl.BlockSpec((GMLP_GROUPS, CHUNK, GMLP_GROUP_DIM), const3),
            pl.BlockSpec((1, GMLP_W), const2),
        ],
        out_specs=out_specs,
        out_shape=out_shape,
        compiler_params=pltpu.CompilerParams(
            dimension_semantics=("parallel",), vmem_limit_bytes=VMEM_LIMIT),
        name="inproj",
    )(xs, mod, g.reshape(1, d), w_head, w_tail, qk_g, cos_t, sin_t, gate_w_r, gate_b_r, ws, bs_b,
      gm_g.reshape(1, GMLP_W))


def _attn_kernel(*refs, n_lat_tiles):
    if n_lat_tiles:
        q_ref, kc_ref, vtc_ref, kl_ref, vtl_ref, o_ref, s_ref, off_ref, jump_ref, m_ref, acc_ref = refs
    else:
        q_ref, kc_ref, vtc_ref, o_ref, m_ref, acc_ref = refs
    tq = q_ref.shape[0]
    q = jnp.concatenate([q_ref[:, g * HEAD_DIM:(g + 1) * HEAD_DIM] for g in range(ATTN_GROUP)], axis=0)

    def scores(k):
        return _dot_t(k, q)

    def with_ones(vt):
        return jnp.concatenate([vt, jnp.ones((SUM_ROWS, vt.shape[1]), BF16)], axis=0)

    def lat_keys(t):
        start = pl.multiple_of(t * TKA, TKA)
        return kl_ref[pl.ds(start, TKA), :]

    def lat_values(t):
        n = TKA // TK
        return jnp.concatenate([vtl_ref[n * t + u] for u in range(n)], axis=1) if n > 1 else vtl_ref[t]

    def ctx_tile():
        s = scores(kc_ref[...])
        m_new = jnp.max(s, axis=0, keepdims=True)
        m_ref[...] = m_new
        acc_ref[...] = _dot(with_ones(vtc_ref[0]), jnp.exp2(s - m_new).astype(BF16))

    def lagged_tile(t, carry):
        c = m_ref[...]
        s = scores(lat_keys(t))
        tmax = jnp.max(s, axis=0, keepdims=True)
        p = jnp.exp2(s - c)
        alpha = jnp.exp2(off_ref[...] - c)
        acc_ref[...] = alpha * acc_ref[...] + _dot(with_ones(lat_values(t)), p.astype(BF16))
        off_ref[...] = c
        jump_ref[...] = jnp.maximum(jump_ref[...], tmax - c)
        m_ref[...] = jnp.maximum(c, tmax)
        return carry

    def exact_tile(t, carry):
        s_ref[...] = scores(lat_keys(t))
        m_old = m_ref[...]
        m_new = jnp.maximum(m_old, jnp.max(s_ref[...], axis=0, keepdims=True))
        alpha = jnp.exp2(m_old - m_new)
        p = jnp.exp2(s_ref[...] - m_new)
        acc_ref[...] = alpha * acc_ref[...] + _dot(with_ones(lat_values(t)), p.astype(BF16))
        m_ref[...] = m_new
        return carry

    ctx_tile()
    if n_lat_tiles:
        off_ref[...] = m_ref[...]
        jump_ref[...] = jnp.zeros_like(jump_ref)
        lax.fori_loop(0, n_lat_tiles, lagged_tile, 0, unroll=ATTN_UNROLL if n_lat_tiles % ATTN_UNROLL == 0 else 1)

        @pl.when(jnp.max(jump_ref[...]) > MAX_SAFE_JUMP)
        def _():
            ctx_tile()
            lax.fori_loop(0, n_lat_tiles, exact_tile, 0)
    out_t = acc_ref[0:HEAD_DIM, :] / acc_ref[HEAD_DIM:HEAD_DIM + 1, :]
    for g in range(ATTN_GROUP):
        o_ref[:, g * HEAD_DIM:(g + 1) * HEAD_DIM] = out_t[:, g * tq:(g + 1) * tq].T.astype(BF16)


def _attention(q, k, vt, *, batch, n_lat, n_ctx, latent):
    gw = ATTN_GROUP * HEAD_DIM
    ctx_blk0 = (batch * n_lat) // n_ctx
    lat_tiles = (batch * n_lat) // TK
    per_tile = TK // n_ctx
    kc_spec = pl.BlockSpec((n_ctx, HEAD_DIM), lambda b, kh, i: (ctx_blk0 + b, kh))
    vtc_spec = pl.BlockSpec((1, HEAD_DIM, n_ctx), lambda b, kh, i: (lat_tiles + b // per_tile, kh, b % per_tile))
    if latent:
        tq = TQ
        nq = n_lat // tq
        n_lat_tiles = n_lat // TKA
        q_spec = pl.BlockSpec((tq, gw), lambda b, kh, i: (b * nq + i, kh))
        kl_spec = pl.BlockSpec((n_lat, HEAD_DIM), lambda b, kh, i: (b, kh))
        vtl_spec = pl.BlockSpec((n_lat // TK, HEAD_DIM, TK), lambda b, kh, i: (b, kh, 0))
        in_specs = [q_spec, kc_spec, vtc_spec, kl_spec, vtl_spec]
        args = (q, k, vt, k, vt)
        out_rows = batch * n_lat
    else:
        tq = n_ctx
        nq = 1
        n_lat_tiles = 0
        q_spec = pl.BlockSpec((tq, gw), lambda b, kh, i: (ctx_blk0 + b, kh))
        in_specs = [q_spec, kc_spec, vtc_spec]
        args = (q, k, vt)
        out_rows = batch * n_ctx
    cols = ATTN_GROUP * tq
    scratch = [pltpu.VMEM((1, cols), F32), pltpu.VMEM((HEAD_DIM + SUM_ROWS, cols), F32)]
    if latent:
        scratch = [pltpu.VMEM((TKA, cols), F32), pltpu.VMEM((1, cols), F32), pltpu.VMEM((1, cols), F32)] + scratch
    return pl.pallas_call(
        functools.partial(_attn_kernel, n_lat_tiles=n_lat_tiles),
        grid=(batch, ATTN_KV_HEADS, nq),
        in_specs=in_specs,
        out_specs=pl.BlockSpec((tq, gw), lambda b, kh, i: (b * nq + i, kh)),
        out_shape=jax.ShapeDtypeStruct((out_rows, ATTN_Q_W), BF16),
        scratch_shapes=scratch,
        compiler_params=pltpu.CompilerParams(
            dimension_semantics=("parallel", "parallel", "arbitrary"), vmem_limit_bytes=VMEM_LIMIT),
        name="attn_lat" if latent else "attn_ctx",
    )(*args)


def _gla_consts():
    idx = np.arange(CHUNK)
    tri = (idx[None, :] <= idx[:, None]).astype(np.float32)
    mats_f, mats_b = [tri], [tri.T]
    for lvl in range(1, N_LEVELS + 1):
        s = (2 * CHUNK) >> lvl
        base = (idx // s) * s
        mats_f.append(tri[base + s // 2 - 1])
        mats_b.append(tri.T[base + s // 2])
    cm = np.stack([np.concatenate(mats_f, 0), np.concatenate(mats_b, 0)])
    cm = np.concatenate([cm, cm], axis=-1)
    x = idx[:, None] ^ idx[None, :]
    hb = np.floor(np.log2(np.maximum(x, 1))).astype(np.int32)
    lv = np.where(x == 0, 0, N_LEVELS - hb)
    lv_f = np.where(idx[:, None] >= idx[None, :], lv, -1)
    lv_b = np.where(idx[:, None] <= idx[None, :], lv, -1)
    return cm, np.stack([lv_f, lv_b]).astype(np.int32)


def _gla_chunk(cm_ref, lv_ref, q_ref, k_ref, v_ref, g_ref, o_ref, st_ref, d, rows):
    g = g_ref[rows, :]
    g_hi = g.astype(BF16)
    g_lo = (g - g_hi.astype(F32)).astype(BF16)
    cums = _dot(cm_ref[d], jnp.concatenate([g_hi, g_lo], axis=0))
    cum = cums[0:CHUNK]
    q = q_ref[rows, :]
    k = k_ref[rows, :]
    lv = lv_ref[d]
    last = CHUNK - 1 if d == 0 else 0
    tail = cum[last:last + 1, :]
    lane = lax.broadcasted_iota(jnp.int32, (CHUNK, LANES), 1)
    low = lane < GLA_DK

    qs = [q.astype(BF16)]
    ks = [k.astype(BF16)]
    for lvl in range(1, N_LEVELS + 1):
        decay = jnp.exp(-jnp.abs(cum - cums[lvl * CHUNK:(lvl + 1) * CHUNK]))
        qs.append((q * decay).astype(BF16))
        ks.append((k * decay).astype(BF16))
    q_in = (q * jnp.exp(cum)).astype(BF16)
    k_out = (k * jnp.exp(tail - cum)).astype(BF16)
    zero = jnp.zeros((CHUNK, LANES), BF16)
    lv2 = jnp.concatenate([lv, lv], axis=1)

    def per_head_rows(x):
        return jnp.concatenate([jnp.where(low, x, zero), jnp.where(low, zero, x)], axis=0)

    for p in range(GLA_HEADS // 2):
        pl_sl = slice(p * LANES, (p + 1) * LANES)
        pv_sl = slice(2 * p * GLA_DV, 2 * (p + 1) * GLA_DV)
        st = st_ref[d, p]
        a = jnp.zeros((CHUNK, 2 * CHUNK), F32)
        for lvl in range(N_LEVELS + 1):
            a = jnp.where(lv2 == lvl, _dot_t(qs[lvl][:, pl_sl], per_head_rows(ks[lvl][:, pl_sl])), a)
        v2 = v_ref[rows, pv_sl]
        zv = jnp.zeros((CHUNK, GLA_DV), BF16)
        v_bd = jnp.concatenate([jnp.concatenate([v2[:, :GLA_DV], zv], axis=1),
                                jnp.concatenate([zv, v2[:, GLA_DV:]], axis=1)], axis=0)
        inter = _dot_t(per_head_rows(q_in[:, pl_sl]), st.astype(BF16))
        o_ref[rows, pv_sl] = (_dot(a.astype(BF16), v_bd)
                              + jnp.concatenate([inter[:CHUNK], inter[CHUNK:]], axis=1))
        upd = _tdot(v2, k_out[:, pl_sl])
        st_ref[d, p] = st * jnp.exp(tail[:, pl_sl]) + jnp.where(low, upd[:GLA_DV], upd[GLA_DV:])


def _gla_kernel(cm_ref, lv_ref, qf, kf, vf, gf, qb, kb, vb, gb, of_ref, ob_ref, st_ref):
    @pl.when(pl.program_id(1) == 0)
    def _():
        st_ref[...] = jnp.zeros_like(st_ref)

    n = qf.shape[0] // CHUNK
    for c in range(n):
        _gla_chunk(cm_ref, lv_ref, qf, kf, vf, gf, of_ref, st_ref, 0, slice(c * CHUNK, (c + 1) * CHUNK))
        cb = n - 1 - c
        _gla_chunk(cm_ref, lv_ref, qb, kb, vb, gb, ob_ref, st_ref, 1, slice(cb * CHUNK, (cb + 1) * CHUNK))


def _gla(gq, gk, gv, gf, gb, *, batch, n_lat, n_ctx):
    t = gq.shape[0]
    rows = GLA_STEP_CHUNKS * CHUNK
    cl, cc = n_lat // rows, n_ctx // rows
    ctx0 = batch * cl
    cm_np, lv_np = _gla_consts()
    cm = jnp.asarray(cm_np, BF16)
    lv = jnp.asarray(lv_np)

    def fwd(b, s):
        return (jnp.where(s < cc, ctx0 + b * cc + s, b * cl + s - cc), 0)

    def bwd(b, s):
        return (jnp.where(s < cc, ctx0 + b * cc + (cc - 1 - s), b * cl + (cl - 1 - (s - cc))), 0)

    def specs(m):
        return [pl.BlockSpec((rows, GLA_K_W), m), pl.BlockSpec((rows, GLA_K_W), m),
                pl.BlockSpec((rows, GLA_V_W), m), pl.BlockSpec((rows, GLA_K_W), m)]

    return pl.pallas_call(
        _gla_kernel,
        grid=(batch, cc + cl),
        in_specs=[pl.BlockSpec(cm.shape, lambda b, s: (0, 0, 0)), pl.BlockSpec(lv.shape, lambda b, s: (0, 0, 0))]
        + specs(fwd) + specs(bwd),
        out_specs=[pl.BlockSpec((rows, GLA_V_W), fwd), pl.BlockSpec((rows, GLA_V_W), bwd)],
        out_shape=[jax.ShapeDtypeStruct((t, GLA_V_W), F32)] * 2,
        scratch_shapes=[pltpu.VMEM((2, GLA_HEADS // 2, GLA_DV, LANES), F32)],
        compiler_params=pltpu.CompilerParams(
            dimension_semantics=("parallel", "arbitrary"), vmem_limit_bytes=VMEM_LIMIT),
        name="gla",
    )(cm, lv, gq, gk, gv, gf, gq, gk, gv, gb)


def _outproj_kernel(*refs, split_at):
    if split_at is None:
        x_ref, mod_ref, att_ref, of_ref, ob_ref, gr_ref, ym_ref, gg_ref, w_ref, o_ref = refs
        att = att_ref[...]
    else:
        x_ref, mod_ref, att_ref, attc_ref, of_ref, ob_ref, gr_ref, ym_ref, gg_ref, w_ref, o_ref = refs
        att = jnp.where(pl.program_id(0) >= split_at, attc_ref[...], att_ref[...])
    o = of_ref[...] + ob_ref[...]
    r = gr_ref[...]
    mix = [att]
    for hh in range(GLA_HEADS):
        sl = slice(hh * GLA_DV, (hh + 1) * GLA_DV)
        mix.append((_rms(o[:, sl], gg_ref[:, sl]) * _silu(r[:, sl])).astype(BF16))
    mix.append(ym_ref[...])
    y = _dot(jnp.concatenate(mix, axis=1), w_ref[...])
    o_ref[...] = x_ref[...] + mod_ref[0, 5:6, :] * y


def _outproj(xs, mod, att, att_c, o_f, o_b, gr, ym, gla_g, w_out, *, layer, n_tiles, mod_row):
    t, d = xs.shape
    row = lambda i: (i, 0)
    const2 = lambda i: (0, 0)
    if att_c is None:
        split_at = None
        att_specs = [pl.BlockSpec((TM, ATTN_Q_W), row)]
        att_args = (att,)
    else:
        split_at = att.shape[0] // TM
        att_specs = [pl.BlockSpec((TM, ATTN_Q_W), lambda i: (jnp.minimum(i, split_at - 1), 0)),
                     pl.BlockSpec((TM, ATTN_Q_W), lambda i: (jnp.maximum(i - split_at, 0), 0))]
        att_args = (att, att_c)
    return pl.pallas_call(
        functools.partial(_outproj_kernel, split_at=split_at),
        grid=(n_tiles,),
        in_specs=[
            pl.BlockSpec((TM, d), row),
            pl.BlockSpec((1, N_MOD, d), lambda i: (mod_row(i), 0, 0))] + att_specs + [
            pl.BlockSpec((TM, GLA_V_W), row),
            pl.BlockSpec((TM, GLA_V_W), row),
            pl.BlockSpec((TM, GLA_V_W), row),
            pl.BlockSpec((TM, GMLP_W), row),
            pl.BlockSpec((1, GLA_V_W), const2),
            pl.BlockSpec((None,) + w_out.shape[1:], lambda i: (layer, 0, 0), pipeline_mode=pl.Buffered(1)),
        ],
        out_specs=pl.BlockSpec((TM, d), row),
        out_shape=jax.ShapeDtypeStruct((n_tiles * TM, d), F32),
        compiler_params=pltpu.CompilerParams(
            dimension_semantics=("parallel",), vmem_limit_bytes=VMEM_LIMIT),
        name="outproj",
    )(xs, mod, *att_args, o_f, o_b, gr, ym, gla_g.reshape(1, GLA_V_W), w_out)


def _rope_tables(n_lat):
    rows = n_lat // GRID_W
    nf = HEAD_DIM // 4
    inv = ROPE_THETA ** (-jnp.arange(nf, dtype=F32) / nf)
    ar = jnp.arange(rows, dtype=F32)[:, None] * inv
    ac = jnp.arange(GRID_W, dtype=F32)[:, None] * inv
    per_row = lambda z: jnp.repeat(z, GRID_W, axis=0)
    per_col = lambda z: jnp.tile(z, (rows, 1))
    cos_r, sin_r, cos_c, sin_c = per_row(jnp.cos(ar)), per_row(jnp.sin(ar)), per_col(jnp.cos(ac)), per_col(jnp.sin(ac))
    cos = jnp.concatenate([cos_r, cos_r, cos_c, cos_c], axis=-1)
    sin = jnp.concatenate([-sin_r, sin_r, -sin_c, sin_c], axis=-1)
    cos = jnp.concatenate([cos, jnp.ones((TM, HEAD_DIM), F32)], axis=0)
    sin = jnp.concatenate([sin, jnp.zeros((TM, HEAD_DIM), F32)], axis=0)
    return cos, sin


def kernel(x, c, ctx, c_ctx, mod_w, mod_b, norm_g, ffn1_w_gu, ffn1_w_down, ffn2_w_gu, ffn2_w_down, w_in, w_out,
           qk_norm_g, gla_gate_w, gla_gate_b, gla_norm_g, gmlp_w_s, gmlp_b_s, gmlp_norm_g, final_norm_g):
    batch, n_lat, d = x.shape
    n_ctx = ctx.shape[1]
    depth = mod_w.shape[0]
    assert n_lat % TM == 0 and (batch * n_ctx) % TM == 0 and n_lat % TKA == 0 and n_lat % TQ == 0
    assert n_ctx % (GLA_STEP_CHUNKS * CHUNK) == 0 and n_lat % n_ctx == 0 and batch + 1 <= 8 and TM == TK and TK % n_ctx == 0
    lat_tiles = batch * n_lat // TM
    all_tiles = lat_tiles + batch * n_ctx // TM
    tiles_per_batch = n_lat // TM

    def mod_row(i):
        return jnp.minimum(i // tiles_per_batch, batch)

    tm_ffn = TM_FFN if n_lat % TM_FFN == 0 else TM
    ffn_lat_tiles = batch * n_lat // tm_ffn
    ffn_all_tiles = ffn_lat_tiles + pl.cdiv(batch * n_ctx, tm_ffn)

    def mod_row_ffn(i):
        return jnp.minimum(i // (n_lat // tm_ffn), batch)

    def rope_row(i):
        return jnp.where(i < lat_tiles, i % tiles_per_batch, tiles_per_batch)

    cc = jnp.zeros((8, d), F32).at[:batch].set(c).at[batch].set(c_ctx)
    mod_all = _modulation(cc, mod_w, mod_b).reshape(depth, 8, N_MOD, d)
    cos_t, sin_t = _rope_tables(n_lat)
    offs = np.cumsum([0, ATTN_Q_W, ATTN_KV_W, ATTN_KV_W, GLA_K_W, GLA_K_W, GLA_V_W, GLA_V_W, 2 * GLA_GATE_RANK,
                      GMLP_W, GMLP_W])
    lr0, lr1 = int(offs[7]), int(offs[8])
    w_head = w_in[:, :, :lr0].astype(BF16)
    w_tail = jnp.concatenate(
        [w_in[:, :, lr1:], w_in[:, :, lr0:lr1],
         jnp.zeros((depth, d, GLR_PAD - 2 * GLA_GATE_RANK), w_in.dtype)], axis=-1).astype(BF16)
    w_out_b = w_out.astype(BF16)
    f1gu, f1d = ffn1_w_gu.astype(BF16), ffn1_w_down.astype(BF16)
    f2gu, f2d = ffn2_w_gu.astype(BF16), ffn2_w_down.astype(BF16)
    xs = x.reshape(batch * n_lat, d)
    xc = ctx.reshape(batch * n_ctx, d)

    for l in range(depth):
        last = l == depth - 1
        mod = mod_all[l]
        gate_w_r = jnp.zeros((GLR_PAD, 2 * GLA_K_W), F32)
        gate_w_r = gate_w_r.at[:GLA_GATE_RANK, :GLA_K_W].set(gla_gate_w[l, 0])
        gate_w_r = gate_w_r.at[GLA_GATE_RANK:2 * GLA_GATE_RANK, GLA_K_W:].set(gla_gate_w[l, 1]).astype(BF16)
        gate_b_r = gla_gate_b[l].reshape(1, 2 * GLA_K_W)
        bs_b = jnp.broadcast_to(gmlp_b_s[l][..., None], gmlp_b_s.shape[1:] + (GMLP_GROUP_DIM,))

        if xc is not None:
            xs = _ffn(xs, mod, norm_g[l, 0], f1gu, f1d, final_norm_g, layer=l, sub=0, tm=TM,
                      n_tiles=all_tiles, mod_row=mod_row, final=False, xc=xc)
        else:
            xs = _ffn(xs, mod, norm_g[l, 0], f1gu, f1d, final_norm_g, layer=l, sub=0, tm=tm_ffn,
                      n_tiles=ffn_all_tiles, mod_row=mod_row_ffn, final=False)
        xc = None
        q, k, vt, gq, gk, gv, gr, gf, gb, ym = _inproj(
            xs, mod, norm_g[l, 1], w_head, w_tail, qk_norm_g[l], cos_t, sin_t, gate_w_r, gate_b_r,
            gmlp_w_s[l].astype(BF16), bs_b, gmlp_norm_g[l], layer=l, mod_row=mod_row, rope_row=rope_row)
        att = _attention(q, k, vt, batch=batch, n_lat=n_lat, n_ctx=n_ctx, latent=True)
        o_f, o_b = _gla(gq, gk, gv, gf, gb, batch=batch, n_lat=n_lat, n_ctx=n_ctx)
        att_c = None if last else _attention(q, k, vt, batch=batch, n_lat=n_lat, n_ctx=n_ctx, latent=False)
        n_tiles = lat_tiles if last else all_tiles
        xs = _outproj(xs, mod, att, att_c, o_f, o_b, gr, ym, gla_norm_g[l], w_out_b,
                      layer=l, n_tiles=n_tiles, mod_row=mod_row)
        xs = _ffn(xs, mod, norm_g[l, 2], f2gu, f2d, final_norm_g, layer=l, sub=2, tm=tm_ffn,
                  n_tiles=ffn_lat_tiles if last else ffn_all_tiles, mod_row=mod_row_ffn, final=last)
    return xs.reshape(batch, n_lat, d)
```

```python
import functools

import numpy as np
import jax
import jax.numpy as jnp
from jax import lax
from jax.experimental import pallas as pl
from jax.experimental.pallas import tpu as pltpu

F32 = jnp.float32
BF16 = jnp.bfloat16

EPS = 1e-6
N_MOD = 9
HEAD_DIM = 128
ATTN_HEADS = 8
ATTN_KV_HEADS = 2
ATTN_GROUP = ATTN_HEADS // ATTN_KV_HEADS
ROPE_THETA = 10000.0
GRID_W = 64
GLA_HEADS = 4
GLA_DK = 64
GLA_DV = 128
GLA_GATE_RANK = 16
GLA_TAU = 16.0
LOG2E = 1.4426950408889634
CHUNK = 128
GMLP_GROUPS = 4
GMLP_GROUP_DIM = 128

ATTN_Q_W = ATTN_HEADS * HEAD_DIM
ATTN_KV_W = ATTN_KV_HEADS * HEAD_DIM
GLA_K_W = GLA_HEADS * GLA_DK
GLA_V_W = GLA_HEADS * GLA_DV
GMLP_W = GMLP_GROUPS * GMLP_GROUP_DIM
LANES = 128
GLR_PAD = LANES
N_LEVELS = 7

OFF_AQ = 0
OFF_AK = OFF_AQ + ATTN_Q_W
OFF_AV = OFF_AK + ATTN_KV_W
OFF_GQ = OFF_AV + ATTN_KV_W
OFF_GK = OFF_GQ + GLA_K_W
OFF_GV = OFF_GK + GLA_K_W
OFF_GR = OFF_GV + GLA_V_W
OFF_MU = OFF_GR + GLA_V_W
OFF_MV = OFF_MU + GMLP_W
OFF_LR = OFF_MV + GMLP_W
IN_W_R = OFF_LR + GLR_PAD

TM = 512
TF = 512
TM_FFN = 512
FFN_SUB = 512
TQ = 256
TK = 512
TKA = 512
GLA_STEP_CHUNKS = 2
SUM_ROWS = 16
ATTN_UNROLL = 32
MAX_SAFE_JUMP = 60.0
MOD_TN = 2048
VMEM_LIMIT = 56 * 1024 * 1024


def _sigmoid(x):
    return 1.0 / (1.0 + jnp.exp(-x))


def _silu(x):
    return x * _sigmoid(x)


def _rms(x, g):
    ms = jnp.mean(x * x, axis=-1, keepdims=True)
    return x * lax.rsqrt(ms + EPS) * g


def _norm_mod(x, g, mod_ref, i):
    ms = jnp.mean(x * x, axis=-1, keepdims=True)
    gain = g * (1.0 + mod_ref[0, 3 * i + 1:3 * i + 2, :])
    return x * lax.rsqrt(ms + EPS) * gain + mod_ref[0, 3 * i:3 * i + 1, :]


def _dot(a, b):
    return jnp.dot(a, b, preferred_element_type=F32)


def _dot_t(a, b):
    return lax.dot_general(a, b, (((1,), (1,)), ((), ())), preferred_element_type=F32)


def _tdot(a, b):
    return lax.dot_general(a, b, (((0,), (0,)), ((), ())), preferred_element_type=F32)


def _mod_kernel(c_ref, w_ref, b_ref, o_ref):
    sc = _silu(c_ref[...]).astype(BF16)
    o_ref[0] = _dot(sc, w_ref[0].astype(BF16)) + b_ref[0]


def _modulation(cc, mod_w, mod_b):
    depth, d, n = mod_w.shape
    return pl.pallas_call(
        _mod_kernel,
        grid=(depth, n // MOD_TN),
        in_specs=[
            pl.BlockSpec((8, d), lambda l, j: (0, 0)),
            pl.BlockSpec((1, d, MOD_TN), lambda l, j: (l, 0, j)),
            pl.BlockSpec((1, 1, MOD_TN), lambda l, j: (l, 0, j)),
        ],
        out_specs=pl.BlockSpec((1, 8, MOD_TN), lambda l, j: (l, 0, j)),
        out_shape=jax.ShapeDtypeStruct((depth, 8, n), F32),
        compiler_params=pltpu.CompilerParams(
            dimension_semantics=("parallel", "parallel"), vmem_limit_bytes=VMEM_LIMIT),
        name="modulation",
    )(cc, mod_w, mod_b.reshape(depth, 1, n))


def _ffn_kernel(*refs, sub, final, split_at):
    if split_at is None:
        x_ref, mod_ref, g_ref, wg_ref, wu_ref, wd_ref, fg_ref, o_ref, h_ref = refs
        sources = [(None, x_ref)]
    else:
        x_ref, xc_ref, mod_ref, g_ref, wg_ref, wu_ref, wd_ref, fg_ref, o_ref, h_ref = refs
        is_ctx = pl.program_id(0) >= split_at
        sources = [(jnp.logical_not(is_ctx), x_ref), (is_ctx, xc_ref)]
    j = pl.program_id(1)
    last_j = pl.num_programs(1) - 1
    subs = [slice(r, r + FFN_SUB) for r in range(0, o_ref.shape[0], FFN_SUB)]

    def prologue(src_ref):
        for rs in subs:
            h_ref[rs, :] = _norm_mod(src_ref[rs, :], g_ref[...], mod_ref, sub).astype(BF16)

    def epilogue(src_ref):
        for rs in subs:
            y = src_ref[rs, :] + mod_ref[0, 3 * sub + 2:3 * sub + 3, :] * (0.5 * o_ref[rs, :])
            if final:
                y = _rms(y, fg_ref[...])
            o_ref[rs, :] = y

    for cond, src_ref in sources:
        first = j == 0
        pl.when(first if cond is None else jnp.logical_and(first, cond))(functools.partial(prologue, src_ref))

    for rs in subs:
        h = h_ref[rs, :]
        hw = TF // 2
        parts = [(_silu(_dot(h, wg_ref[:, s0:s0 + hw])) * _dot(h, wu_ref[:, s0:s0 + hw])).astype(BF16)
                 for s0 in range(0, TF, hw)]
        for c in range(0, o_ref.shape[1], TF):
            d = _dot(parts[0], wd_ref[0:hw, c:c + TF]) + _dot(parts[1], wd_ref[hw:TF, c:c + TF])
            o_ref[rs, c:c + TF] = jnp.where(j == 0, d, o_ref[rs, c:c + TF] + d)

    for cond, src_ref in sources:
        last = j == last_j
        pl.when(last if cond is None else jnp.logical_and(last, cond))(functools.partial(epilogue, src_ref))


def _ffn(xs, mod, g, w_gu, w_down, final_g, *, layer, sub, tm, n_tiles, mod_row, final, xc=None):
    t, d = xs.shape
    f = w_down.shape[1]
    nf = f // TF
    rows_out = t if xc is None else t + xc.shape[0]
    rows_out = min(rows_out, n_tiles * tm)
    if xc is None:
        split_at = None
        x_specs = [pl.BlockSpec((tm, d), lambda i, j: (i, 0))]
        x_args = (xs,)
    else:
        split_at = t // tm
        x_specs = [pl.BlockSpec((tm, d), lambda i, j: (jnp.minimum(i, split_at - 1), 0)),
                   pl.BlockSpec((tm, d), lambda i, j: (jnp.maximum(i - split_at, 0), 0))]
        x_args = (xs, xc)
    kern = functools.partial(_ffn_kernel, sub=sub, final=final, split_at=split_at)
    return pl.pallas_call(
        kern,
        grid=(n_tiles, nf),
        in_specs=x_specs + [
            pl.BlockSpec((1, N_MOD, d), lambda i, j: (mod_row(i), 0, 0)),
            pl.BlockSpec((1, d), lambda i, j: (0, 0)),
            pl.BlockSpec((None, d, TF), lambda i, j: (layer, 0, j)),
            pl.BlockSpec((None, d, TF), lambda i, j: (layer, 0, j + nf)),
            pl.BlockSpec((None, TF, d), lambda i, j: (layer, j, 0)),
            pl.BlockSpec((1, d), lambda i, j: (0, 0)),
        ],
        out_specs=pl.BlockSpec((tm, d), lambda i, j: (i, 0)),
        out_shape=jax.ShapeDtypeStruct((rows_out, d), F32),
        scratch_shapes=[pltpu.VMEM((tm, d), BF16)],
        compiler_params=pltpu.CompilerParams(
            dimension_semantics=("parallel", "arbitrary"), vmem_limit_bytes=VMEM_LIMIT),
        name="ffn_final" if final else "ffn",
    )(*x_args, mod, g.reshape(1, d), w_gu, w_gu, w_down, final_g.reshape(1, d))


def _rope(x, cos, sin_signed, lane_low):
    partner = jnp.where(lane_low, pltpu.roll(x, LANES - 32, 1), pltpu.roll(x, 32, 1))
    return x * cos + partner * sin_signed


def _log_sigmoid(x):
    return jnp.minimum(x, 0.0) - jnp.log(1.0 + jnp.exp(-jnp.abs(x)))


def _inproj_kernel(x_ref, mod_ref, g_ref, w_ref, wt_ref, qkg_ref, cos_ref, sin_ref, gw_ref, gbias_ref,
                   ws_ref, bs_ref, gmg_ref,
                   q_ref, k_ref, vt_ref, gq_ref, gk_ref, gv_ref, gr_ref, gf_ref, gb_ref, ym_ref):
    _inproj_rows(slice(0, x_ref.shape[0]), x_ref, mod_ref, g_ref, w_ref, wt_ref, qkg_ref, cos_ref, sin_ref,
                 gw_ref, gbias_ref, ws_ref, bs_ref, gmg_ref,
                 q_ref, k_ref, vt_ref, gq_ref, gk_ref, gv_ref, gr_ref, gf_ref, gb_ref, ym_ref)


def _inproj_rows(rs, x_ref, mod_ref, g_ref, w_ref, wt_ref, qkg_ref, cos_ref, sin_ref, gw_ref, gbias_ref,
                 ws_ref, bs_ref, gmg_ref,
                 q_ref, k_ref, vt_ref, gq_ref, gk_ref, gv_ref, gr_ref, gf_ref, gb_ref, ym_ref):
    h = _norm_mod(x_ref[rs, :], g_ref[...], mod_ref, 1).astype(BF16)
    cos = cos_ref[rs, :]
    sin = sin_ref[rs, :]
    lane = lax.broadcasted_iota(jnp.int32, cos.shape, 1)
    lane_low = (lane & 63) < 32

    def proj(off, width):
        if off < OFF_MU:
            return _dot(h, w_ref[:, off:off + width])
        return _dot(h, wt_ref[:, off - OFF_MU:off - OFF_MU + width])

    scale = HEAD_DIM ** -0.5 * LOG2E
    zq = proj(OFF_AQ, ATTN_Q_W)
    for hh in range(ATTN_HEADS):
        sl = slice(hh * HEAD_DIM, (hh + 1) * HEAD_DIM)
        qh = _rope(_rms(zq[:, sl], qkg_ref[0:1, :]), cos, sin, lane_low)
        q_ref[rs, sl] = (qh * scale).astype(BF16)
    zk = proj(OFF_AK, ATTN_KV_W)
    for hh in range(ATTN_KV_HEADS):
        sl = slice(hh * HEAD_DIM, (hh + 1) * HEAD_DIM)
        k_ref[rs, sl] = _rope(_rms(zk[:, sl], qkg_ref[1:2, :]), cos, sin, lane_low).astype(BF16)
    vt_ref[0, :, rs] = proj(OFF_AV, ATTN_KV_W).T.astype(BF16)

    gq_ref[rs, :] = proj(OFF_GQ, GLA_K_W) * (GLA_DK ** -0.5)
    gk_ref[rs, :] = proj(OFF_GK, GLA_K_W)
    gv_ref[rs, :] = proj(OFF_GV, GLA_V_W).astype(BF16)
    gr_ref[rs, :] = proj(OFF_GR, GLA_V_W)
    lr = proj(OFF_LR, GLR_PAD).astype(BF16)
    logits = _dot(lr, gw_ref[...]) + gbias_ref[...]
    ld = _log_sigmoid(logits) * (1.0 / GLA_TAU)
    gf_ref[rs, :] = ld[:, :GLA_K_W]
    gb_ref[rs, :] = ld[:, GLA_K_W:]

    mu = proj(OFF_MU, GMLP_W)
    vn = _rms(proj(OFF_MV, GMLP_W), gmg_ref[...]).astype(BF16)
    for c in range((rs.stop - rs.start) // CHUNK):
        rows = slice(c * CHUNK, (c + 1) * CHUNK)
        out_rows = slice(rs.start + c * CHUNK, rs.start + (c + 1) * CHUNK)
        for gi in range(GMLP_GROUPS):
            cols = slice(gi * GMLP_GROUP_DIM, (gi + 1) * GMLP_GROUP_DIM)
            z = _dot(ws_ref[gi], vn[rows, cols]) + bs_ref[gi]
            ym_ref[out_rows, cols] = (mu[rows, cols] * z).astype(BF16)


def _inproj(xs, mod, g, w_head, w_tail, qk_g, cos_t, sin_t, gate_w_r, gate_b_r, ws, bs_b, gm_g, *, layer, mod_row,
            rope_row):
    t, d = xs.shape
    n_tiles = t // TM
    row = lambda i: (i, 0)
    const2 = lambda i: (0, 0)
    const3 = lambda i: (0, 0, 0)
    widths = [(ATTN_Q_W, BF16), (ATTN_KV_W, BF16), None, (GLA_K_W, F32), (GLA_K_W, F32),
              (GLA_V_W, BF16), (GLA_V_W, F32), (GLA_K_W, F32), (GLA_K_W, F32), (GMLP_W, BF16)]
    out_specs = [pl.BlockSpec((1, ATTN_KV_W, TM), lambda i: (i, 0, 0)) if w is None else pl.BlockSpec((TM, w[0]), row)
                 for w in widths]
    out_shape = [jax.ShapeDtypeStruct((n_tiles, ATTN_KV_W, TM), BF16) if w is None
                 else jax.ShapeDtypeStruct((t, w[0]), w[1]) for w in widths]
    return pl.pallas_call(
        _inproj_kernel,
        grid=(n_tiles,),
        in_specs=[
            pl.BlockSpec((TM, d), row),
            pl.BlockSpec((1, N_MOD, d), lambda i: (mod_row(i), 0, 0)),
            pl.BlockSpec((1, d), const2),
            pl.BlockSpec((None, d, OFF_MU), lambda i: (layer, 0, 0), pipeline_mode=pl.Buffered(1)),
            pl.BlockSpec((None, d, IN_W_R - OFF_MU), lambda i: (layer, 0, 0), pipeline_mode=pl.Buffered(1)),
            pl.BlockSpec((2, HEAD_DIM), const2),
            pl.BlockSpec((TM, HEAD_DIM), lambda i: (rope_row(i), 0)),
            pl.BlockSpec((TM, HEAD_DIM), lambda i: (rope_row(i), 0)),
            pl.BlockSpec((GLR_PAD, 2 * GLA_K_W), const2),
            pl.BlockSpec((1, 2 * GLA_K_W), const2),
            pl.BlockSpec((GMLP_GROUPS, CHUNK, CHUNK), const3),
            pl.BlockSpec((GMLP_GROUPS, CHUNK, GMLP_GROUP_DIM), const3),
            pl.BlockSpec((1, GMLP_W), const2),
        ],
        out_specs=out_specs,
        out_shape=out_shape,
        compiler_params=pltpu.CompilerParams(
            dimension_semantics=("parallel",), vmem_limit_bytes=VMEM_LIMIT),
        name="inproj",
    )(xs, mod, g.reshape(1, d), w_head, w_tail, qk_g, cos_t, sin_t, gate_w_r, gate_b_r, ws, bs_b,
      gm_g.reshape(1, GMLP_W))


def _attn_kernel(*refs, n_lat_tiles):
    if n_lat_tiles:
        q_ref, kc_ref, vtc_ref, kl_ref, vtl_ref, o_ref, s_ref, off_ref, jump_ref, m_ref, acc_ref = refs
    else:
        q_ref, kc_ref, vtc_ref, o_ref, m_ref, acc_ref = refs
    tq = q_ref.shape[0]
    q = jnp.concatenate([q_ref[:, g * HEAD_DIM:(g + 1) * HEAD_DIM] for g in range(ATTN_GROUP)], axis=0)

    def scores(k):
        return _dot_t(k, q)

    def with_ones(vt):
        return jnp.concatenate([vt, jnp.ones((SUM_ROWS, vt.shape[1]), BF16)], axis=0)

    def lat_keys(t):
        start = pl.multiple_of(t * TKA, TKA)
        return kl_ref[pl.ds(start, TKA), :]

    def lat_values(t):
        n = TKA // TK
        return jnp.concatenate([vtl_ref[n * t + u] for u in range(n)], axis=1) if n > 1 else vtl_ref[t]

    def ctx_tile():
        s = scores(kc_ref[...])
        m_new = jnp.max(s, axis=0, keepdims=True)
        m_ref[...] = m_new
        acc_ref[...] = _dot(with_ones(vtc_ref[0]), jnp.exp2(s - m_new).astype(BF16))

    def lagged_tile(t, carry):
        c = m_ref[...]
        s = scores(lat_keys(t))
        tmax = jnp.max(s, axis=0, keepdims=True)
        p = jnp.exp2(s - c)
        alpha = jnp.exp2(off_ref[...] - c)
        acc_ref[...] = alpha * acc_ref[...] + _dot(with_ones(lat_values(t)), p.astype(BF16))
        off_ref[...] = c
        jump_ref[...] = jnp.maximum(jump_ref[...], tmax - c)
        m_ref[...] = jnp.maximum(c, tmax)
        return carry

    def exact_tile(t, carry):
        s_ref[...] = scores(lat_keys(t))
        m_old = m_ref[...]
        m_new = jnp.maximum(m_old, jnp.max(s_ref[...], axis=0, keepdims=True))
        alpha = jnp.exp2(m_old - m_new)
        p = jnp.exp2(s_ref[...] - m_new)
        acc_ref[...] = alpha * acc_ref[...] + _dot(with_ones(lat_values(t)), p.astype(BF16))
        m_ref[...] = m_new
        return carry

    ctx_tile()
    if n_lat_tiles:
        off_ref[...] = m_ref[...]
        jump_ref[...] = jnp.zeros_like(jump_ref)
        lax.fori_loop(0, n_lat_tiles, lagged_tile, 0, unroll=ATTN_UNROLL if n_lat_tiles % ATTN_UNROLL == 0 else 1)

        @pl.when(jnp.max(jump_ref[...]) > MAX_SAFE_JUMP)
        def _():
            ctx_tile()
            lax.fori_loop(0, n_lat_tiles, exact_tile, 0)
    out_t = acc_ref[0:HEAD_DIM, :] / acc_ref[HEAD_DIM:HEAD_DIM + 1, :]
    for g in range(ATTN_GROUP):
        o_ref[:, g * HEAD_DIM:(g + 1) * HEAD_DIM] = out_t[:, g * tq:(g + 1) * tq].T.astype(BF16)


def _attention(q, k, vt, *, batch, n_lat, n_ctx, latent):
    gw = ATTN_GROUP * HEAD_DIM
    ctx_blk0 = (batch * n_lat) // n_ctx
    lat_tiles = (batch * n_lat) // TK
    per_tile = TK // n_ctx
    kc_spec = pl.BlockSpec((n_ctx, HEAD_DIM), lambda b, kh, i: (ctx_blk0 + b, kh))
    vtc_spec = pl.BlockSpec((1, HEAD_DIM, n_ctx), lambda b, kh, i: (lat_tiles + b // per_tile, kh, b % per_tile))
    if latent:
        tq = TQ
        nq = n_lat // tq
        n_lat_tiles = n_lat // TKA
        q_spec = pl.BlockSpec((tq, gw), lambda b, kh, i: (b * nq + i, kh))
        kl_spec = pl.BlockSpec((n_lat, HEAD_DIM), lambda b, kh, i: (b, kh))
        vtl_spec = pl.BlockSpec((n_lat // TK, HEAD_DIM, TK), lambda b, kh, i: (b, kh, 0))
        in_specs = [q_spec, kc_spec, vtc_spec, kl_spec, vtl_spec]
        args = (q, k, vt, k, vt)
        out_rows = batch * n_lat
    else:
        tq = n_ctx
        nq = 1
        n_lat_tiles = 0
        q_spec = pl.BlockSpec((tq, gw), lambda b, kh, i: (ctx_blk0 + b, kh))
        in_specs = [q_spec, kc_spec, vtc_spec]
        args = (q, k, vt)
        out_rows = batch * n_ctx
    cols = ATTN_GROUP * tq
    scratch = [pltpu.VMEM((1, cols), F32), pltpu.VMEM((HEAD_DIM + SUM_ROWS, cols), F32)]
    if latent:
        scratch = [pltpu.VMEM((TKA, cols), F32), pltpu.VMEM((1, cols), F32), pltpu.VMEM((1, cols), F32)] + scratch
    return pl.pallas_call(
        functools.partial(_attn_kernel, n_lat_tiles=n_lat_tiles),
        grid=(batch, ATTN_KV_HEADS, nq),
        in_specs=in_specs,
        out_specs=pl.BlockSpec((tq, gw), lambda b, kh, i: (b * nq + i, kh)),
        out_shape=jax.ShapeDtypeStruct((out_rows, ATTN_Q_W), BF16),
        scratch_shapes=scratch,
        compiler_params=pltpu.CompilerParams(
            dimension_semantics=("parallel", "parallel", "arbitrary"), vmem_limit_bytes=VMEM_LIMIT),
        name="attn_lat" if latent else "attn_ctx",
    )(*args)


def _gla_consts():
    idx = np.arange(CHUNK)
    tri = (idx[None, :] <= idx[:, None]).astype(np.float32)
    mats_f, mats_b = [tri], [tri.T]
    for lvl in range(1, N_LEVELS + 1):
        s = (2 * CHUNK) >> lvl
        base = (idx // s) * s
        mats_f.append(tri[base + s // 2 - 1])
        mats_b.append(tri.T[base + s // 2])
    cm = np.stack([np.concatenate(mats_f, 0), np.concatenate(mats_b, 0)])
    cm = np.concatenate([cm, cm], axis=-1)
    x = idx[:, None] ^ idx[None, :]
    hb = np.floor(np.log2(np.maximum(x, 1))).astype(np.int32)
    lv = np.where(x == 0, 0, N_LEVELS - hb)
    lv_f = np.where(idx[:, None] >= idx[None, :], lv, -1)
    lv_b = np.where(idx[:, None] <= idx[None, :], lv, -1)
    return cm, np.stack([lv_f, lv_b]).astype(np.int32)


def _gla_chunk(cm_ref, lv_ref, q_ref, k_ref, v_ref, g_ref, o_ref, st_ref, d, rows):
    g = g_ref[rows, :]
    g_hi = g.astype(BF16)
    g_lo = (g - g_hi.astype(F32)).astype(BF16)
    cums = _dot(cm_ref[d], jnp.concatenate([g_hi, g_lo], axis=0))
    cum = cums[0:CHUNK]
    q = q_ref[rows, :]
    k = k_ref[rows, :]
    lv = lv_ref[d]
    last = CHUNK - 1 if d == 0 else 0
    tail = cum[last:last + 1, :]
    lane = lax.broadcasted_iota(jnp.int32, (CHUNK, LANES), 1)
    low = lane < GLA_DK

    qs = [q.astype(BF16)]
    ks = [k.astype(BF16)]
    for lvl in range(1, N_LEVELS + 1):
        decay = jnp.exp(-jnp.abs(cum - cums[lvl * CHUNK:(lvl + 1) * CHUNK]))
        qs.append((q * decay).astype(BF16))
        ks.append((k * decay).astype(BF16))
    q_in = (q * jnp.exp(cum)).astype(BF16)
    k_out = (k * jnp.exp(tail - cum)).astype(BF16)
    zero = jnp.zeros((CHUNK, LANES), BF16)
    lv2 = jnp.concatenate([lv, lv], axis=1)

    def per_head_rows(x):
        return jnp.concatenate([jnp.where(low, x, zero), jnp.where(low, zero, x)], axis=0)

    for p in range(GLA_HEADS // 2):
        pl_sl = slice(p * LANES, (p + 1) * LANES)
        pv_sl = slice(2 * p * GLA_DV, 2 * (p + 1) * GLA_DV)
        st = st_ref[d, p]
        a = jnp.zeros((CHUNK, 2 * CHUNK), F32)
        for lvl in range(N_LEVELS + 1):
            a = jnp.where(lv2 == lvl, _dot_t(qs[lvl][:, pl_sl], per_head_rows(ks[lvl][:, pl_sl])), a)
        v2 = v_ref[rows, pv_sl]
        zv = jnp.zeros((CHUNK, GLA_DV), BF16)
        v_bd = jnp.concatenate([jnp.concatenate([v2[:, :GLA_DV], zv], axis=1),
                                jnp.concatenate([zv, v2[:, GLA_DV:]], axis=1)], axis=0)
        inter = _dot_t(per_head_rows(q_in[:, pl_sl]), st.astype(BF16))
        o_ref[rows, pv_sl] = (_dot(a.astype(BF16), v_bd)
                              + jnp.concatenate([inter[:CHUNK], inter[CHUNK:]], axis=1))
        upd = _tdot(v2, k_out[:, pl_sl])
        st_ref[d, p] = st * jnp.exp(tail[:, pl_sl]) + jnp.where(low, upd[:GLA_DV], upd[GLA_DV:])


def _gla_kernel(cm_ref, lv_ref, qf, kf, vf, gf, qb, kb, vb, gb, of_ref, ob_ref, st_ref):
    @pl.when(pl.program_id(1) == 0)
    def _():
        st_ref[...] = jnp.zeros_like(st_ref)

    n = qf.shape[0] // CHUNK
    for c in range(n):
        _gla_chunk(cm_ref, lv_ref, qf, kf, vf, gf, of_ref, st_ref, 0, slice(c * CHUNK, (c + 1) * CHUNK))
        cb = n - 1 - c
        _gla_chunk(cm_ref, lv_ref, qb, kb, vb, gb, ob_ref, st_ref, 1, slice(cb * CHUNK, (cb + 1) * CHUNK))


def _gla(gq, gk, gv, gf, gb, *, batch, n_lat, n_ctx):
    t = gq.shape[0]
    rows = GLA_STEP_CHUNKS * CHUNK
    cl, cc = n_lat // rows, n_ctx // rows
    ctx0 = batch * cl
    cm_np, lv_np = _gla_consts()
    cm = jnp.asarray(cm_np, BF16)
    lv = jnp.asarray(lv_np)

    def fwd(b, s):
        return (jnp.where(s < cc, ctx0 + b * cc + s, b * cl + s - cc), 0)

    def bwd(b, s):
        return (jnp.where(s < cc, ctx0 + b * cc + (cc - 1 - s), b * cl + (cl - 1 - (s - cc))), 0)

    def specs(m):
        return [pl.BlockSpec((rows, GLA_K_W), m), pl.BlockSpec((rows, GLA_K_W), m),
                pl.BlockSpec((rows, GLA_V_W), m), pl.BlockSpec((rows, GLA_K_W), m)]

    return pl.pallas_call(
        _gla_kernel,
        grid=(batch, cc + cl),
        in_specs=[pl.BlockSpec(cm.shape, lambda b, s: (0, 0, 0)), pl.BlockSpec(lv.shape, lambda b, s: (0, 0, 0))]
        + specs(fwd) + specs(bwd),
        out_specs=[pl.BlockSpec((rows, GLA_V_W), fwd), pl.BlockSpec((rows, GLA_V_W), bwd)],
        out_shape=[jax.ShapeDtypeStruct((t, GLA_V_W), F32)] * 2,
        scratch_shapes=[pltpu.VMEM((2, GLA_HEADS // 2, GLA_DV, LANES), F32)],
        compiler_params=pltpu.CompilerParams(
            dimension_semantics=("parallel", "arbitrary"), vmem_limit_bytes=VMEM_LIMIT),
        name="gla",
    )(cm, lv, gq, gk, gv, gf, gq, gk, gv, gb)


def _outproj_kernel(*refs, split_at):
    if split_at is None:
        x_ref, mod_ref, att_ref, of_ref, ob_ref, gr_ref, ym_ref, gg_ref, w_ref, o_ref = refs
        att = att_ref[...]
    else:
        x_ref, mod_ref, att_ref, attc_ref, of_ref, ob_ref, gr_ref, ym_ref, gg_ref, w_ref, o_ref = refs
        att = jnp.where(pl.program_id(0) >= split_at, attc_ref[...], att_ref[...])
    o = of_ref[...] + ob_ref[...]
    r = gr_ref[...]
    mix = [att]
    for hh in range(GLA_HEADS):
        sl = slice(hh * GLA_DV, (hh + 1) * GLA_DV)
        mix.append((_rms(o[:, sl], gg_ref[:, sl]) * _silu(r[:, sl])).astype(BF16))
    mix.append(ym_ref[...])
    y = _dot(jnp.concatenate(mix, axis=1), w_ref[...])
    o_ref[...] = x_ref[...] + mod_ref[0, 5:6, :] * y


def _outproj(xs, mod, att, att_c, o_f, o_b, gr, ym, gla_g, w_out, *, layer, n_tiles, mod_row):
    t, d = xs.shape
    row = lambda i: (i, 0)
    const2 = lambda i: (0, 0)
    if att_c is None:
        split_at = None
        att_specs = [pl.BlockSpec((TM, ATTN_Q_W), row)]
        att_args = (att,)
    else:
        split_at = att.shape[0] // TM
        att_specs = [pl.BlockSpec((TM, ATTN_Q_W), lambda i: (jnp.minimum(i, split_at - 1), 0)),
                     pl.BlockSpec((TM, ATTN_Q_W), lambda i: (jnp.maximum(i - split_at, 0), 0))]
        att_args = (att, att_c)
    return pl.pallas_call(
        functools.partial(_outproj_kernel, split_at=split_at),
        grid=(n_tiles,),
        in_specs=[
            pl.BlockSpec((TM, d), row),
            pl.BlockSpec((1, N_MOD, d), lambda i: (mod_row(i), 0, 0))] + att_specs + [
            pl.BlockSpec((TM, GLA_V_W), row),
            pl.BlockSpec((TM, GLA_V_W), row),
            pl.BlockSpec((TM, GLA_V_W), row),
            pl.BlockSpec((TM, GMLP_W), row),
            pl.BlockSpec((1, GLA_V_W), const2),
            pl.BlockSpec((None,) + w_out.shape[1:], lambda i: (layer, 0, 0), pipeline_mode=pl.Buffered(1)),
        ],
        out_specs=pl.BlockSpec((TM, d), row),
        out_shape=jax.ShapeDtypeStruct((n_tiles * TM, d), F32),
        compiler_params=pltpu.CompilerParams(
            dimension_semantics=("parallel",), vmem_limit_bytes=VMEM_LIMIT),
        name="outproj",
    )(xs, mod, *att_args, o_f, o_b, gr, ym, gla_g.reshape(1, GLA_V_W), w_out)


def _rope_tables(n_lat):
    rows = n_lat // GRID_W
    nf = HEAD_DIM // 4
    inv = ROPE_THETA ** (-jnp.arange(nf, dtype=F32) / nf)
    ar = jnp.arange(rows, dtype=F32)[:, None] * inv
    ac = jnp.arange(GRID_W, dtype=F32)[:, None] * inv
    per_row = lambda z: jnp.repeat(z, GRID_W, axis=0)
    per_col = lambda z: jnp.tile(z, (rows, 1))
    cos_r, sin_r, cos_c, sin_c = per_row(jnp.cos(ar)), per_row(jnp.sin(ar)), per_col(jnp.cos(ac)), per_col(jnp.sin(ac))
    cos = jnp.concatenate([cos_r, cos_r, cos_c, cos_c], axis=-1)
    sin = jnp.concatenate([-sin_r, sin_r, -sin_c, sin_c], axis=-1)
    cos = jnp.concatenate([cos, jnp.ones((TM, HEAD_DIM), F32)], axis=0)
    sin = jnp.concatenate([sin, jnp.zeros((TM, HEAD_DIM), F32)], axis=0)
    return cos, sin


def kernel(x, c, ctx, c_ctx, mod_w, mod_b, norm_g, ffn1_w_gu, ffn1_w_down, ffn2_w_gu, ffn2_w_down, w_in, w_out,
           qk_norm_g, gla_gate_w, gla_gate_b, gla_norm_g, gmlp_w_s, gmlp_b_s, gmlp_norm_g, final_norm_g):
    batch, n_lat, d = x.shape
    n_ctx = ctx.shape[1]
    depth = mod_w.shape[0]
    assert n_lat % TM == 0 and (batch * n_ctx) % TM == 0 and n_lat % TKA == 0 and n_lat % TQ == 0
    assert n_ctx % (GLA_STEP_CHUNKS * CHUNK) == 0 and n_lat % n_ctx == 0 and batch + 1 <= 8 and TM == TK and TK % n_ctx == 0
    lat_tiles = batch * n_lat // TM
    all_tiles = lat_tiles + batch * n_ctx // TM
    tiles_per_batch = n_lat // TM

    def mod_row(i):
        return jnp.minimum(i // tiles_per_batch, batch)

    tm_ffn = TM_FFN if n_lat % TM_FFN == 0 else TM
    ffn_lat_tiles = batch * n_lat // tm_ffn
    ffn_all_tiles = ffn_lat_tiles + pl.cdiv(batch * n_ctx, tm_ffn)

    def mod_row_ffn(i):
        return jnp.minimum(i // (n_lat // tm_ffn), batch)

    def rope_row(i):
        return jnp.where(i < lat_tiles, i % tiles_per_batch, tiles_per_batch)

    cc = jnp.zeros((8, d), F32).at[:batch].set(c).at[batch].set(c_ctx)
    mod_all = _modulation(cc, mod_w, mod_b).reshape(depth, 8, N_MOD, d)
    cos_t, sin_t = _rope_tables(n_lat)
    offs = np.cumsum([0, ATTN_Q_W, ATTN_KV_W, ATTN_KV_W, GLA_K_W, GLA_K_W, GLA_V_W, GLA_V_W, 2 * GLA_GATE_RANK,
                      GMLP_W, GMLP_W])
    lr0, lr1 = int(offs[7]), int(offs[8])
    w_head = w_in[:, :, :lr0].astype(BF16)
    w_tail = jnp.concatenate(
        [w_in[:, :, lr1:], w_in[:, :, lr0:lr1],
         jnp.zeros((depth, d, GLR_PAD - 2 * GLA_GATE_RANK), w_in.dtype)], axis=-1).astype(BF16)
    w_out_b = w_out.astype(BF16)
    f1gu, f1d = ffn1_w_gu.astype(BF16), ffn1_w_down.astype(BF16)
    f2gu, f2d = ffn2_w_gu.astype(BF16), ffn2_w_down.astype(BF16)
    xs = x.reshape(batch * n_lat, d)
    xc = ctx.reshape(batch * n_ctx, d)

    for l in range(depth):
        last = l == depth - 1
        mod = mod_all[l]
        gate_w_r = jnp.zeros((GLR_PAD, 2 * GLA_K_W), F32)
        gate_w_r = gate_w_r.at[:GLA_GATE_RANK, :GLA_K_W].set(gla_gate_w[l, 0])
        gate_w_r = gate_w_r.at[GLA_GATE_RANK:2 * GLA_GATE_RANK, GLA_K_W:].set(gla_gate_w[l, 1]).astype(BF16)
        gate_b_r = gla_gate_b[l].reshape(1, 2 * GLA_K_W)
        bs_b = jnp.broadcast_to(gmlp_b_s[l][..., None], gmlp_b_s.shape[1:] + (GMLP_GROUP_DIM,))

        if xc is not None:
            xs = _ffn(xs, mod, norm_g[l, 0], f1gu, f1d, final_norm_g, layer=l, sub=0, tm=TM,
                      n_tiles=all_tiles, mod_row=mod_row, final=False, xc=xc)
        else:
            xs = _ffn(xs, mod, norm_g[l, 0], f1gu, f1d, final_norm_g, layer=l, sub=0, tm=tm_ffn,
                      n_tiles=ffn_all_tiles, mod_row=mod_row_ffn, final=False)
        xc = None
        q, k, vt, gq, gk, gv, gr, gf, gb, ym = _inproj(
            xs, mod, norm_g[l, 1], w_head, w_tail, qk_norm_g[l], cos_t, sin_t, gate_w_r, gate_b_r,
            gmlp_w_s[l].astype(BF16), bs_b, gmlp_norm_g[l], layer=l, mod_row=mod_row, rope_row=rope_row)
        att = _attention(q, k, vt, batch=batch, n_lat=n_lat, n_ctx=n_ctx, latent=True)
        o_f, o_b = _gla(gq, gk, gv, gf, gb, batch=batch, n_lat=n_lat, n_ctx=n_ctx)
        att_c = None if last else _attention(q, k, vt, batch=batch, n_lat=n_lat, n_ctx=n_ctx, latent=False)
        n_tiles = lat_tiles if last else all_tiles
        xs = _outproj(xs, mod, att, att_c, o_f, o_b, gr, ym, gla_norm_g[l], w_out_b,
                      layer=l, n_tiles=n_tiles, mod_row=mod_row)
        xs = _ffn(xs, mod, norm_g[l, 2], f2gu, f2d, final_norm_g, layer=l, sub=2, tm=tm_ffn,
                  n_tiles=ffn_lat_tiles if last else ffn_all_tiles, mod_row=mod_row_ffn, final=last)
    return xs.reshape(batch, n_lat, d)
```

```python
import functools

import numpy as np
import jax
import jax.numpy as jnp
from jax import lax
from jax.experimental import pallas as pl
from jax.experimental.pallas import tpu as pltpu

F32 = jnp.float32
BF16 = jnp.bfloat16

EPS = 1e-6
N_MOD = 9
HEAD_DIM = 128
ATTN_HEADS = 8
ATTN_KV_HEADS = 2
ATTN_GROUP = ATTN_HEADS // ATTN_KV_HEADS
ROPE_THETA = 10000.0
GRID_W = 64
GLA_HEADS = 4
GLA_DK = 64
GLA_DV = 128
GLA_GATE_RANK = 16
GLA_TAU = 16.0
LOG2E = 1.4426950408889634
CHUNK = 128
GMLP_GROUPS = 4
GMLP_GROUP_DIM = 128

ATTN_Q_W = ATTN_HEADS * HEAD_DIM
ATTN_KV_W = ATTN_KV_HEADS * HEAD_DIM
GLA_K_W = GLA_HEADS * GLA_DK
GLA_V_W = GLA_HEADS * GLA_DV
GMLP_W = GMLP_GROUPS * GMLP_GROUP_DIM
LANES = 128
GLR_PAD = LANES
N_LEVELS = 7

OFF_AQ = 0
OFF_AK = OFF_AQ + ATTN_Q_W
OFF_AV = OFF_AK + ATTN_KV_W
OFF_GQ = OFF_AV + ATTN_KV_W
OFF_GK = OFF_GQ + GLA_K_W
OFF_GV = OFF_GK + GLA_K_W
OFF_GR = OFF_GV + GLA_V_W
OFF_MU = OFF_GR + GLA_V_W
OFF_MV = OFF_MU + GMLP_W
OFF_LR = OFF_MV + GMLP_W
IN_W_R = OFF_LR + GLR_PAD

TM = 512
TF = 512
TM_FFN = 512
FFN_SUB = 512
TQ = 256
TK = 512
TKA = 512
GLA_STEP_CHUNKS = 2
SUM_ROWS = 16
ATTN_UNROLL = 32
MAX_SAFE_JUMP = 60.0
MOD_TN = 2048
VMEM_LIMIT = 56 * 1024 * 1024


def _sigmoid(x):
    return 1.0 / (1.0 + jnp.exp(-x))


def _silu(x):
    return x * _sigmoid(x)


def _rms(x, g):
    ms = jnp.mean(x * x, axis=-1, keepdims=True)
    return x * lax.rsqrt(ms + EPS) * g


def _norm_mod(x, g, mod_ref, i):
    ms = jnp.mean(x * x, axis=-1, keepdims=True)
    gain = g * (1.0 + mod_ref[0, 3 * i + 1:3 * i + 2, :])
    return x * lax.rsqrt(ms + EPS) * gain + mod_ref[0, 3 * i:3 * i + 1, :]


def _dot(a, b):
    return jnp.dot(a, b, preferred_element_type=F32)


def _dot_t(a, b):
    return lax.dot_general(a, b, (((1,), (1,)), ((), ())), preferred_element_type=F32)


def _tdot(a, b):
    return lax.dot_general(a, b, (((0,), (0,)), ((), ())), preferred_element_type=F32)


def _mod_kernel(c_ref, w_ref, b_ref, o_ref):
    sc = _silu(c_ref[...]).astype(BF16)
    o_ref[0] = _dot(sc, w_ref[0].astype(BF16)) + b_ref[0]


def _modulation(cc, mod_w, mod_b):
    depth, d, n = mod_w.shape
    return pl.pallas_call(
        _mod_kernel,
        grid=(depth, n // MOD_TN),
        in_specs=[
            pl.BlockSpec((8, d), lambda l, j: (0, 0)),
            pl.BlockSpec((1, d, MOD_TN), lambda l, j: (l, 0, j)),
            pl.BlockSpec((1, 1, MOD_TN), lambda l, j: (l, 0, j)),
        ],
        out_specs=pl.BlockSpec((1, 8, MOD_TN), lambda l, j: (l, 0, j)),
        out_shape=jax.ShapeDtypeStruct((depth, 8, n), F32),
        compiler_params=pltpu.CompilerParams(
            dimension_semantics=("parallel", "parallel"), vmem_limit_bytes=VMEM_LIMIT),
        name="modulation",
    )(cc, mod_w, mod_b.reshape(depth, 1, n))


def _ffn_kernel(*refs, sub, final, split_at):
    if split_at is None:
        x_ref, mod_ref, g_ref, wg_ref, wu_ref, wd_ref, fg_ref, o_ref, h_ref = refs
        sources = [(None, x_ref)]
    else:
        x_ref, xc_ref, mod_ref, g_ref, wg_ref, wu_ref, wd_ref, fg_ref, o_ref, h_ref = refs
        is_ctx = pl.program_id(0) >= split_at
        sources = [(jnp.logical_not(is_ctx), x_ref), (is_ctx, xc_ref)]
    j = pl.program_id(1)
    last_j = pl.num_programs(1) - 1
    subs = [slice(r, r + FFN_SUB) for r in range(0, o_ref.shape[0], FFN_SUB)]

    def prologue(src_ref):
        for rs in subs:
            h_ref[rs, :] = _norm_mod(src_ref[rs, :], g_ref[...], mod_ref, sub).astype(BF16)

    def epilogue(src_ref):
        for rs in subs:
            y = src_ref[rs, :] + mod_ref[0, 3 * sub + 2:3 * sub + 3, :] * (0.5 * o_ref[rs, :])
            if final:
                y = _rms(y, fg_ref[...])
            o_ref[rs, :] = y

    for cond, src_ref in sources:
        first = j == 0
        pl.when(first if cond is None else jnp.logical_and(first, cond))(functools.partial(prologue, src_ref))

    for rs in subs:
        h = h_ref[rs, :]
        hw = TF // 2
        parts = [(_silu(_dot(h, wg_ref[:, s0:s0 + hw])) * _dot(h, wu_ref[:, s0:s0 + hw])).astype(BF16)
                 for s0 in range(0, TF, hw)]
        for c in range(0, o_ref.shape[1], TF):
            d = _dot(parts[0], wd_ref[0:hw, c:c + TF]) + _dot(parts[1], wd_ref[hw:TF, c:c + TF])
            o_ref[rs, c:c + TF] = jnp.where(j == 0, d, o_ref[rs, c:c + TF] + d)

    for cond, src_ref in sources:
        last = j == last_j
        pl.when(last if cond is None else jnp.logical_and(last, cond))(functools.partial(epilogue, src_ref))


def _ffn(xs, mod, g, w_gu, w_down, final_g, *, layer, sub, tm, n_tiles, mod_row, final, xc=None):
    t, d = xs.shape
    f = w_down.shape[1]
    nf = f // TF
    rows_out = t if xc is None else t + xc.shape[0]
    rows_out = min(rows_out, n_tiles * tm)
    if xc is None:
        split_at = None
        x_specs = [pl.BlockSpec((tm, d), lambda i, j: (i, 0))]
        x_args = (xs,)
    else:
        split_at = t // tm
        x_specs = [pl.BlockSpec((tm, d), lambda i, j: (jnp.minimum(i, split_at - 1), 0)),
                   pl.BlockSpec((tm, d), lambda i, j: (jnp.maximum(i - split_at, 0), 0))]
        x_args = (xs, xc)
    kern = functools.partial(_ffn_kernel, sub=sub, final=final, split_at=split_at)
    return pl.pallas_call(
        kern,
        grid=(n_tiles, nf),
        in_specs=x_specs + [
            pl.BlockSpec((1, N_MOD, d), lambda i, j: (mod_row(i), 0, 0)),
            pl.BlockSpec((1, d), lambda i, j: (0, 0)),
            pl.BlockSpec((None, d, TF), lambda i, j: (layer, 0, j)),
            pl.BlockSpec((None, d, TF), lambda i, j: (layer, 0, j + nf)),
            pl.BlockSpec((None, TF, d), lambda i, j: (layer, j, 0)),
            pl.BlockSpec((1, d), lambda i, j: (0, 0)),
        ],
        out_specs=pl.BlockSpec((tm, d), lambda i, j: (i, 0)),
        out_shape=jax.ShapeDtypeStruct((rows_out, d), F32),
        scratch_shapes=[pltpu.VMEM((tm, d), BF16)],
        compiler_params=pltpu.CompilerParams(
            dimension_semantics=("parallel", "arbitrary"), vmem_limit_bytes=VMEM_LIMIT),
        name="ffn_final" if final else "ffn",
    )(*x_args, mod, g.reshape(1, d), w_gu, w_gu, w_down, final_g.reshape(1, d))


def _rope(x, cos, sin_signed, lane_low):
    partner = jnp.where(lane_low, pltpu.roll(x, LANES - 32, 1), pltpu.roll(x, 32, 1))
    return x * cos + partner * sin_signed


def _log_sigmoid(x):
    return jnp.minimum(x, 0.0) - jnp.log(1.0 + jnp.exp(-jnp.abs(x)))


def _inproj_kernel(x_ref, mod_ref, g_ref, w_ref, wt_ref, qkg_ref, cos_ref, sin_ref, gw_ref, gbias_ref,
                   ws_ref, bs_ref, gmg_ref,
                   q_ref, k_ref, vt_ref, gq_ref, gk_ref, gv_ref, gr_ref, gf_ref, gb_ref, ym_ref):
    _inproj_rows(slice(0, x_ref.shape[0]), x_ref, mod_ref, g_ref, w_ref, wt_ref, qkg_ref, cos_ref, sin_ref,
                 gw_ref, gbias_ref, ws_ref, bs_ref, gmg_ref,
                 q_ref, k_ref, vt_ref, gq_ref, gk_ref, gv_ref, gr_ref, gf_ref, gb_ref, ym_ref)


def _inproj_rows(rs, x_ref, mod_ref, g_ref, w_ref, wt_ref, qkg_ref, cos_ref, sin_ref, gw_ref, gbias_ref,
                 ws_ref, bs_ref, gmg_ref,
                 q_ref, k_ref, vt_ref, gq_ref, gk_ref, gv_ref, gr_ref, gf_ref, gb_ref, ym_ref):
    h = _norm_mod(x_ref[rs, :], g_ref[...], mod_ref, 1).astype(BF16)
    cos = cos_ref[rs, :]
    sin = sin_ref[rs, :]
    lane = lax.broadcasted_iota(jnp.int32, cos.shape, 1)
    lane_low = (lane & 63) < 32

    def proj(off, width):
        if off < OFF_MU:
            return _dot(h, w_ref[:, off:off + width])
        return _dot(h, wt_ref[:, off - OFF_MU:off - OFF_MU + width])

    scale = HEAD_DIM ** -0.5 * LOG2E
    zq = proj(OFF_AQ, ATTN_Q_W)
    for hh in range(ATTN_HEADS):
        sl = slice(hh * HEAD_DIM, (hh + 1) * HEAD_DIM)
        qh = _rope(_rms(zq[:, sl], qkg_ref[0:1, :]), cos, sin, lane_low)
        q_ref[rs, sl] = (qh * scale).astype(BF16)
    zk = proj(OFF_AK, ATTN_KV_W)
    for hh in range(ATTN_KV_HEADS):
        sl = slice(hh * HEAD_DIM, (hh + 1) * HEAD_DIM)
        k_ref[rs, sl] = _rope(_rms(zk[:, sl], qkg_ref[1:2, :]), cos, sin, lane_low).astype(BF16)
    vt_ref[0, :, rs] = proj(OFF_AV, ATTN_KV_W).T.astype(BF16)

    gq_ref[rs, :] = proj(OFF_GQ, GLA_K_W) * (GLA_DK ** -0.5)
    gk_ref[rs, :] = proj(OFF_GK, GLA_K_W)
    gv_ref[rs, :] = proj(OFF_GV, GLA_V_W).astype(BF16)
    gr_ref[rs, :] = proj(OFF_GR, GLA_V_W)
    lr = proj(OFF_LR, GLR_PAD).astype(BF16)
    logits = _dot(lr, gw_ref[...]) + gbias_ref[...]
    ld = _log_sigmoid(logits) * (1.0 / GLA_TAU)
    gf_ref[rs, :] = ld[:, :GLA_K_W]
    gb_ref[rs, :] = ld[:, GLA_K_W:]

    mu = proj(OFF_MU, GMLP_W)
    vn = _rms(proj(OFF_MV, GMLP_W), gmg_ref[...]).astype(BF16)
    for c in range((rs.stop - rs.start) // CHUNK):
        rows = slice(c * CHUNK, (c + 1) * CHUNK)
        out_rows = slice(rs.start + c * CHUNK, rs.start + (c + 1) * CHUNK)
        for gi in range(GMLP_GROUPS):
            cols = slice(gi * GMLP_GROUP_DIM, (gi + 1) * GMLP_GROUP_DIM)
            z = _dot(ws_ref[gi], vn[rows, cols]) + bs_ref[gi]
            ym_ref[out_rows, cols] = (mu[rows, cols] * z).astype(BF16)


def _inproj(xs, mod, g, w_head, w_tail, qk_g, cos_t, sin_t, gate_w_r, gate_b_r, ws, bs_b, gm_g, *, layer, mod_row,
            rope_row):
    t, d = xs.shape
    n_tiles = t // TM
    row = lambda i: (i, 0)
    const2 = lambda i: (0, 0)
    const3 = lambda i: (0, 0, 0)
    widths = [(ATTN_Q_W, BF16), (ATTN_KV_W, BF16), None, (GLA_K_W, F32), (GLA_K_W, F32),
              (GLA_V_W, BF16), (GLA_V_W, F32), (GLA_K_W, F32), (GLA_K_W, F32), (GMLP_W, BF16)]
    out_specs = [pl.BlockSpec((1, ATTN_KV_W, TM), lambda i: (i, 0, 0)) if w is None else pl.BlockSpec((TM, w[0]), row)
                 for w in widths]
    out_shape = [jax.ShapeDtypeStruct((n_tiles, ATTN_KV_W, TM), BF16) if w is None
                 else jax.ShapeDtypeStruct((t, w[0]), w[1]) for w in widths]
    return pl.pallas_call(
        _inproj_kernel,
        grid=(n_tiles,),
        in_specs=[
            pl.BlockSpec((TM, d), row),
            pl.BlockSpec((1, N_MOD, d), lambda i: (mod_row(i), 0, 0)),
            pl.BlockSpec((1, d), const2),
            pl.BlockSpec((None, d, OFF_MU), lambda i: (layer, 0, 0), pipeline_mode=pl.Buffered(1)),
            pl.BlockSpec((None, d, IN_W_R - OFF_MU), lambda i: (layer, 0, 0), pipeline_mode=pl.Buffered(1)),
            pl.BlockSpec((2, HEAD_DIM), const2),
            pl.BlockSpec((TM, HEAD_DIM), lambda i: (rope_row(i), 0)),
            pl.BlockSpec((TM, HEAD_DIM), lambda i: (rope_row(i), 0)),
            pl.BlockSpec((GLR_PAD, 2 * GLA_K_W), const2),
            pl.BlockSpec((1, 2 * GLA_K_W), const2),
            pl.BlockSpec((GMLP_GROUPS, CHUNK, CHUNK), const3),
            pl.BlockSpec((GMLP_GROUPS, CHUNK, GMLP_GROUP_DIM), const3),
            pl.BlockSpec((1, GMLP_W), const2),
        ],
        out_specs=out_specs,
        out_shape=out_shape,
        compiler_params=pltpu.CompilerParams(
            dimension_semantics=("parallel",), vmem_limit_bytes=VMEM_LIMIT,
            allow_input_fusion=[False, False, False, True, True] + [False] * 8),
        name="inproj",
    )(xs, mod, g.reshape(1, d), w_head, w_tail, qk_g, cos_t, sin_t, gate_w_r, gate_b_r, ws, bs_b,
      gm_g.reshape(1, GMLP_W))


def _attn_kernel(*refs, n_lat_tiles):
    if n_lat_tiles:
        q_ref, kc_ref, vtc_ref, kl_ref, vtl_ref, o_ref, s_ref, off_ref, jump_ref, m_ref, acc_ref = refs
    else:
        q_ref, kc_ref, vtc_ref, o_ref, m_ref, acc_ref = refs
    tq = q_ref.shape[0]
    q = jnp.concatenate([q_ref[:, g * HEAD_DIM:(g + 1) * HEAD_DIM] for g in range(ATTN_GROUP)], axis=0)

    def scores(k):
        return _dot_t(k, q)

    def with_ones(vt):
        return jnp.concatenate([vt, jnp.ones((SUM_ROWS, vt.shape[1]), BF16)], axis=0)

    def lat_keys(t):
        start = pl.multiple_of(t * TKA, TKA)
        return kl_ref[pl.ds(start, TKA), :]

    def lat_values(t):
        n = TKA // TK
        return jnp.concatenate([vtl_ref[n * t + u] for u in range(n)], axis=1) if n > 1 else vtl_ref[t]

    def ctx_tile():
        s = scores(kc_ref[...])
        m_new = jnp.max(s, axis=0, keepdims=True)
        m_ref[...] = m_new
        acc_ref[...] = _dot(with_ones(vtc_ref[0]), jnp.exp2(s - m_new).astype(BF16))

    def lagged_tile(t, carry):
        c = m_ref[...]
        s = scores(lat_keys(t))
        tmax = jnp.max(s, axis=0, keepdims=True)
        p = jnp.exp2(s - c)
        alpha = jnp.exp2(off_ref[...] - c)
        acc_ref[...] = alpha * acc_ref[...] + _dot(with_ones(lat_values(t)), p.astype(BF16))
        off_ref[...] = c
        jump_ref[...] = jnp.maximum(jump_ref[...], tmax - c)
        m_ref[...] = jnp.maximum(c, tmax)
        return carry

    def exact_tile(t, carry):
        s_ref[...] = scores(lat_keys(t))
        m_old = m_ref[...]
        m_new = jnp.maximum(m_old, jnp.max(s_ref[...], axis=0, keepdims=True))
        alpha = jnp.exp2(m_old - m_new)
        p = jnp.exp2(s_ref[...] - m_new)
        acc_ref[...] = alpha * acc_ref[...] + _dot(with_ones(lat_values(t)), p.astype(BF16))
        m_ref[...] = m_new
        return carry

    ctx_tile()
    if n_lat_tiles:
        off_ref[...] = m_ref[...]
        jump_ref[...] = jnp.zeros_like(jump_ref)
        lax.fori_loop(0, n_lat_tiles, lagged_tile, 0, unroll=ATTN_UNROLL if n_lat_tiles % ATTN_UNROLL == 0 else 1)

        @pl.when(jnp.max(jump_ref[...]) > MAX_SAFE_JUMP)
        def _():
            ctx_tile()
            lax.fori_loop(0, n_lat_tiles, exact_tile, 0)
    out_t = acc_ref[0:HEAD_DIM, :] / acc_ref[HEAD_DIM:HEAD_DIM + 1, :]
    for g in range(ATTN_GROUP):
        o_ref[:, g * HEAD_DIM:(g + 1) * HEAD_DIM] = out_t[:, g * tq:(g + 1) * tq].T.astype(BF16)


def _attention(q, k, vt, *, batch, n_lat, n_ctx, latent):
    gw = ATTN_GROUP * HEAD_DIM
    ctx_blk0 = (batch * n_lat) // n_ctx
    lat_tiles = (batch * n_lat) // TK
    per_tile = TK // n_ctx
    kc_spec = pl.BlockSpec((n_ctx, HEAD_DIM), lambda b, kh, i: (ctx_blk0 + b, kh))
    vtc_spec = pl.BlockSpec((1, HEAD_DIM, n_ctx), lambda b, kh, i: (lat_tiles + b // per_tile, kh, b % per_tile))
    if latent:
        tq = TQ
        nq = n_lat // tq
        n_lat_tiles = n_lat // TKA
        q_spec = pl.BlockSpec((tq, gw), lambda b, kh, i: (b * nq + i, kh))
        kl_spec = pl.BlockSpec((n_lat, HEAD_DIM), lambda b, kh, i: (b, kh))
        vtl_spec = pl.BlockSpec((n_lat // TK, HEAD_DIM, TK), lambda b, kh, i: (b, kh, 0))
        in_specs = [q_spec, kc_spec, vtc_spec, kl_spec, vtl_spec]
        args = (q, k, vt, k, vt)
        out_rows = batch * n_lat
    else:
        tq = n_ctx
        nq = 1
        n_lat_tiles = 0
        q_spec = pl.BlockSpec((tq, gw), lambda b, kh, i: (ctx_blk0 + b, kh))
        in_specs = [q_spec, kc_spec, vtc_spec]
        args = (q, k, vt)
        out_rows = batch * n_ctx
    cols = ATTN_GROUP * tq
    scratch = [pltpu.VMEM((1, cols), F32), pltpu.VMEM((HEAD_DIM + SUM_ROWS, cols), F32)]
    if latent:
        scratch = [pltpu.VMEM((TKA, cols), F32), pltpu.VMEM((1, cols), F32), pltpu.VMEM((1, cols), F32)] + scratch
    return pl.pallas_call(
        functools.partial(_attn_kernel, n_lat_tiles=n_lat_tiles),
        grid=(batch, ATTN_KV_HEADS, nq),
        in_specs=in_specs,
        out_specs=pl.BlockSpec((tq, gw), lambda b, kh, i: (b * nq + i, kh)),
        out_shape=jax.ShapeDtypeStruct((out_rows, ATTN_Q_W), BF16),
        scratch_shapes=scratch,
        compiler_params=pltpu.CompilerParams(
            dimension_semantics=("parallel", "parallel", "arbitrary"), vmem_limit_bytes=VMEM_LIMIT),
        name="attn_lat" if latent else "attn_ctx",
    )(*args)


def _gla_consts():
    idx = np.arange(CHUNK)
    tri = (idx[None, :] <= idx[:, None]).astype(np.float32)
    mats_f, mats_b = [tri], [tri.T]
    for lvl in range(1, N_LEVELS + 1):
        s = (2 * CHUNK) >> lvl
        base = (idx // s) * s
        mats_f.append(tri[base + s // 2 - 1])
        mats_b.append(tri.T[base + s // 2])
    cm = np.stack([np.concatenate(mats_f, 0), np.concatenate(mats_b, 0)])
    cm = np.concatenate([cm, cm], axis=-1)
    x = idx[:, None] ^ idx[None, :]
    hb = np.floor(np.log2(np.maximum(x, 1))).astype(np.int32)
    lv = np.where(x == 0, 0, N_LEVELS - hb)
    lv_f = np.where(idx[:, None] >= idx[None, :], lv, -1)
    lv_b = np.where(idx[:, None] <= idx[None, :], lv, -1)
    return cm, np.stack([lv_f, lv_b]).astype(np.int32)


def _gla_chunk(cm_ref, lv_ref, q_ref, k_ref, v_ref, g_ref, o_ref, st_ref, d, rows):
    g = g_ref[rows, :]
    g_hi = g.astype(BF16)
    g_lo = (g - g_hi.astype(F32)).astype(BF16)
    cums = _dot(cm_ref[d], jnp.concatenate([g_hi, g_lo], axis=0))
    cum = cums[0:CHUNK]
    q = q_ref[rows, :]
    k = k_ref[rows, :]
    lv = lv_ref[d]
    last = CHUNK - 1 if d == 0 else 0
    tail = cum[last:last + 1, :]
    lane = lax.broadcasted_iota(jnp.int32, (CHUNK, LANES), 1)
    low = lane < GLA_DK

    qs = [q.astype(BF16)]
    ks = [k.astype(BF16)]
    for lvl in range(1, N_LEVELS + 1):
        decay = jnp.exp(-jnp.abs(cum - cums[lvl * CHUNK:(lvl + 1) * CHUNK]))
        qs.append((q * decay).astype(BF16))
        ks.append((k * decay).astype(BF16))
    q_in = (q * jnp.exp(cum)).astype(BF16)
    k_out = (k * jnp.exp(tail - cum)).astype(BF16)
    zero = jnp.zeros((CHUNK, LANES), BF16)
    lv2 = jnp.concatenate([lv, lv], axis=1)

    def per_head_rows(x):
        return jnp.concatenate([jnp.where(low, x, zero), jnp.where(low, zero, x)], axis=0)

    for p in range(GLA_HEADS // 2):
        pl_sl = slice(p * LANES, (p + 1) * LANES)
        pv_sl = slice(2 * p * GLA_DV, 2 * (p + 1) * GLA_DV)
        st = st_ref[d, p]
        a = jnp.zeros((CHUNK, 2 * CHUNK), F32)
        for lvl in range(N_LEVELS + 1):
            a = jnp.where(lv2 == lvl, _dot_t(qs[lvl][:, pl_sl], per_head_rows(ks[lvl][:, pl_sl])), a)
        v2 = v_ref[rows, pv_sl]
        zv = jnp.zeros((CHUNK, GLA_DV), BF16)
        v_bd = jnp.concatenate([jnp.concatenate([v2[:, :GLA_DV], zv], axis=1),
                                jnp.concatenate([zv, v2[:, GLA_DV:]], axis=1)], axis=0)
        inter = _dot_t(per_head_rows(q_in[:, pl_sl]), st.astype(BF16))
        o_ref[rows, pv_sl] = (_dot(a.astype(BF16), v_bd)
                              + jnp.concatenate([inter[:CHUNK], inter[CHUNK:]], axis=1))
        upd = _tdot(v2, k_out[:, pl_sl])
        st_ref[d, p] = st * jnp.exp(tail[:, pl_sl]) + jnp.where(low, upd[:GLA_DV], upd[GLA_DV:])


def _gla_kernel(cm_ref, lv_ref, qf, kf, vf, gf, qb, kb, vb, gb, of_ref, ob_ref, st_ref):
    @pl.when(pl.program_id(1) == 0)
    def _():
        st_ref[...] = jnp.zeros_like(st_ref)

    n = qf.shape[0] // CHUNK
    for c in range(n):
        _gla_chunk(cm_ref, lv_ref, qf, kf, vf, gf, of_ref, st_ref, 0, slice(c * CHUNK, (c + 1) * CHUNK))
        cb = n - 1 - c
        _gla_chunk(cm_ref, lv_ref, qb, kb, vb, gb, ob_ref, st_ref, 1, slice(cb * CHUNK, (cb + 1) * CHUNK))


def _gla(gq, gk, gv, gf, gb, *, batch, n_lat, n_ctx):
    t = gq.shape[0]
    rows = GLA_STEP_CHUNKS * CHUNK
    cl, cc = n_lat // rows, n_ctx // rows
    ctx0 = batch * cl
    cm_np, lv_np = _gla_consts()
    cm = jnp.asarray(cm_np, BF16)
    lv = jnp.asarray(lv_np)

    def fwd(b, s):
        return (jnp.where(s < cc, ctx0 + b * cc + s, b * cl + s - cc), 0)

    def bwd(b, s):
        return (jnp.where(s < cc, ctx0 + b * cc + (cc - 1 - s), b * cl + (cl - 1 - (s - cc))), 0)

    def specs(m):
        return [pl.BlockSpec((rows, GLA_K_W), m), pl.BlockSpec((rows, GLA_K_W), m),
                pl.BlockSpec((rows, GLA_V_W), m), pl.BlockSpec((rows, GLA_K_W), m)]

    return pl.pallas_call(
        _gla_kernel,
        grid=(batch, cc + cl),
        in_specs=[pl.BlockSpec(cm.shape, lambda b, s: (0, 0, 0)), pl.BlockSpec(lv.shape, lambda b, s: (0, 0, 0))]
        + specs(fwd) + specs(bwd),
        out_specs=[pl.BlockSpec((rows, GLA_V_W), fwd), pl.BlockSpec((rows, GLA_V_W), bwd)],
        out_shape=[jax.ShapeDtypeStruct((t, GLA_V_W), F32)] * 2,
        scratch_shapes=[pltpu.VMEM((2, GLA_HEADS // 2, GLA_DV, LANES), F32)],
        compiler_params=pltpu.CompilerParams(
            dimension_semantics=("parallel", "arbitrary"), vmem_limit_bytes=VMEM_LIMIT),
        name="gla",
    )(cm, lv, gq, gk, gv, gf, gq, gk, gv, gb)


def _outproj_kernel(*refs, split_at):
    if split_at is None:
        x_ref, mod_ref, att_ref, of_ref, ob_ref, gr_ref, ym_ref, gg_ref, w_ref, o_ref = refs
        att = att_ref[...]
    else:
        x_ref, mod_ref, att_ref, attc_ref, of_ref, ob_ref, gr_ref, ym_ref, gg_ref, w_ref, o_ref = refs
        att = jnp.where(pl.program_id(0) >= split_at, attc_ref[...], att_ref[...])
    o = of_ref[...] + ob_ref[...]
    r = gr_ref[...]
    mix = [att]
    for hh in range(GLA_HEADS):
        sl = slice(hh * GLA_DV, (hh + 1) * GLA_DV)
        mix.append((_rms(o[:, sl], gg_ref[:, sl]) * _silu(r[:, sl])).astype(BF16))
    mix.append(ym_ref[...])
    y = _dot(jnp.concatenate(mix, axis=1), w_ref[...])
    o_ref[...] = x_ref[...] + mod_ref[0, 5:6, :] * y


def _outproj(xs, mod, att, att_c, o_f, o_b, gr, ym, gla_g, w_out, *, layer, n_tiles, mod_row):
    t, d = xs.shape
    row = lambda i: (i, 0)
    const2 = lambda i: (0, 0)
    if att_c is None:
        split_at = None
        att_specs = [pl.BlockSpec((TM, ATTN_Q_W), row)]
        att_args = (att,)
    else:
        split_at = att.shape[0] // TM
        att_specs = [pl.BlockSpec((TM, ATTN_Q_W), lambda i: (jnp.minimum(i, split_at - 1), 0)),
                     pl.BlockSpec((TM, ATTN_Q_W), lambda i: (jnp.maximum(i - split_at, 0), 0))]
        att_args = (att, att_c)
    return pl.pallas_call(
        functools.partial(_outproj_kernel, split_at=split_at),
        grid=(n_tiles,),
        in_specs=[
            pl.BlockSpec((TM, d), row),
            pl.BlockSpec((1, N_MOD, d), lambda i: (mod_row(i), 0, 0))] + att_specs + [
            pl.BlockSpec((TM, GLA_V_W), row),
            pl.BlockSpec((TM, GLA_V_W), row),
            pl.BlockSpec((TM, GLA_V_W), row),
            pl.BlockSpec((TM, GMLP_W), row),
            pl.BlockSpec((1, GLA_V_W), const2),
            pl.BlockSpec((None,) + w_out.shape[1:], lambda i: (layer, 0, 0), pipeline_mode=pl.Buffered(1)),
        ],
        out_specs=pl.BlockSpec((TM, d), row),
        out_shape=jax.ShapeDtypeStruct((n_tiles * TM, d), F32),
        compiler_params=pltpu.CompilerParams(
            dimension_semantics=("parallel",), vmem_limit_bytes=VMEM_LIMIT),
        name="outproj",
    )(xs, mod, *att_args, o_f, o_b, gr, ym, gla_g.reshape(1, GLA_V_W), w_out)


def _rope_tables(n_lat):
    rows = n_lat // GRID_W
    nf = HEAD_DIM // 4
    inv = ROPE_THETA ** (-jnp.arange(nf, dtype=F32) / nf)
    ar = jnp.arange(rows, dtype=F32)[:, None] * inv
    ac = jnp.arange(GRID_W, dtype=F32)[:, None] * inv
    per_row = lambda z: jnp.repeat(z, GRID_W, axis=0)
    per_col = lambda z: jnp.tile(z, (rows, 1))
    cos_r, sin_r, cos_c, sin_c = per_row(jnp.cos(ar)), per_row(jnp.sin(ar)), per_col(jnp.cos(ac)), per_col(jnp.sin(ac))
    cos = jnp.concatenate([cos_r, cos_r, cos_c, cos_c], axis=-1)
    sin = jnp.concatenate([-sin_r, sin_r, -sin_c, sin_c], axis=-1)
    cos = jnp.concatenate([cos, jnp.ones((TM, HEAD_DIM), F32)], axis=0)
    sin = jnp.concatenate([sin, jnp.zeros((TM, HEAD_DIM), F32)], axis=0)
    return cos, sin


def kernel(x, c, ctx, c_ctx, mod_w, mod_b, norm_g, ffn1_w_gu, ffn1_w_down, ffn2_w_gu, ffn2_w_down, w_in, w_out,
           qk_norm_g, gla_gate_w, gla_gate_b, gla_norm_g, gmlp_w_s, gmlp_b_s, gmlp_norm_g, final_norm_g):
    batch, n_lat, d = x.shape
    n_ctx = ctx.shape[1]
    depth = mod_w.shape[0]
    assert n_lat % TM == 0 and (batch * n_ctx) % TM == 0 and n_lat % TKA == 0 and n_lat % TQ == 0
    assert n_ctx % (GLA_STEP_CHUNKS * CHUNK) == 0 and n_lat % n_ctx == 0 and batch + 1 <= 8 and TM == TK and TK % n_ctx == 0
    lat_tiles = batch * n_lat // TM
    all_tiles = lat_tiles + batch * n_ctx // TM
    tiles_per_batch = n_lat // TM

    def mod_row(i):
        return jnp.minimum(i // tiles_per_batch, batch)

    tm_ffn = TM_FFN if n_lat % TM_FFN == 0 else TM
    ffn_lat_tiles = batch * n_lat // tm_ffn
    ffn_all_tiles = ffn_lat_tiles + pl.cdiv(batch * n_ctx, tm_ffn)

    def mod_row_ffn(i):
        return jnp.minimum(i // (n_lat // tm_ffn), batch)

    def rope_row(i):
        return jnp.where(i < lat_tiles, i % tiles_per_batch, tiles_per_batch)

    cc = jnp.zeros((8, d), F32).at[:batch].set(c).at[batch].set(c_ctx)
    mod_all = _modulation(cc, mod_w, mod_b).reshape(depth, 8, N_MOD, d)
    cos_t, sin_t = _rope_tables(n_lat)
    offs = np.cumsum([0, ATTN_Q_W, ATTN_KV_W, ATTN_KV_W, GLA_K_W, GLA_K_W, GLA_V_W, GLA_V_W, 2 * GLA_GATE_RANK,
                      GMLP_W, GMLP_W])
    lr0, lr1 = int(offs[7]), int(offs[8])
    w_head = w_in[:, :, :lr0].astype(BF16)
    w_tail = jnp.concatenate(
        [w_in[:, :, lr1:], w_in[:, :, lr0:lr1],
         jnp.zeros((depth, d, GLR_PAD - 2 * GLA_GATE_RANK), w_in.dtype)], axis=-1).astype(BF16)
    w_out_b = w_out.astype(BF16)
    f1gu, f1d = ffn1_w_gu.astype(BF16), ffn1_w_down.astype(BF16)
    f2gu, f2d = ffn2_w_gu.astype(BF16), ffn2_w_down.astype(BF16)
    xs = x.reshape(batch * n_lat, d)
    xc = ctx.reshape(batch * n_ctx, d)

    for l in range(depth):
        last = l == depth - 1
        mod = mod_all[l]
        gate_w_r = jnp.zeros((GLR_PAD, 2 * GLA_K_W), F32)
        gate_w_r = gate_w_r.at[:GLA_GATE_RANK, :GLA_K_W].set(gla_gate_w[l, 0])
        gate_w_r = gate_w_r.at[GLA_GATE_RANK:2 * GLA_GATE_RANK, GLA_K_W:].set(gla_gate_w[l, 1]).astype(BF16)
        gate_b_r = gla_gate_b[l].reshape(1, 2 * GLA_K_W)
        bs_b = jnp.broadcast_to(gmlp_b_s[l][..., None], gmlp_b_s.shape[1:] + (GMLP_GROUP_DIM,))

        if xc is not None:
            xs = _ffn(xs, mod, norm_g[l, 0], f1gu, f1d, final_norm_g, layer=l, sub=0, tm=TM,
                      n_tiles=all_tiles, mod_row=mod_row, final=False, xc=xc)
        else:
            xs = _ffn(xs, mod, norm_g[l, 0], f1gu, f1d, final_norm_g, layer=l, sub=0, tm=tm_ffn,
                      n_tiles=ffn_all_tiles, mod_row=mod_row_ffn, final=False)
        xc = None
        q, k, vt, gq, gk, gv, gr, gf, gb, ym = _inproj(
            xs, mod, norm_g[l, 1], w_head, w_tail, qk_norm_g[l], cos_t, sin_t, gate_w_r, gate_b_r,
            gmlp_w_s[l].astype(BF16), bs_b, gmlp_norm_g[l], layer=l, mod_row=mod_row, rope_row=rope_row)
        att = _attention(q, k, vt, batch=batch, n_lat=n_lat, n_ctx=n_ctx, latent=True)
        o_f, o_b = _gla(gq, gk, gv, gf, gb, batch=batch, n_lat=n_lat, n_ctx=n_ctx)
        att_c = None if last else _attention(q, k, vt, batch=batch, n_lat=n_lat, n_ctx=n_ctx, latent=False)
        n_tiles = lat_tiles if last else all_tiles
        xs = _outproj(xs, mod, att, att_c, o_f, o_b, gr, ym, gla_norm_g[l], w_out_b,
                      layer=l, n_tiles=n_tiles, mod_row=mod_row)
        xs = _ffn(xs, mod, norm_g[l, 2], f2gu, f2d, final_norm_g, layer=l, sub=2, tm=tm_ffn,
                  n_tiles=ffn_lat_tiles if last else ffn_all_tiles, mod_row=mod_row_ffn, final=last)
    return xs.reshape(batch, n_lat, d)
```
